```python
import math
import jax, jax.numpy as jnp
from jax import lax
import numpy as np

D_MODEL = 1024
BATCH = 8
SEQ = 4096
DEPTH = 1

N_HEADS = 8
HEAD_DIM = 64
N_KV = 2
GQA = N_HEADS // N_KV
CMP_BLOCK = 32
CMP_STRIDE = 16
SLC_BLOCK = 64
N_SLC = 16
WINDOW = 512
Q_BLOCK = 128
SLC_Q_BLOCK = 64
FORCE_SCORE = 1e4
SCALE = 1.0 / math.sqrt(HEAD_DIM)
CONV_WIDTH = D_MODEL
CONV_K = 3
D_FF = 4 * D_MODEL
EPS = 1e-6
NEG_INF = -1e30

QW = N_HEADS * HEAD_DIM
KVW = N_KV * HEAD_DIM
SPLITS = (QW, KVW, KVW, KVW, KVW, KVW, KVW, 3 * N_HEADS,
          CONV_WIDTH, CONV_WIDTH, CONV_WIDTH, D_MODEL, D_MODEL)
PROJ_WIDTH = sum(SPLITS)
SPLIT_POINTS = tuple(int(v) for v in np.cumsum(SPLITS)[:-1])

kernel_name = "hybrid_nsa_shortconv_gated_block"


def rms_norm(x, g):
    xf = x.astype(jnp.float32)
    y = xf * lax.rsqrt(jnp.mean(xf * xf, axis=-1, keepdims=True) + EPS)
    return (y * g.astype(jnp.float32)).astype(x.dtype)


def alibi_slopes():
    s = 2.0 ** (-8.0 * jnp.arange(1, N_HEADS + 1, dtype=jnp.float32) / N_HEADS)
    return s.reshape(N_KV, GQA)


def masked_softmax(s, mask):
    s = jnp.where(mask, s.astype(jnp.float32), NEG_INF)
    m = jnp.max(s, axis=-1, keepdims=True)
    e = jnp.where(mask, jnp.exp(s - m), 0.0)
    return e / jnp.maximum(jnp.sum(e, axis=-1, keepdims=True), 1e-30)


def compress(k, pos_emb, w):
    B_, S_ = k.shape[0], k.shape[1]
    r = CMP_BLOCK // CMP_STRIDE
    c = k.reshape(B_, S_ // CMP_STRIDE, CMP_STRIDE, N_KV, HEAD_DIM)
    nc = S_ // CMP_STRIDE - r + 1
    blocks = jnp.concatenate([c[:, i:i + nc] for i in range(r)], axis=2)
    blocks = blocks + pos_emb[None, None, :, None, :]
    blocks = blocks.transpose(0, 1, 3, 2, 4).reshape(B_, nc, N_KV, CMP_BLOCK * HEAD_DIM)
    return blocks @ w


def hybrid_layer(x, norm1_g, w_in, q_norm_g, k_norm_g, cmp_pos_k, cmp_pos_v,
                 w_cmp_k, w_cmp_v, conv_w, w_branch_a, w_branch_b, w_out,
                 norm2_g, w_up, w_down):
    B_, S_, _ = x.shape
    xn = rms_norm(x, norm1_g)
    proj = xn @ w_in
    (q, k_c, v_c, k_s, v_s, k_w, v_w, g_nsa, conv_b, conv_c, conv_x,
     gate_a, gate_b) = jnp.split(proj, SPLIT_POINTS, axis=-1)

    q = rms_norm(q.reshape(B_, S_, N_HEADS, HEAD_DIM), q_norm_g)
    q = q.reshape(B_, S_, N_KV, GQA, HEAD_DIM)
    kv_shape = (B_, S_, N_KV, HEAD_DIM)
    k_s = rms_norm(k_s.reshape(kv_shape), k_norm_g[1])
    k_w = rms_norm(k_w.reshape(kv_shape), k_norm_g[2])
    v_s = v_s.reshape(kv_shape)
    v_w = v_w.reshape(kv_shape)
    kc = rms_norm(compress(k_c.reshape(kv_shape), cmp_pos_k, w_cmp_k), k_norm_g[0])
    vc = compress(v_c.reshape(kv_shape), cmp_pos_v, w_cmp_v)

    slopes = alibi_slopes()
    t = jnp.arange(S_)

    nc = kc.shape[1]
    c_start = jnp.arange(nc) * CMP_STRIDE
    c_end = c_start + CMP_BLOCK - 1
    dist_c = (t[:, None] - c_end[None, :]).astype(jnp.float32)
    s_c = (jnp.einsum('bsgrd,bcgd->bgrsc', q, kc).astype(jnp.float32) * SCALE
           - slopes[:, :, None, None] * dist_c)
    p_c = masked_softmax(s_c, dist_c >= 0)
    o_cmp = jnp.einsum('bgrsc,bcgd->bsgrd', p_c.astype(vc.dtype), vc)

    ns = S_ // SLC_BLOCK
    s_start = jnp.arange(ns) * SLC_BLOCK
    overlap = jnp.clip(jnp.minimum(c_start[:, None] + CMP_BLOCK, s_start[None, :] + SLC_BLOCK)
                       - jnp.maximum(c_start[:, None], s_start[None, :]), 0, None)
    overlap = overlap.astype(jnp.float32) / CMP_BLOCK
    imp = jnp.einsum('bgrsc,cj->bgsj', p_c, overlap)
    cur = (t // SLC_BLOCK)[:, None]
    j = jnp.arange(ns)[None, :]
    forced = (j == 0) | (j == cur) | (j == cur - 1)
    score = jnp.where(forced, FORCE_SCORE, jnp.where(j <= cur, imp, -1.0))
    n_sel = min(N_SLC, ns)
    _, sel_idx = lax.top_k(score, n_sel)

    kb = k_s.reshape(B_, ns, SLC_BLOCK, N_KV, HEAD_DIM).transpose(0, 3, 1, 2, 4)
    vb = v_s.reshape(B_, ns, SLC_BLOCK, N_KV, HEAD_DIM).transpose(0, 3, 1, 2, 4)
    gather = jax.vmap(jax.vmap(lambda blk, ids: blk[ids]))
    m_sel = n_sel * SLC_BLOCK

    def sel_chunk(i):
        start = i * SLC_Q_BLOCK
        q_i = lax.dynamic_slice_in_dim(q, start, SLC_Q_BLOCK, axis=1)
        idx_i = lax.dynamic_slice_in_dim(sel_idx, start, SLC_Q_BLOCK, axis=2)
        t_i = start + jnp.arange(SLC_Q_BLOCK)
        ks = gather(kb, idx_i).reshape(B_, N_KV, SLC_Q_BLOCK, m_sel, HEAD_DIM)
        vs = gather(vb, idx_i).reshape(B_, N_KV, SLC_Q_BLOCK, m_sel, HEAD_DIM)
        pos = (idx_i[..., None] * SLC_BLOCK + jnp.arange(SLC_BLOCK)).reshape(B_, N_KV, SLC_Q_BLOCK, m_sel)
        dist = (t_i[None, None, :, None] - pos).astype(jnp.float32)
        s = (jnp.einsum('bqgrd,bgqmd->bgrqm', q_i, ks).astype(jnp.float32) * SCALE
             - slopes[None, :, :, None, None] * dist[:, :, None])
        p = masked_softmax(s, (dist >= 0)[:, :, None])
        return jnp.einsum('bgrqm,bgqmd->bqgrd', p.astype(vs.dtype), vs)

    o_slc = lax.map(sel_chunk, jnp.arange(S_ // SLC_Q_BLOCK))
    o_slc = o_slc.transpose(1, 0, 2, 3, 4, 5).reshape(B_, S_, N_KV, GQA, HEAD_DIM)

    kp = jnp.pad(k_w, ((0, 0), (WINDOW, 0), (0, 0), (0, 0)))
    vp = jnp.pad(v_w, ((0, 0), (WINDOW, 0), (0, 0), (0, 0)))
    span = WINDOW + Q_BLOCK

    def win_chunk(i):
        start = i * Q_BLOCK
        q_i = lax.dynamic_slice_in_dim(q, start, Q_BLOCK, axis=1)
        k_i = lax.dynamic_slice_in_dim(kp, start, span, axis=1)
        v_i = lax.dynamic_slice_in_dim(vp, start, span, axis=1)
        t_i = start + jnp.arange(Q_BLOCK)
        s_pos = start - WINDOW + jnp.arange(span)
        dist = t_i[:, None] - s_pos[None, :]
        mask = (dist >= 0) & (dist < WINDOW) & (s_pos[None, :] >= 0)
        s = (jnp.einsum('bqgrd,bkgd->bgrqk', q_i, k_i).astype(jnp.float32) * SCALE
             - slopes[:, :, None, None] * dist.astype(jnp.float32))
        p = masked_softmax(s, mask)
        return jnp.einsum('bgrqk,bkgd->bqgrd', p.astype(v_i.dtype), v_i)

    o_win = lax.map(win_chunk, jnp.arange(S_ // Q_BLOCK))
    o_win = o_win.transpose(1, 0, 2, 3, 4, 5).reshape(B_, S_, N_KV, GQA, HEAD_DIM)

    g = jax.nn.sigmoid(g_nsa.astype(jnp.float32)).astype(x.dtype).reshape(B_, S_, 3, N_KV, GQA, 1)
    o_nsa = (g[:, :, 0] * o_cmp + g[:, :, 1] * o_slc + g[:, :, 2] * o_win).reshape(B_, S_, QW)

    u = conv_c * conv_x
    y = lax.conv_general_dilated(u, conv_w[:, None, :], window_strides=(1,),
                                 padding=[(CONV_K - 1, 0)],
                                 dimension_numbers=('NWC', 'WIO', 'NWC'),
                                 feature_group_count=CONV_WIDTH)
    z = conv_b * y

    mixed = (jax.nn.sigmoid(gate_a) * (o_nsa @ w_branch_a)
             + jax.nn.sigmoid(gate_b) * (z @ w_branch_b))
    x = x + mixed @ w_out

    h = rms_norm(x, norm2_g)
    return x + jnp.square(jax.nn.relu(h @ w_up)) @ w_down


def setup_inputs(seed: int = 0) -> dict:
    key = jax.random.key(seed)
    ks = jax.random.split(key, 16)
    nrm = jax.random.normal
    L = DEPTH
    return {
        "x": nrm(ks[0], (BATCH, SEQ, D_MODEL), jnp.float32),
        "norm1_g": 1.0 + 0.1 * nrm(ks[1], (L, D_MODEL), jnp.float32),
        "w_in": nrm(ks[2], (L, D_MODEL, PROJ_WIDTH), jnp.float32) * D_MODEL ** -0.5,
        "q_norm_g": 1.0 + 0.1 * nrm(ks[3], (L, HEAD_DIM), jnp.float32),
        "k_norm_g": 1.0 + 0.1 * nrm(ks[4], (L, 3, HEAD_DIM), jnp.float32),
        "cmp_pos_k": 0.1 * nrm(ks[5], (L, CMP_BLOCK, HEAD_DIM), jnp.float32),
        "cmp_pos_v": 0.1 * nrm(ks[6], (L, CMP_BLOCK, HEAD_DIM), jnp.float32),
        "w_cmp_k": nrm(ks[7], (L, CMP_BLOCK * HEAD_DIM, HEAD_DIM), jnp.float32) * (CMP_BLOCK * HEAD_DIM) ** -0.5,
        "w_cmp_v": nrm(ks[8], (L, CMP_BLOCK * HEAD_DIM, HEAD_DIM), jnp.float32) * (CMP_BLOCK * HEAD_DIM) ** -0.5,
        "conv_w": nrm(ks[9], (L, CONV_K, CONV_WIDTH), jnp.float32) * CONV_K ** -0.5,
        "w_branch_a": nrm(ks[10], (L, QW, D_MODEL), jnp.float32) * QW ** -0.5,
        "w_branch_b": nrm(ks[11], (L, CONV_WIDTH, D_MODEL), jnp.float32) * CONV_WIDTH ** -0.5,
        "w_out": nrm(ks[12], (L, D_MODEL, D_MODEL), jnp.float32) * D_MODEL ** -0.5,
        "norm2_g": 1.0 + 0.1 * nrm(ks[13], (L, D_MODEL), jnp.float32),
        "w_up": nrm(ks[14], (L, D_MODEL, D_FF), jnp.float32) * D_MODEL ** -0.5,
        "w_down": nrm(ks[15], (L, D_FF, D_MODEL), jnp.float32) * D_FF ** -0.5,
    }


def reference(x, norm1_g, w_in, q_norm_g, k_norm_g, cmp_pos_k, cmp_pos_v,
              w_cmp_k, w_cmp_v, conv_w, w_branch_a, w_branch_b, w_out,
              norm2_g, w_up, w_down):
    for l in range(DEPTH):
        x = hybrid_layer(x, norm1_g[l], w_in[l], q_norm_g[l], k_norm_g[l],
                         cmp_pos_k[l], cmp_pos_v[l], w_cmp_k[l], w_cmp_v[l],
                         conv_w[l], w_branch_a[l], w_branch_b[l], w_out[l],
                         norm2_g[l], w_up[l], w_down[l])
    return x
```

```python
import functools

import numpy as np
import jax
import jax.numpy as jnp
from jax import lax
from jax.experimental import pallas as pl
from jax.experimental.pallas import tpu as pltpu

D_MODEL = 1024
N_HEADS = 8
HEAD_DIM = 64
N_KV = 2
GQA = N_HEADS // N_KV
CMP_BLOCK = 32
CMP_STRIDE = 16
SLC_BLOCK = 64
N_SLC = 16
WINDOW = 512
FORCE_SCORE = 1e4
SCALE = 0.125
CONV_K = 3
D_FF = 4 * D_MODEL
EPS = 1e-6
NEG_INF = -1e30

QW = N_HEADS * HEAD_DIM
KVW = N_KV * HEAD_DIM
ATTN_COLS = QW + 6 * KVW
GATE_COLS = 2 * 128
PA_COLS = ATTN_COLS + GATE_COLS
OFF_CONV_B = PA_COLS
OFF_CONV_C = OFF_CONV_B + D_MODEL
OFF_CONV_X = OFF_CONV_C + D_MODEL
OFF_GATE_A = OFF_CONV_X + D_MODEL
OFF_GATE_B = OFF_GATE_A + D_MODEL
W_IN_COLS = OFF_GATE_B + D_MODEL

TQ = 128
RT = GQA * TQ
AUG = 128
LANE_A = 64
VMEM_LIMIT = 56 * 1024 * 1024

F32 = jnp.float32
BF16 = jnp.bfloat16
NT = (((1,), (1,)), ((), ()))


def _rms(x, g):
    return x * lax.rsqrt(jnp.mean(x * x, axis=-1, keepdims=True) + EPS) * g


def _const_spec(shape):
    zeros = (0,) * len(shape)
    return pl.BlockSpec(shape, lambda *_: zeros, pipeline_mode=pl.Buffered(1))


def _proj_kernel(x_ref, g1_ref, w_ref, wb_ref, qg_ref, kg_ref, cw_ref,
                 q_ref, kci_ref, vci_ref, ks_ref, vs_ref, kw_ref, vw_ref, gn_ref,
                 sga_ref, ob_ref, u_scr):
    i = pl.program_id(1)
    tm = x_ref.shape[0]
    xn = _rms(x_ref[...], g1_ref[...]).astype(BF16)

    pa = jnp.dot(xn, w_ref[:, 0:PA_COLS], preferred_element_type=F32)
    qg = qg_ref[...]
    for h in range(N_HEADS):
        qh = _rms(pa[:, h * HEAD_DIM:(h + 1) * HEAD_DIM], qg)
        q_ref[h] = (qh * SCALE).astype(BF16)
    kci_ref[...] = pa[:, QW:QW + KVW]
    vci_ref[...] = pa[:, QW + KVW:QW + 2 * KVW]
    off = QW + 2 * KVW
    for g in range(N_KV):
        lo, hi = g * HEAD_DIM, (g + 1) * HEAD_DIM
        ks_ref[g] = _rms(pa[:, off + lo:off + hi], kg_ref[1:2, :]).astype(BF16)
        vs_ref[g] = pa[:, off + KVW + lo:off + KVW + hi].astype(BF16)
        kw_ref[g] = _rms(pa[:, off + 2 * KVW + lo:off + 2 * KVW + hi],
                         kg_ref[2:3, :]).astype(BF16)
        vw_ref[g] = pa[:, off + 3 * KVW + lo:off + 3 * KVW + hi].astype(BF16)
    gn_ref[...] = jax.nn.sigmoid(pa[:, ATTN_COLS:PA_COLS])

    def col(o):
        return jnp.dot(xn, w_ref[:, o:o + D_MODEL], preferred_element_type=F32)

    u = col(OFF_CONV_C) * col(OFF_CONV_X)

    @pl.when(i == 0)
    def _():
        u_scr[0:8, :] = jnp.zeros((8, D_MODEL), F32)

    u_scr[8:8 + tm, :] = u
    cw = cw_ref[...]
    y = (cw[2:3, :] * u + cw[1:2, :] * u_scr[7:7 + tm, :]
         + cw[0:1, :] * u_scr[6:6 + tm, :])
    u_scr[0:8, :] = u_scr[tm:tm + 8, :]
    z = col(OFF_CONV_B) * y
    zb = jnp.dot(z.astype(BF16), wb_ref[...], preferred_element_type=F32)
    ob_ref[...] = jax.nn.sigmoid(col(OFF_GATE_B)) * zb
    sga_ref[...] = jax.nn.sigmoid(col(OFF_GATE_A))


def _proj_call(x2, g1, w_in_p, w_b, qg, kg, cw, B, S, tm):
    T = B * S
    nt = S // tm
    row = lambda b, i: (b * nt + i, 0)
    hb = lambda b, i: (b, 0, i, 0)
    kv_shape = jax.ShapeDtypeStruct((B, N_KV, S, HEAD_DIM), BF16)
    kv_spec = pl.BlockSpec((None, N_KV, tm, HEAD_DIM), hb)
    return pl.pallas_call(
        _proj_kernel,
        grid=(B, nt),
        in_specs=[
            pl.BlockSpec((tm, D_MODEL), row),
            _const_spec((1, D_MODEL)),
            _const_spec((D_MODEL, W_IN_COLS)),
            _const_spec((D_MODEL, D_MODEL)),
            _const_spec((1, HEAD_DIM)),
            _const_spec((3, HEAD_DIM)),
            _const_spec((CONV_K, D_MODEL)),
        ],
        out_specs=[
            pl.BlockSpec((None, N_HEADS, tm, HEAD_DIM), hb),
            pl.BlockSpec((tm, KVW), row),
            pl.BlockSpec((tm, KVW), row),
            kv_spec, kv_spec, kv_spec, kv_spec,
            pl.BlockSpec((tm, GATE_COLS), row),
            pl.BlockSpec((tm, D_MODEL), row),
            pl.BlockSpec((tm, D_MODEL), row),
        ],
        out_shape=[
            jax.ShapeDtypeStruct((B, N_HEADS, S, HEAD_DIM), BF16),
            jax.ShapeDtypeStruct((T, KVW), F32),
            jax.ShapeDtypeStruct((T, KVW), F32),
            kv_shape, kv_shape, kv_shape, kv_shape,
            jax.ShapeDtypeStruct((T, GATE_COLS), F32),
            jax.ShapeDtypeStruct((T, D_MODEL), F32),
            jax.ShapeDtypeStruct((T, D_MODEL), F32),
        ],
        scratch_shapes=[pltpu.VMEM((tm + 8, D_MODEL), F32)],
        compiler_params=pltpu.CompilerParams(
            dimension_semantics=("arbitrary", "arbitrary"),
            vmem_limit_bytes=VMEM_LIMIT),
        name="proj",
    )(x2, g1, w_in_p, w_b, qg, kg, cw)


def _cmp_kernel(kci_ref, vci_ref, pos_ref, w_ref, kg_ref, kc_ref, vc_ref, b_scr):
    ncp = kc_ref.shape[1]
    wide = 2 * KVW
    acc_a = jnp.zeros((ncp, wide), F32)
    acc_b = jnp.zeros((ncp, wide), F32)
    for l in range(CMP_STRIDE):
        rows = jnp.concatenate([kci_ref[pl.ds(l, ncp, stride=CMP_STRIDE), :],
                                vci_ref[pl.ds(l, ncp, stride=CMP_STRIDE), :]], axis=1)
        xa = (rows + pos_ref[l:l + 1, :]).astype(BF16)
        xb = (rows + pos_ref[CMP_STRIDE + l:CMP_STRIDE + l + 1, :]).astype(BF16)
        acc_a = acc_a + jnp.dot(xa, w_ref[l], preferred_element_type=F32)
        acc_b = acc_b + jnp.dot(xb, w_ref[CMP_STRIDE + l], preferred_element_type=F32)
    b_scr[0:ncp, :] = acc_b
    b_scr[ncp:ncp + 8, :] = jnp.zeros((8, wide), F32)
    kcv = acc_a + b_scr[1:ncp + 1, :]
    for g in range(N_KV):
        lo, hi = g * HEAD_DIM, (g + 1) * HEAD_DIM
        kc_ref[g] = _rms(kcv[:, lo:hi], kg_ref[0:1, :]).astype(BF16)
        vc_ref[g] = kcv[:, KVW + lo:KVW + hi].astype(BF16)


def _cmp_call(kci, vci, pos4, w_bd, kg, B, S):
    ncp = S // CMP_STRIDE
    out_shape = jax.ShapeDtypeStruct((B, N_KV, ncp, HEAD_DIM), BF16)
    out_spec = pl.BlockSpec((None, N_KV, ncp, HEAD_DIM), lambda b: (b, 0, 0, 0))
    return pl.pallas_call(
        _cmp_kernel,
        grid=(B,),
        in_specs=[
            pl.BlockSpec((S, KVW), lambda b: (b, 0)),
            pl.BlockSpec((S, KVW), lambda b: (b, 0)),
            _const_spec((CMP_BLOCK, 2 * KVW)),
            _const_spec((CMP_BLOCK, 2 * KVW, 2 * KVW)),
            _const_spec((3, HEAD_DIM)),
        ],
        out_specs=[out_spec, out_spec],
        out_shape=[out_shape, out_shape],
        scratch_shapes=[pltpu.VMEM((ncp + 8, 2 * KVW), F32)],
        compiler_params=pltpu.CompilerParams(
            dimension_semantics=("arbitrary",), vmem_limit_bytes=VMEM_LIMIT),
        name="compress",
    )(kci, vci, pos4, w_bd, kg)


def _attn_kernel(q_ref, kc_ref, vc_ref, ks_ref, vs_ref, kw_ref, vw_ref, gn_ref,
                 ovl_ref, kaug_ref, o_ref, m_scr, l_scr, acc_scr):
    g = pl.program_id(1)
    i = pl.program_id(2)
    t0 = i * TQ
    ns = ovl_ref.shape[0]
    ncp = ovl_ref.shape[1]

    qs = q_ref[...].reshape(RT, HEAD_DIM)
    row = lax.broadcasted_iota(jnp.int32, (RT, 1), 0)
    t_col = t0 + (row & (TQ - 1))
    head = g * GQA + (row >> 7) + 1
    slope = lax.bitcast_convert_type((127 - head) << 23, F32)

    s = lax.dot_general(qs, kc_ref[...], NT, preferred_element_type=F32)
    c_end = lax.broadcasted_iota(jnp.int32, (1, ncp), 1) * CMP_STRIDE + (CMP_BLOCK - 1)
    dist = (t_col - c_end).astype(F32)
    mask = dist >= 0
    s = jnp.where(mask, s - slope * dist, NEG_INF)
    m = jnp.max(s, axis=-1, keepdims=True)
    e = jnp.where(mask, jnp.exp(s - m), 0.0)
    p = (e / jnp.maximum(jnp.sum(e, axis=-1, keepdims=True), 1e-30)).astype(BF16)
    o_cmp = jnp.dot(p, vc_ref[...], preferred_element_type=F32)

    ovl = ovl_ref[...]
    imp = jnp.zeros((ns, TQ), F32)
    for r in range(GQA):
        imp = imp + lax.dot_general(ovl, p[r * TQ:(r + 1) * TQ, :], NT,
                                    preferred_element_type=F32)
    j = lax.broadcasted_iota(jnp.int32, (ns, TQ), 0)
    cur = (t0 + lax.broadcasted_iota(jnp.int32, (ns, TQ), 1)) >> 6
    forced = (j == 0) | (j == cur) | (j == cur - 1)
    score = jnp.where(forced, FORCE_SCORE, jnp.where(j <= cur, imp, -1.0))
    cnt = jnp.zeros((ns, TQ), jnp.int32)
    for b in range(ns):
        sb = score[b:b + 1, :]
        ahead = (sb > score) | ((sb == score) & (j > b))
        cnt = cnt + ahead.astype(jnp.int32)
    sel = (cnt < N_SLC) & (j <= cur)
    selb_t = jnp.where(sel, 0.0, NEG_INF)
    selb = jnp.concatenate([selb_t, jnp.zeros((AUG - ns, TQ), F32)], axis=0).T
    selb = jnp.concatenate([selb] * GQA, axis=0)

    lane = lax.broadcasted_iota(jnp.int32, (RT, AUG), 1)
    a_t = (t_col >> 6).astype(F32)
    b_t = (t_col & 63).astype(F32)
    alibi = jnp.where(lane == LANE_A, -(slope * 64.0) * a_t,
                      jnp.where(lane == LANE_A + 1, -slope * b_t,
                                jnp.where(lane == LANE_A + 2, slope * 64.0,
                                          jnp.where(lane == LANE_A + 3, slope, 0.0))))
    qa_sel = jnp.where(lane < LANE_A, selb, alibi).astype(BF16)
    qa_win = alibi.astype(BF16)

    pos_row = lax.broadcasted_iota(jnp.int32, (1, TQ), 1)

    def scores(kt, qa, k_ref):
        k0 = pl.multiple_of(kt * TQ, TQ)
        sc = lax.dot_general(qs, k_ref[pl.ds(k0, TQ), :], NT, preferred_element_type=F32)
        return sc + lax.dot_general(qa, kaug_ref[pl.ds(k0, TQ), :], NT,
                                    preferred_element_type=F32)

    def first_tile(sc, kt, v_ref):
        k0 = pl.multiple_of(kt * TQ, TQ)
        mx = jnp.max(sc, axis=-1, keepdims=True)
        pr = jnp.exp(sc - mx)
        m_scr[...] = mx
        l_scr[...] = jnp.sum(pr, axis=-1, keepdims=True)
        acc_scr[...] = jnp.dot(pr.astype(BF16), v_ref[pl.ds(k0, TQ), :],
                               preferred_element_type=F32)

    def next_tile(sc, kt, v_ref):
        k0 = pl.multiple_of(kt * TQ, TQ)
        m_old = m_scr[...]
        m_new = jnp.maximum(m_old, jnp.max(sc, axis=-1, keepdims=True))
        alpha = jnp.exp(m_old - m_new)
        pr = jnp.exp(sc - m_new)
        m_scr[...] = m_new
        l_scr[...] = alpha * l_scr[...] + jnp.sum(pr, axis=-1, keepdims=True)
        acc_scr[...] = alpha * acc_scr[...] + jnp.dot(
            pr.astype(BF16), v_ref[pl.ds(k0, TQ), :], preferred_element_type=F32)

    def finish():
        return acc_scr[...] / jnp.maximum(l_scr[...], 1e-30)

    causal = t_col >= (t0 + pos_row)

    first_tile(jnp.where(causal, scores(i, qa_sel, ks_ref), NEG_INF), i, vs_ref)

    def sel_body(kt, carry):
        next_tile(scores(kt, qa_sel, ks_ref), kt, vs_ref)
        return carry

    lax.fori_loop(0, i, sel_body, 0)
    o_slc = finish()

    first_tile(jnp.where(causal, scores(i, qa_win, kw_ref), NEG_INF), i, vw_ref)
    n_full = WINDOW // TQ - 1
    for d in range(1, n_full + 1):
        @pl.when(i >= d)
        def _(d=d):
            next_tile(scores(i - d, qa_win, kw_ref), i - d, vw_ref)

    @pl.when(i >= n_full + 1)
    def _():
        kt = i - (n_full + 1)
        inside = (t_col - (kt * TQ + pos_row)) < WINDOW
        next_tile(jnp.where(inside, scores(kt, qa_win, kw_ref), NEG_INF), kt, vw_ref)

    o_win = finish()

    gn = gn_ref[...]

    def gate(branch):
        return jnp.concatenate(
            [gn[:, branch * GQA + r:branch * GQA + r + 1] for r in range(GQA)], axis=0)

    o = gate(0) * o_cmp + gate(1) * o_slc + gate(2) * o_win
    o_ref[...] = jnp.concatenate(
        [o[r * TQ:(r + 1) * TQ, :] for r in range(GQA)], axis=1).astype(BF16)


def _attn_call(q, kc, vc, ks, vs, kw, vw, gn, ovl_t, kaug, B, S):
    T = B * S
    nq = S // TQ
    ncp = S // CMP_STRIDE
    ns = S // SLC_BLOCK
    bg = lambda b, g, i: (b, g, 0, 0)
    kv_spec = pl.BlockSpec((None, None, S, HEAD_DIM), bg)
    c_spec = pl.BlockSpec((None, None, ncp, HEAD_DIM), bg)
    return pl.pallas_call(
        _attn_kernel,
        grid=(B, N_KV, nq),
        in_specs=[
            pl.BlockSpec((None, GQA, TQ, HEAD_DIM), lambda b, g, i: (b, g, i, 0)),
            c_spec, c_spec, kv_spec, kv_spec, kv_spec, kv_spec,
            pl.BlockSpec((TQ, 128), lambda b, g, i: (b * nq + i, g)),
            _const_spec((ns, ncp)),
            _const_spec((S, AUG)),
        ],
        out_specs=pl.BlockSpec((TQ, GQA * HEAD_DIM), lambda b, g, i: (b * nq + i, g)),
        out_shape=jax.ShapeDtypeStruct((T, QW), BF16),
        scratch_shapes=[pltpu.VMEM((RT, 1), F32), pltpu.VMEM((RT, 1), F32),
                        pltpu.VMEM((RT, HEAD_DIM), F32)],
        compiler_params=pltpu.CompilerParams(
            dimension_semantics=("arbitrary", "arbitrary", "arbitrary"),
            vmem_limit_bytes=VMEM_LIMIT),
        name="attn",
    )(q, kc, vc, ks, vs, kw, vw, gn, ovl_t, kaug)


FF_CHUNK = 512


def _mlp_kernel(x_ref, o_ref, sga_ref, ob_ref, wa_ref, wo_ref, g2_ref, wu_ref, wd_ref,
                out_ref, acc_scr):
    a = jnp.dot(o_ref[...], wa_ref[...], preferred_element_type=F32)
    mixed = sga_ref[...] * a + ob_ref[...]
    x1 = x_ref[...] + jnp.dot(mixed.astype(BF16), wo_ref[...], preferred_element_type=F32)
    h = _rms(x1, g2_ref[...]).astype(BF16)
    acc_scr[...] = x1
    for c in range(D_FF // FF_CHUNK):
        lo, hi = c * FF_CHUNK, (c + 1) * FF_CHUNK
        up = jnp.dot(h, wu_ref[:, lo:hi], preferred_element_type=F32)
        act = jnp.square(jnp.maximum(up, 0.0)).astype(BF16)
        acc_scr[...] += jnp.dot(act, wd_ref[lo:hi, :], preferred_element_type=F32)
    out_ref[...] = acc_scr[...]


def _mlp_call(x2, o_nsa, sga, ob, w_a, w_o, g2, w_up, w_down, tm):
    T = x2.shape[0]
    row = lambda i: (i, 0)
    return pl.pallas_call(
        _mlp_kernel,
        grid=(T // tm,),
        in_specs=[
            pl.BlockSpec((tm, D_MODEL), row),
            pl.BlockSpec((tm, QW), row),
            pl.BlockSpec((tm, D_MODEL), row),
            pl.BlockSpec((tm, D_MODEL), row),
            _const_spec((QW, D_MODEL)),
            _const_spec((D_MODEL, D_MODEL)),
            _const_spec((1, D_MODEL)),
            _const_spec((D_MODEL, D_FF)),
            _const_spec((D_FF, D_MODEL)),
        ],
        out_specs=pl.BlockSpec((tm, D_MODEL), row),
        out_shape=jax.ShapeDtypeStruct((T, D_MODEL), F32),
        scratch_shapes=[pltpu.VMEM((tm, D_MODEL), F32)],
        compiler_params=pltpu.CompilerParams(
            dimension_semantics=("arbitrary",), vmem_limit_bytes=VMEM_LIMIT),
        name="mlp",
    )(x2, o_nsa, sga, ob, w_a, w_o, g2, w_up, w_down)


def _pack_w_in(w_in):
    gates = w_in[:, ATTN_COLS:ATTN_COLS + 3 * N_HEADS]
    gates = gates.reshape(D_MODEL, 3, N_KV, GQA).transpose(0, 2, 1, 3)
    gates = gates.reshape(D_MODEL, N_KV, 3 * GQA)
    gates = jnp.pad(gates, ((0, 0), (0, 0), (0, 128 - 3 * GQA))).reshape(D_MODEL, GATE_COLS)
    rest = w_in[:, ATTN_COLS + 3 * N_HEADS:]
    return jnp.concatenate([w_in[:, :ATTN_COLS], gates, rest], axis=1).astype(BF16)


def _pack_cmp(w_cmp_k, w_cmp_v, cmp_pos_k, cmp_pos_v):
    wk = w_cmp_k.reshape(CMP_BLOCK, HEAD_DIM, HEAD_DIM)
    wv = w_cmp_v.reshape(CMP_BLOCK, HEAD_DIM, HEAD_DIM)
    w = jnp.zeros((CMP_BLOCK, 2 * KVW, 2 * KVW), F32)
    for slot, blk in enumerate((wk, wk, wv, wv)):
        lo, hi = slot * HEAD_DIM, (slot + 1) * HEAD_DIM
        w = w.at[:, lo:hi, lo:hi].set(blk)
    pos4 = jnp.concatenate([cmp_pos_k, cmp_pos_k, cmp_pos_v, cmp_pos_v], axis=1)
    return w.astype(BF16), pos4


def _overlap_t(S):
    ncp = S // CMP_STRIDE
    ns = S // SLC_BLOCK
    c_start = np.arange(ncp) * CMP_STRIDE
    s_start = np.arange(ns) * SLC_BLOCK
    ov = np.clip(np.minimum(c_start[None, :] + CMP_BLOCK, s_start[:, None] + SLC_BLOCK)
                 - np.maximum(c_start[None, :], s_start[:, None]), 0, None)
    return jnp.asarray(ov.astype(np.float32) / CMP_BLOCK, dtype=BF16)


def _key_aug(S):
    pos = np.arange(S)
    ka = np.zeros((S, AUG), np.float32)
    ka[pos, pos // SLC_BLOCK] = 1.0
    ka[:, LANE_A] = 1.0
    ka[:, LANE_A + 1] = 1.0
    ka[:, LANE_A + 2] = pos // 64
    ka[:, LANE_A + 3] = pos % 64
    return jnp.asarray(ka, dtype=BF16)


def _layer(x, norm1_g, w_in, q_norm_g, k_norm_g, cmp_pos_k, cmp_pos_v, w_cmp_k, w_cmp_v,
           conv_w, w_branch_a, w_branch_b, w_out, norm2_g, w_up, w_down):
    B, S, _ = x.shape
    assert S % 256 == 0 and S // SLC_BLOCK <= LANE_A
    x2 = x.reshape(B * S, D_MODEL)
    w_bd, pos4 = _pack_cmp(w_cmp_k, w_cmp_v, cmp_pos_k, cmp_pos_v)
    q, kci, vci, ks, vs, kw, vw, gn, sga, ob = _proj_call(
        x2, norm1_g[None, :], _pack_w_in(w_in), w_branch_b.astype(BF16),
        q_norm_g[None, :], k_norm_g, conv_w, B, S, tm=256)
    kc, vc = _cmp_call(kci, vci, pos4, w_bd, k_norm_g, B, S)
    o_nsa = _attn_call(q, kc, vc, ks, vs, kw, vw, gn, _overlap_t(S), _key_aug(S), B, S)
    out = _mlp_call(x2, o_nsa, sga, ob, w_branch_a.astype(BF16), w_out.astype(BF16),
                    norm2_g[None, :], w_up.astype(BF16), w_down.astype(BF16), tm=256)
    return out.reshape(B, S, D_MODEL)


@jax.jit
def kernel(x, norm1_g, w_in, q_norm_g, k_norm_g, cmp_pos_k, cmp_pos_v, w_cmp_k, w_cmp_v,
           conv_w, w_branch_a, w_branch_b, w_out, norm2_g, w_up, w_down):
    for l in range(norm1_g.shape[0]):
        x = _layer(x, norm1_g[l], w_in[l], q_norm_g[l], k_norm_g[l], cmp_pos_k[l],
                   cmp_pos_v[l], w_cmp_k[l], w_cmp_v[l], conv_w[l], w_branch_a[l],
                   w_branch_b[l], w_out[l], norm2_g[l], w_up[l], w_down[l])
    return x
```

```python
import numpy as np
import jax
import jax.numpy as jnp
from jax import lax
from jax.experimental import pallas as pl
from jax.experimental.pallas import tpu as pltpu

D_MODEL = 1024
N_HEADS = 8
HEAD_DIM = 64
N_KV = 2
GQA = N_HEADS // N_KV
CMP_BLOCK = 32
CMP_STRIDE = 16
SLC_BLOCK = 64
N_SLC = 16
WINDOW = 512
FORCE_SCORE = 1e4
SCALE = 0.125
CONV_K = 3
D_FF = 4 * D_MODEL
EPS = 1e-6
NEG_INF = -1e30

LANES = 128
QW = N_HEADS * HEAD_DIM
KVW = N_KV * HEAD_DIM

OFF_Q = 0
OFF_KC = OFF_Q + QW
OFF_VC = OFF_KC + KVW
OFF_KS = OFF_VC + KVW
OFF_VS = OFF_KS + N_KV * LANES
OFF_KW = OFF_VS + KVW
OFF_VW = OFF_KW + N_KV * LANES
OFF_GN = OFF_VW + KVW
PA_COLS = OFF_GN + N_KV * LANES
OFF_CONV_B = PA_COLS
OFF_CONV_C = OFF_CONV_B + D_MODEL
OFF_CONV_X = OFF_CONV_C + D_MODEL
OFF_GATE_A = OFF_CONV_X + D_MODEL
OFF_GATE_B = OFF_GATE_A + D_MODEL
W_IN_COLS = OFF_GATE_B + D_MODEL

TQ = 128
NL = GQA * TQ
UNROLL = 4
VROWS = 80
GROWS = 16
CMP_OUT = 2 * LANES + KVW
VMEM_LIMIT = 56 * 1024 * 1024

F32 = jnp.float32
BF16 = jnp.bfloat16


def _rms(x, g):
    return x * lax.rsqrt(jnp.mean(x * x, axis=-1, keepdims=True) + EPS) * g


def _rms_padded(x, g):
    ms = jnp.sum(x * x, axis=-1, keepdims=True) * (1.0 / HEAD_DIM)
    return x * lax.rsqrt(ms + EPS) * g


def _alibi_key_lanes(pos):
    lane = lax.broadcasted_iota(jnp.int32, (pos.shape[0], LANES), 1)
    hi = (pos >> 6).astype(F32)
    lo = (pos & 63).astype(F32)
    return jnp.where((lane == HEAD_DIM) | (lane == HEAD_DIM + 1), 1.0,
                     jnp.where(lane == HEAD_DIM + 2, hi,
                               jnp.where(lane == HEAD_DIM + 3, lo, 0.0)))


def _value_tail():
    r = lax.broadcasted_iota(jnp.int32, (VROWS - HEAD_DIM, LANES), 0)
    return jnp.where(r == 0, 1.0, 0.0).astype(BF16)


def _const_spec(shape):
    zeros = (0,) * len(shape)
    return pl.BlockSpec(shape, lambda *_: zeros, pipeline_mode=pl.Buffered(1))


def _proj_kernel(x_ref, g1_ref, w_ref, wb_ref, qg_ref, kg_ref, cw_ref,
                 qt_ref, kci_ref, vci_ref, ks_ref, vst_ref, kw_ref, vwt_ref, gnt_ref,
                 sga_ref, ob_ref, u_scr):
    i = pl.program_id(1)
    tm = x_ref.shape[0]
    xn = _rms(x_ref[...], g1_ref[...]).astype(BF16)

    pa = jnp.dot(xn, w_ref[:, 0:PA_COLS], preferred_element_type=F32)

    qg = qg_ref[...]
    for pair in range(N_HEADS // 2):
        qp = pa[:, OFF_Q + pair * LANES:OFF_Q + (pair + 1) * LANES].T
        for half in range(2):
            qh = qp[half * HEAD_DIM:(half + 1) * HEAD_DIM, :]
            ms = jnp.mean(qh * qh, axis=0, keepdims=True)
            qt_ref[2 * pair + half] = (qh * lax.rsqrt(ms + EPS) * qg * SCALE).astype(BF16)

    kci_ref[...] = pa[:, OFF_KC:OFF_KC + KVW]
    vci_ref[...] = pa[:, OFF_VC:OFF_VC + KVW]

    pos = i * tm + lax.broadcasted_iota(jnp.int32, (tm, 1), 0)
    key_lanes = _alibi_key_lanes(pos)
    tail = _value_tail()
    for g in range(N_KV):
        ks = pa[:, OFF_KS + g * LANES:OFF_KS + (g + 1) * LANES]
        ks_ref[g] = (_rms_padded(ks, kg_ref[1:2, :]) + key_lanes).astype(BF16)
        kw = pa[:, OFF_KW + g * LANES:OFF_KW + (g + 1) * LANES]
        kw_ref[g] = (_rms_padded(kw, kg_ref[2:3, :]) + key_lanes).astype(BF16)
    vs_t = pa[:, OFF_VS:OFF_VS + KVW].T.astype(BF16)
    vw_t = pa[:, OFF_VW:OFF_VW + KVW].T.astype(BF16)
    for g in range(N_KV):
        for c in range(tm // TQ):
            vst_ref[g, c, 0:HEAD_DIM, :] = vs_t[g * HEAD_DIM:(g + 1) * HEAD_DIM, c * TQ:(c + 1) * TQ]
            vst_ref[g, c, HEAD_DIM:VROWS, :] = tail
            vwt_ref[g, c, 0:HEAD_DIM, :] = vw_t[g * HEAD_DIM:(g + 1) * HEAD_DIM, c * TQ:(c + 1) * TQ]
            vwt_ref[g, c, HEAD_DIM:VROWS, :] = tail
        gates = jax.nn.sigmoid(pa[:, OFF_GN + g * LANES:OFF_GN + (g + 1) * LANES])
        gnt_ref[g] = gates.T[0:GROWS, :]

    def col(o):
        return jnp.dot(xn, w_ref[:, o:o + D_MODEL], preferred_element_type=F32)

    u = col(OFF_CONV_C) * col(OFF_CONV_X)

    @pl.when(i == 0)
    def _():
        u_scr[0:8, :] = jnp.zeros((8, D_MODEL), F32)

    u_scr[8:8 + tm, :] = u
    cw = cw_ref[...]
    y = (cw[2:3, :] * u + cw[1:2, :] * u_scr[7:7 + tm, :]
         + cw[0:1, :] * u_scr[6:6 + tm, :])
    u_scr[0:8, :] = u_scr[tm:tm + 8, :]
    z = col(OFF_CONV_B) * y
    zb = jnp.dot(z.astype(BF16), wb_ref[...], preferred_element_type=F32)
    ob_ref[...] = jax.nn.sigmoid(col(OFF_GATE_B)) * zb
    sga_ref[...] = jax.nn.sigmoid(col(OFF_GATE_A))


def _proj_call(x2, g1, w_in_p, w_b, qg, kg, cw, B, S, tm):
    T = B * S
    nt = S // tm
    row = lambda b, i: (b * nt + i, 0)
    k_shape = jax.ShapeDtypeStruct((B, N_KV, S, LANES), BF16)
    k_spec = pl.BlockSpec((None, N_KV, tm, LANES), lambda b, i: (b, 0, i, 0))
    vt_shape = jax.ShapeDtypeStruct((B, N_KV, S // TQ, VROWS, TQ), BF16)
    vt_spec = pl.BlockSpec((None, N_KV, tm // TQ, VROWS, TQ), lambda b, i: (b, 0, i, 0, 0))
    return pl.pallas_call(
        _proj_kernel,
        grid=(B, nt),
        in_specs=[
            pl.BlockSpec((tm, D_MODEL), row),
            _const_spec((1, D_MODEL)),
            _const_spec((D_MODEL, W_IN_COLS)),
            _const_spec((D_MODEL, D_MODEL)),
            _const_spec((HEAD_DIM, 1)),
            _const_spec((3, LANES)),
            _const_spec((CONV_K, D_MODEL)),
        ],
        out_specs=[
            pl.BlockSpec((None, N_HEADS, HEAD_DIM, tm), lambda b, i: (b, 0, 0, i)),
            pl.BlockSpec((tm, KVW), row),
            pl.BlockSpec((tm, KVW), row),
            k_spec, vt_spec, k_spec, vt_spec,
            pl.BlockSpec((None, N_KV, GROWS, tm), lambda b, i: (b, 0, 0, i)),
            pl.BlockSpec((tm, D_MODEL), row),
            pl.BlockSpec((tm, D_MODEL), row),
        ],
        out_shape=[
            jax.ShapeDtypeStruct((B, N_HEADS, HEAD_DIM, S), BF16),
            jax.ShapeDtypeStruct((T, KVW), F32),
            jax.ShapeDtypeStruct((T, KVW), F32),
            k_shape, vt_shape, k_shape, vt_shape,
            jax.ShapeDtypeStruct((B, N_KV, GROWS, S), F32),
            jax.ShapeDtypeStruct((T, D_MODEL), F32),
            jax.ShapeDtypeStruct((T, D_MODEL), F32),
        ],
        scratch_shapes=[pltpu.VMEM((tm + 8, D_MODEL), F32)],
        compiler_params=pltpu.CompilerParams(
            dimension_semantics=("arbitrary", "arbitrary"),
            vmem_limit_bytes=VMEM_LIMIT),
        name="proj",
    )(x2, g1, w_in_p, w_b, qg, kg, cw)


def _cmp_kernel(kci_ref, vci_ref, pos_ref, w_ref, kg_ref, kc_ref, vct_ref, b_scr):
    ncp = kc_ref.shape[1]
    acc_a = jnp.zeros((ncp, CMP_OUT), F32)
    acc_b = jnp.zeros((ncp, CMP_OUT), F32)
    for l in range(CMP_STRIDE):
        rows = jnp.concatenate([kci_ref[pl.ds(l, ncp, stride=CMP_STRIDE), :],
                                vci_ref[pl.ds(l, ncp, stride=CMP_STRIDE), :]], axis=1)
        xa = (rows + pos_ref[l:l + 1, :]).astype(BF16)
        xb = (rows + pos_ref[CMP_STRIDE + l:CMP_STRIDE + l + 1, :]).astype(BF16)
        acc_a = acc_a + jnp.dot(xa, w_ref[l], preferred_element_type=F32)
        acc_b = acc_b + jnp.dot(xb, w_ref[CMP_STRIDE + l], preferred_element_type=F32)
    b_scr[0:ncp, :] = acc_b
    b_scr[ncp:ncp + 8, :] = jnp.zeros((8, CMP_OUT), F32)
    kcv = acc_a + b_scr[1:ncp + 1, :]
    c_end = lax.broadcasted_iota(jnp.int32, (ncp, 1), 0) * CMP_STRIDE + (CMP_BLOCK - 1)
    key_lanes = _alibi_key_lanes(c_end)
    vc_t = kcv[:, 2 * LANES:CMP_OUT].T.astype(BF16)
    tail = jnp.concatenate([_value_tail()] * (ncp // LANES), axis=1)
    for g in range(N_KV):
        kc = _rms_padded(kcv[:, g * LANES:(g + 1) * LANES], kg_ref[0:1, :])
        kc_ref[g] = (kc + key_lanes).astype(BF16)
        vct_ref[g, 0:HEAD_DIM, :] = vc_t[g * HEAD_DIM:(g + 1) * HEAD_DIM, :]
        vct_ref[g, HEAD_DIM:VROWS, :] = tail


def _cmp_call(kci, vci, pos4, w_bd, kg, B, S):
    ncp = S // CMP_STRIDE
    return pl.pallas_call(
        _cmp_kernel,
        grid=(B,),
        in_specs=[
            pl.BlockSpec((S, KVW), lambda b: (b, 0)),
            pl.BlockSpec((S, KVW), lambda b: (b, 0)),
            _const_spec((CMP_BLOCK, 2 * KVW)),
            _const_spec((CMP_BLOCK, 2 * KVW, CMP_OUT)),
            _const_spec((3, LANES)),
        ],
        out_specs=[
            pl.BlockSpec((None, N_KV, ncp, LANES), lambda b: (b, 0, 0, 0)),
            pl.BlockSpec((None, N_KV, VROWS, ncp), lambda b: (b, 0, 0, 0)),
        ],
        out_shape=[
            jax.ShapeDtypeStruct((B, N_KV, ncp, LANES), BF16),
            jax.ShapeDtypeStruct((B, N_KV, VROWS, ncp), BF16),
        ],
        scratch_shapes=[pltpu.VMEM((ncp + 8, CMP_OUT), F32)],
        compiler_params=pltpu.CompilerParams(
            dimension_semantics=("arbitrary",), vmem_limit_bytes=VMEM_LIMIT),
        name="compress",
    )(kci, vci, pos4, w_bd, kg)


def _attn_kernel(qt_ref, kc_ref, vct_ref, ks_ref, vst_ref, kw_ref, vwt_ref, gnt_ref,
                 ovl_ref, oh_ref, o_ref, score_scr, acc_scr):
    g = pl.program_id(1)
    i = pl.program_id(2)
    t0 = i * TQ
    ns = ovl_ref.shape[0]
    ncp = ovl_ref.shape[1]

    lane = lax.broadcasted_iota(jnp.int32, (1, NL), 1)
    tq = t0 + (lane & (TQ - 1))
    head = g * GQA + (lane >> 7) + 1
    slope = lax.bitcast_convert_type((127 - head) << 23, F32)
    a_t = (tq >> 6).astype(F32)
    b_t = (tq & 63).astype(F32)
    sub = lax.broadcasted_iota(jnp.int32, (HEAD_DIM, NL), 0)
    alibi = jnp.where(sub == 0, -(slope * 64.0) * a_t,
                      jnp.where(sub == 1, -slope * b_t,
                                jnp.where(sub == 2, slope * 64.0,
                                          jnp.where(sub == 3, slope, 0.0)))).astype(BF16)
    qt = jnp.concatenate([qt_ref[r] for r in range(GQA)], axis=1)
    qa = jnp.concatenate([qt, alibi], axis=0)

    s = jnp.dot(kc_ref[...], qa, preferred_element_type=F32)
    c_end = lax.broadcasted_iota(jnp.int32, (ncp, 1), 0) * CMP_STRIDE + (CMP_BLOCK - 1)
    mask = tq >= c_end
    s = jnp.where(mask, s, NEG_INF)
    m = jnp.max(s, axis=0, keepdims=True)
    e = jnp.where(mask, jnp.exp(s - m), 0.0)
    inv = 1.0 / jnp.maximum(jnp.sum(e, axis=0, keepdims=True), 1e-30)
    p = (e * inv).astype(BF16)
    o_cmp = jnp.dot(vct_ref[...], p, preferred_element_type=F32)[0:HEAD_DIM, :]

    imp4 = jnp.dot(ovl_ref[...], p, preferred_element_type=F32)
    imp = imp4[:, 0:TQ]
    for r in range(1, GQA):
        imp = imp + imp4[:, r * TQ:(r + 1) * TQ]
    j = lax.broadcasted_iota(jnp.int32, (ns, TQ), 0)
    cur = (t0 + lax.broadcasted_iota(jnp.int32, (ns, TQ), 1)) >> 6
    forced = (j == 0) | (j == cur) | (j == cur - 1)
    score = jnp.where(forced, FORCE_SCORE, jnp.where(j <= cur, imp, -1.0))
    score_scr[...] = score

    def rank_body(b2, cnt):
        for b in (2 * b2, 2 * b2 + 1):
            sb = score_scr[pl.ds(b, 1), :]
            ahead = (sb > score) | ((sb == score) & (j > b))
            cnt = cnt + jnp.where(ahead, 1, 0)
        return cnt

    cnt = lax.fori_loop(0, i + 1, rank_body, jnp.zeros((ns, TQ), jnp.int32))
    sel = (cnt < N_SLC) & (j <= cur)
    selb = jnp.where(sel, 0.0, NEG_INF).astype(BF16)
    selb = jnp.concatenate([selb, jnp.zeros((LANES - ns, TQ), BF16)], axis=0)
    qa_sel = jnp.concatenate([qa, jnp.concatenate([selb] * GQA, axis=1)], axis=0)

    pos_col = lax.broadcasted_iota(jnp.int32, (TQ, 1), 0)

    def tile(k_ref, kt):
        return k_ref[pl.ds(pl.multiple_of(kt * TQ, TQ), TQ), :]

    def sel_scores(kt):
        lhs = jnp.concatenate([tile(ks_ref, kt), tile(oh_ref, kt)], axis=1)
        return jnp.dot(lhs, qa_sel, preferred_element_type=F32)

    def win_scores(kt):
        return jnp.dot(tile(kw_ref, kt), qa, preferred_element_type=F32)

    def first_tile(sc, vt):
        mx = jnp.max(sc, axis=0, keepdims=True)
        pr = jnp.exp(sc - mx).astype(BF16)
        acc_scr[...] = jnp.dot(vt, pr, preferred_element_type=F32)
        return mx

    def next_tile(sc, vt, m_old):
        m_new = jnp.maximum(m_old, jnp.max(sc, axis=0, keepdims=True))
        alpha = jnp.exp(m_old - m_new)
        pr = jnp.exp(sc - m_new).astype(BF16)
        acc_scr[...] = alpha * acc_scr[...] + jnp.dot(vt, pr, preferred_element_type=F32)
        return m_new

    def finish():
        acc = acc_scr[...]
        return acc[0:HEAD_DIM, :] / jnp.maximum(acc[HEAD_DIM:HEAD_DIM + 1, :], 1e-30)

    def run_tiles(score_fn, vt_ref, kts, masks, m_run):
        sc_next = score_fn(kts[0])
        for u, (kt, mk) in enumerate(zip(kts, masks)):
            sc = sc_next
            if u + 1 < len(kts):
                sc_next = score_fn(kts[u + 1])
            if mk is not None:
                sc = jnp.where(mk, sc, NEG_INF)
            vt = vt_ref[kt]
            m_run = first_tile(sc, vt) if m_run is None else next_tile(sc, vt, m_run)
        return m_run

    def causal(kt):
        return tq >= (kt * TQ + pos_col)

    n_groups = i // UNROLL
    kts = [n_groups * UNROLL + u for u in range(UNROLL)]
    m_run = run_tiles(sel_scores, vst_ref, kts, [causal(kt) for kt in kts], None)

    def sel_body(grp, mr):
        kts = [grp * UNROLL + u for u in range(UNROLL)]
        return run_tiles(sel_scores, vst_ref, kts, [None] * UNROLL, mr)

    lax.fori_loop(0, n_groups, sel_body, m_run)
    o_slc = finish()

    n_win = WINDOW // TQ

    @pl.when(i >= n_win)
    def _():
        kts = [i - d for d in range(n_win + 1)]
        far = (tq - (kts[-1] * TQ + pos_col)) < WINDOW
        run_tiles(win_scores, vwt_ref, kts, [causal(i)] + [None] * (n_win - 1) + [far], None)

    @pl.when(i < n_win)
    def _():
        kts = list(range(n_win))
        run_tiles(win_scores, vwt_ref, kts, [causal(kt) for kt in kts], None)

    o_win = finish()

    def gate(branch):
        return jnp.concatenate(
            [gnt_ref[branch * GQA + r:branch * GQA + r + 1, :] for r in range(GQA)], axis=1)

    o = gate(0) * o_cmp + gate(1) * o_slc + gate(2) * o_win
    halves = [jnp.concatenate([o[:, (2 * h) * TQ:(2 * h + 1) * TQ],
                               o[:, (2 * h + 1) * TQ:(2 * h + 2) * TQ]], axis=0).T
              for h in range(GQA // 2)]
    o_ref[...] = jnp.concatenate(halves, axis=1).astype(BF16)


def _attn_call(qt, kc, vct, ks, vst, kw, vwt, gnt, ovl, onehot, B, S):
    T = B * S
    nq = S // TQ
    ncp = S // CMP_STRIDE
    ns = S // SLC_BLOCK
    bg4 = lambda b, g, i: (b, g, 0, 0)
    bg5 = lambda b, g, i: (b, g, 0, 0, 0)
    k_spec = pl.BlockSpec((None, None, S, LANES), bg4)
    vt_spec = pl.BlockSpec((None, None, nq, VROWS, TQ), bg5)
    return pl.pallas_call(
        _attn_kernel,
        grid=(B, N_KV, nq),
        in_specs=[
            pl.BlockSpec((None, GQA, HEAD_DIM, TQ), lambda b, g, i: (b, g, 0, i)),
            pl.BlockSpec((None, None, ncp, LANES), bg4),
            pl.BlockSpec((None, None, VROWS, ncp), bg4),
            k_spec, vt_spec, k_spec, vt_spec,
            pl.BlockSpec((None, None, GROWS, TQ), lambda b, g, i: (b, g, 0, i)),
            _const_spec((ns, ncp)),
            _const_spec((S, LANES)),
        ],
        out_specs=pl.BlockSpec((TQ, GQA * HEAD_DIM), lambda b, g, i: (b * nq + i, g)),
        out_shape=jax.ShapeDtypeStruct((T, QW), BF16),
        scratch_shapes=[pltpu.VMEM((ns, TQ), F32), pltpu.VMEM((VROWS, NL), F32)],
        compiler_params=pltpu.CompilerParams(
            dimension_semantics=("arbitrary", "arbitrary", "arbitrary"),
            vmem_limit_bytes=VMEM_LIMIT),
        name="attn",
    )(qt, kc, vct, ks, vst, kw, vwt, gnt, ovl, onehot)


FF_CHUNK = 512


def _mlp_kernel(x_ref, o_ref, sga_ref, ob_ref, wa_ref, wo_ref, g2_ref, wu_ref, wd_ref,
                out_ref, acc_scr):
    a = jnp.dot(o_ref[...], wa_ref[...], preferred_element_type=F32)
    mixed = sga_ref[...] * a + ob_ref[...]
    x1 = x_ref[...] + jnp.dot(mixed.astype(BF16), wo_ref[...], preferred_element_type=F32)
    h = _rms(x1, g2_ref[...]).astype(BF16)
    acc_scr[...] = x1
    for c in range(D_FF // FF_CHUNK):
        lo, hi = c * FF_CHUNK, (c + 1) * FF_CHUNK
        up = jnp.dot(h, wu_ref[:, lo:hi], preferred_element_type=F32)
        act = jnp.square(jnp.maximum(up, 0.0)).astype(BF16)
        acc_scr[...] += jnp.dot(act, wd_ref[lo:hi, :], preferred_element_type=F32)
    out_ref[...] = acc_scr[...]


def _mlp_call(x2, o_nsa, sga, ob, w_a, w_o, g2, w_up, w_down, tm):
    T = x2.shape[0]
    row = lambda i: (i, 0)
    return pl.pallas_call(
        _mlp_kernel,
        grid=(T // tm,),
        in_specs=[
            pl.BlockSpec((tm, D_MODEL), row),
            pl.BlockSpec((tm, QW), row),
            pl.BlockSpec((tm, D_MODEL), row),
            pl.BlockSpec((tm, D_MODEL), row),
            _const_spec((QW, D_MODEL)),
            _const_spec((D_MODEL, D_MODEL)),
            _const_spec((1, D_MODEL)),
            _const_spec((D_MODEL, D_FF)),
            _const_spec((D_FF, D_MODEL)),
        ],
        out_specs=pl.BlockSpec((tm, D_MODEL), row),
        out_shape=jax.ShapeDtypeStruct((T, D_MODEL), F32),
        scratch_shapes=[pltpu.VMEM((tm, D_MODEL), F32)],
        compiler_params=pltpu.CompilerParams(
            dimension_semantics=("arbitrary",), vmem_limit_bytes=VMEM_LIMIT),
        name="mlp",
    )(x2, o_nsa, sga, ob, w_a, w_o, g2, w_up, w_down)


def _pad_heads(w):
    w = w.reshape(D_MODEL, N_KV, HEAD_DIM)
    return jnp.pad(w, ((0, 0), (0, 0), (0, LANES - HEAD_DIM))).reshape(D_MODEL, N_KV * LANES)


def _pack_w_in(w_in):
    o = QW
    k_c, v_c, k_s, v_s, k_w, v_w = (w_in[:, o + n * KVW:o + (n + 1) * KVW] for n in range(6))
    o += 6 * KVW
    gates = w_in[:, o:o + 3 * N_HEADS]
    gates = gates.reshape(D_MODEL, 3, N_KV, GQA).transpose(0, 2, 1, 3)
    gates = gates.reshape(D_MODEL, N_KV, 3 * GQA)
    gates = jnp.pad(gates, ((0, 0), (0, 0), (0, LANES - 3 * GQA))).reshape(D_MODEL, N_KV * LANES)
    rest = w_in[:, o + 3 * N_HEADS:]
    return jnp.concatenate([w_in[:, :QW], k_c, v_c, _pad_heads(k_s), v_s, _pad_heads(k_w), v_w,
                            gates, rest], axis=1).astype(BF16)


def _pack_cmp(w_cmp_k, w_cmp_v, cmp_pos_k, cmp_pos_v):
    wk = w_cmp_k.reshape(CMP_BLOCK, HEAD_DIM, HEAD_DIM)
    wv = w_cmp_v.reshape(CMP_BLOCK, HEAD_DIM, HEAD_DIM)
    w = jnp.zeros((CMP_BLOCK, 2 * KVW, CMP_OUT), F32)
    for slot, (blk, col) in enumerate(((wk, 0), (wk, LANES), (wv, 2 * LANES),
                                       (wv, 2 * LANES + HEAD_DIM))):
        w = w.at[:, slot * HEAD_DIM:(slot + 1) * HEAD_DIM, col:col + HEAD_DIM].set(blk)
    pos4 = jnp.concatenate([cmp_pos_k, cmp_pos_k, cmp_pos_v, cmp_pos_v], axis=1)
    return w.astype(BF16), pos4


def _pad_gain(g):
    return jnp.pad(g, ((0, 0), (0, LANES - HEAD_DIM)))


def _overlap(S):
    ncp = S // CMP_STRIDE
    ns = S // SLC_BLOCK
    c_start = np.arange(ncp) * CMP_STRIDE
    s_start = np.arange(ns) * SLC_BLOCK
    ov = np.clip(np.minimum(c_start[None, :] + CMP_BLOCK, s_start[:, None] + SLC_BLOCK)
                 - np.maximum(c_start[None, :], s_start[:, None]), 0, None)
    return jnp.asarray(ov.astype(np.float32) / CMP_BLOCK, dtype=BF16)


def _block_onehot(S):
    pos = np.arange(S)
    oh = np.zeros((S, LANES), np.float32)
    oh[pos, pos // SLC_BLOCK] = 1.0
    return jnp.asarray(oh, dtype=BF16)


def _layer(x, norm1_g, w_in, q_norm_g, k_norm_g, cmp_pos_k, cmp_pos_v, w_cmp_k, w_cmp_v,
           conv_w, w_branch_a, w_branch_b, w_out, norm2_g, w_up, w_down):
    B, S, _ = x.shape
    assert S % (CMP_STRIDE * LANES) == 0 and S // SLC_BLOCK <= LANES
    assert (S // TQ) % UNROLL == 0 and S >= WINDOW
    x2 = x.reshape(B * S, D_MODEL)
    w_bd, pos4 = _pack_cmp(w_cmp_k, w_cmp_v, cmp_pos_k, cmp_pos_v)
    kg = _pad_gain(k_norm_g)
    qt, kci, vci, ks, vst, kw, vwt, gnt, sga, ob = _proj_call(
        x2, norm1_g[None, :], _pack_w_in(w_in), w_branch_b.astype(BF16),
        q_norm_g[:, None], kg, conv_w, B, S, tm=256)
    kc, vct = _cmp_call(kci, vci, pos4, w_bd, kg, B, S)
    o_nsa = _attn_call(qt, kc, vct, ks, vst, kw, vwt, gnt, _overlap(S), _block_onehot(S), B, S)
    out = _mlp_call(x2, o_nsa, sga, ob, w_branch_a.astype(BF16), w_out.astype(BF16),
                    norm2_g[None, :], w_up.astype(BF16), w_down.astype(BF16), tm=256)
    return out.reshape(B, S, D_MODEL)


@jax.jit
def kernel(x, norm1_g, w_in, q_norm_g, k_norm_g, cmp_pos_k, cmp_pos_v, w_cmp_k, w_cmp_v,
           conv_w, w_branch_a, w_branch_b, w_out, norm2_g, w_up, w_down):
    for l in range(norm1_g.shape[0]):
        x = _layer(x, norm1_g[l], w_in[l], q_norm_g[l], k_norm_g[l], cmp_pos_k[l],
                   cmp_pos_v[l], w_cmp_k[l], w_cmp_v[l], conv_w[l], w_branch_a[l],
                   w_branch_b[l], w_out[l], norm2_g[l], w_up[l], w_down[l])
    return x
```

```python
import numpy as np
import jax
import jax.numpy as jnp
from jax import lax
from jax.experimental import pallas as pl
from jax.experimental.pallas import tpu as pltpu

D_MODEL = 1024
N_HEADS = 8
HEAD_DIM = 64
N_KV = 2
GQA = N_HEADS // N_KV
CMP_BLOCK = 32
CMP_STRIDE = 16
SLC_BLOCK = 64
N_SLC = 16
WINDOW = 512
FORCE_SCORE = 1e4
SCALE = 0.125
CONV_K = 3
D_FF = 4 * D_MODEL
EPS = 1e-6
NEG_INF = -1e30

LANES = 128
QW = N_HEADS * HEAD_DIM
KVW = N_KV * HEAD_DIM

OFF_Q = 0
OFF_KC = OFF_Q + QW
OFF_VC = OFF_KC + KVW
OFF_KS = OFF_VC + KVW
OFF_VS = OFF_KS + N_KV * LANES
OFF_KW = OFF_VS + KVW
OFF_VW = OFF_KW + N_KV * LANES
OFF_GN = OFF_VW + KVW
PA_COLS = OFF_GN + N_KV * LANES
OFF_CONV_B = PA_COLS
OFF_CONV_C = OFF_CONV_B + D_MODEL
OFF_CONV_X = OFF_CONV_C + D_MODEL
OFF_GATE_A = OFF_CONV_X + D_MODEL
OFF_GATE_B = OFF_GATE_A + D_MODEL
W_IN_COLS = OFF_GATE_B + D_MODEL

TQ = 128
NL = GQA * TQ
UNROLL = 8
LOOKAHEAD = 4
VROWS = 80
GROWS = 16
CMP_OUT = 2 * LANES + KVW
VMEM_LIMIT = 56 * 1024 * 1024

F32 = jnp.float32
BF16 = jnp.bfloat16


def _rms(x, g):
    return x * lax.rsqrt(jnp.mean(x * x, axis=-1, keepdims=True) + EPS) * g


def _rms_padded(x, g):
    ms = jnp.sum(x * x, axis=-1, keepdims=True) * (1.0 / HEAD_DIM)
    return x * lax.rsqrt(ms + EPS) * g


def _alibi_key_lanes(pos):
    lane = lax.broadcasted_iota(jnp.int32, (pos.shape[0], LANES), 1)
    hi = (pos >> 6).astype(F32)
    lo = (pos & 63).astype(F32)
    return jnp.where((lane == HEAD_DIM) | (lane == HEAD_DIM + 1), 1.0,
                     jnp.where(lane == HEAD_DIM + 2, hi,
                               jnp.where(lane == HEAD_DIM + 3, lo, 0.0)))


def _value_tail():
    r = lax.broadcasted_iota(jnp.int32, (VROWS - HEAD_DIM, LANES), 0)
    return jnp.where(r == 0, 1.0, 0.0).astype(BF16)


def _const_spec(shape):
    zeros = (0,) * len(shape)
    return pl.BlockSpec(shape, lambda *_: zeros, pipeline_mode=pl.Buffered(1))


def _proj_kernel(x_ref, g1_ref, w_ref, wb_ref, qg_ref, kg_ref, cw_ref,
                 qt_ref, kci_ref, vci_ref, ks_ref, vst_ref, kw_ref, vwt_ref, gnt_ref,
                 sga_ref, ob_ref, u_scr):
    i = pl.program_id(1)
    tm = x_ref.shape[0]
    xn = _rms(x_ref[...], g1_ref[...]).astype(BF16)

    pa = jnp.dot(xn, w_ref[:, 0:PA_COLS], preferred_element_type=F32)

    qg = qg_ref[...]
    for pair in range(N_HEADS // 2):
        qp = pa[:, OFF_Q + pair * LANES:OFF_Q + (pair + 1) * LANES].T
        for half in range(2):
            qh = qp[half * HEAD_DIM:(half + 1) * HEAD_DIM, :]
            ms = jnp.mean(qh * qh, axis=0, keepdims=True)
            qt_ref[2 * pair + half] = (qh * lax.rsqrt(ms + EPS) * qg * SCALE).astype(BF16)

    kci_ref[...] = pa[:, OFF_KC:OFF_KC + KVW]
    vci_ref[...] = pa[:, OFF_VC:OFF_VC + KVW]

    pos = i * tm + lax.broadcasted_iota(jnp.int32, (tm, 1), 0)
    key_lanes = _alibi_key_lanes(pos)
    tail = _value_tail()
    for g in range(N_KV):
        ks = pa[:, OFF_KS + g * LANES:OFF_KS + (g + 1) * LANES]
        ks_ref[g] = (_rms_padded(ks, kg_ref[1:2, :]) + key_lanes).astype(BF16)
        kw = pa[:, OFF_KW + g * LANES:OFF_KW + (g + 1) * LANES]
        kw_ref[g] = (_rms_padded(kw, kg_ref[2:3, :]) + key_lanes).astype(BF16)
    vs_t = pa[:, OFF_VS:OFF_VS + KVW].T.astype(BF16)
    vw_t = pa[:, OFF_VW:OFF_VW + KVW].T.astype(BF16)
    for g in range(N_KV):
        for c in range(tm // TQ):
            vst_ref[g, c, 0:HEAD_DIM, :] = vs_t[g * HEAD_DIM:(g + 1) * HEAD_DIM, c * TQ:(c + 1) * TQ]
            vst_ref[g, c, HEAD_DIM:VROWS, :] = tail
            vwt_ref[g, c, 0:HEAD_DIM, :] = vw_t[g * HEAD_DIM:(g + 1) * HEAD_DIM, c * TQ:(c + 1) * TQ]
            vwt_ref[g, c, HEAD_DIM:VROWS, :] = tail
        gates = jax.nn.sigmoid(pa[:, OFF_GN + g * LANES:OFF_GN + (g + 1) * LANES])
        gnt_ref[g] = gates.T[0:GROWS, :]

    def col(o):
        return jnp.dot(xn, w_ref[:, o:o + D_MODEL], preferred_element_type=F32)

    u = col(OFF_CONV_C) * col(OFF_CONV_X)

    @pl.when(i == 0)
    def _():
        u_scr[0:8, :] = jnp.zeros((8, D_MODEL), F32)

    u_scr[8:8 + tm, :] = u
    cw = cw_ref[...]
    y = (cw[2:3, :] * u + cw[1:2, :] * u_scr[7:7 + tm, :]
         + cw[0:1, :] * u_scr[6:6 + tm, :])
    u_scr[0:8, :] = u_scr[tm:tm + 8, :]
    z = col(OFF_CONV_B) * y
    zb = jnp.dot(z.astype(BF16), wb_ref[...], preferred_element_type=F32)
    ob_ref[...] = jax.nn.sigmoid(col(OFF_GATE_B)) * zb
    sga_ref[...] = jax.nn.sigmoid(col(OFF_GATE_A))


def _proj_call(x2, g1, w_in_p, w_b, qg, kg, cw, B, S, tm):
    T = B * S
    nt = S // tm
    row = lambda b, i: (b * nt + i, 0)
    k_shape = jax.ShapeDtypeStruct((B, N_KV, S, LANES), BF16)
    k_spec = pl.BlockSpec((None, N_KV, tm, LANES), lambda b, i: (b, 0, i, 0))
    vt_shape = jax.ShapeDtypeStruct((B, N_KV, S // TQ, VROWS, TQ), BF16)
    vt_spec = pl.BlockSpec((None, N_KV, tm // TQ, VROWS, TQ), lambda b, i: (b, 0, i, 0, 0))
    return pl.pallas_call(
        _proj_kernel,
        grid=(B, nt),
        in_specs=[
            pl.BlockSpec((tm, D_MODEL), row),
            _const_spec((1, D_MODEL)),
            _const_spec((D_MODEL, W_IN_COLS)),
            _const_spec((D_MODEL, D_MODEL)),
            _const_spec((HEAD_DIM, 1)),
            _const_spec((3, LANES)),
            _const_spec((CONV_K, D_MODEL)),
        ],
        out_specs=[
            pl.BlockSpec((None, N_HEADS, HEAD_DIM, tm), lambda b, i: (b, 0, 0, i)),
            pl.BlockSpec((tm, KVW), row),
            pl.BlockSpec((tm, KVW), row),
            k_spec, vt_spec, k_spec, vt_spec,
            pl.BlockSpec((None, N_KV, GROWS, tm), lambda b, i: (b, 0, 0, i)),
            pl.BlockSpec((tm, D_MODEL), row),
            pl.BlockSpec((tm, D_MODEL), row),
        ],
        out_shape=[
            jax.ShapeDtypeStruct((B, N_HEADS, HEAD_DIM, S), BF16),
            jax.ShapeDtypeStruct((T, KVW), F32),
            jax.ShapeDtypeStruct((T, KVW), F32),
            k_shape, vt_shape, k_shape, vt_shape,
            jax.ShapeDtypeStruct((B, N_KV, GROWS, S), F32),
            jax.ShapeDtypeStruct((T, D_MODEL), F32),
            jax.ShapeDtypeStruct((T, D_MODEL), F32),
        ],
        scratch_shapes=[pltpu.VMEM((tm + 8, D_MODEL), F32)],
        compiler_params=pltpu.CompilerParams(
            dimension_semantics=("arbitrary", "arbitrary"),
            vmem_limit_bytes=VMEM_LIMIT),
        name="proj",
    )(x2, g1, w_in_p, w_b, qg, kg, cw)


def _cmp_kernel(kci_ref, vci_ref, pos_ref, w_ref, kg_ref, kc_ref, vct_ref, b_scr):
    ncp = kc_ref.shape[1]
    acc_a = jnp.zeros((ncp, CMP_OUT), F32)
    acc_b = jnp.zeros((ncp, CMP_OUT), F32)
    for l in range(CMP_STRIDE):
        rows = jnp.concatenate([kci_ref[pl.ds(l, ncp, stride=CMP_STRIDE), :],
                                vci_ref[pl.ds(l, ncp, stride=CMP_STRIDE), :]], axis=1)
        xa = (rows + pos_ref[l:l + 1, :]).astype(BF16)
        xb = (rows + pos_ref[CMP_STRIDE + l:CMP_STRIDE + l + 1, :]).astype(BF16)
        acc_a = acc_a + jnp.dot(xa, w_ref[l], preferred_element_type=F32)
        acc_b = acc_b + jnp.dot(xb, w_ref[CMP_STRIDE + l], preferred_element_type=F32)
    b_scr[0:ncp, :] = acc_b
    b_scr[ncp:ncp + 8, :] = jnp.zeros((8, CMP_OUT), F32)
    kcv = acc_a + b_scr[1:ncp + 1, :]
    c_end = lax.broadcasted_iota(jnp.int32, (ncp, 1), 0) * CMP_STRIDE + (CMP_BLOCK - 1)
    key_lanes = _alibi_key_lanes(c_end)
    vc_t = kcv[:, 2 * LANES:CMP_OUT].T.astype(BF16)
    tail = jnp.concatenate([_value_tail()] * (ncp // LANES), axis=1)
    for g in range(N_KV):
        kc = _rms_padded(kcv[:, g * LANES:(g + 1) * LANES], kg_ref[0:1, :])
        kc_ref[g] = (kc + key_lanes).astype(BF16)
        vct_ref[g, 0:HEAD_DIM, :] = vc_t[g * HEAD_DIM:(g + 1) * HEAD_DIM, :]
        vct_ref[g, HEAD_DIM:VROWS, :] = tail


def _cmp_call(kci, vci, pos4, w_bd, kg, B, S):
    ncp = S // CMP_STRIDE
    return pl.pallas_call(
        _cmp_kernel,
        grid=(B,),
        in_specs=[
            pl.BlockSpec((S, KVW), lambda b: (b, 0)),
            pl.BlockSpec((S, KVW), lambda b: (b, 0)),
            _const_spec((CMP_BLOCK, 2 * KVW)),
            _const_spec((CMP_BLOCK, 2 * KVW, CMP_OUT)),
            _const_spec((3, LANES)),
        ],
        out_specs=[
            pl.BlockSpec((None, N_KV, ncp, LANES), lambda b: (b, 0, 0, 0)),
            pl.BlockSpec((None, N_KV, VROWS, ncp), lambda b: (b, 0, 0, 0)),
        ],
        out_shape=[
            jax.ShapeDtypeStruct((B, N_KV, ncp, LANES), BF16),
            jax.ShapeDtypeStruct((B, N_KV, VROWS, ncp), BF16),
        ],
        scratch_shapes=[pltpu.VMEM((ncp + 8, CMP_OUT), F32)],
        compiler_params=pltpu.CompilerParams(
            dimension_semantics=("arbitrary",), vmem_limit_bytes=VMEM_LIMIT),
        name="compress",
    )(kci, vci, pos4, w_bd, kg)


def _attn_kernel(qt_ref, kc_ref, vct_ref, ks_ref, vst_ref, kw_ref, vwt_ref, gnt_ref,
                 ovl_ref, oh_ref, o_ref, score_scr, cnt_scr, acc_s, acc_w):
    g = pl.program_id(1)
    i = pl.program_id(2)
    t0 = i * TQ
    ns = ovl_ref.shape[0]
    ncp = ovl_ref.shape[1]

    lane = lax.broadcasted_iota(jnp.int32, (1, NL), 1)
    tq = t0 + (lane & (TQ - 1))
    head = g * GQA + (lane >> 7) + 1
    slope = lax.bitcast_convert_type((127 - head) << 23, F32)
    a_t = (tq >> 6).astype(F32)
    b_t = (tq & 63).astype(F32)
    sub = lax.broadcasted_iota(jnp.int32, (HEAD_DIM, NL), 0)
    alibi = jnp.where(sub == 0, -(slope * 64.0) * a_t,
                      jnp.where(sub == 1, -slope * b_t,
                                jnp.where(sub == 2, slope * 64.0,
                                          jnp.where(sub == 3, slope, 0.0)))).astype(BF16)
    qt = jnp.concatenate([qt_ref[r] for r in range(GQA)], axis=1)
    qa = jnp.concatenate([qt, alibi], axis=0)

    s = jnp.dot(kc_ref[...], qa, preferred_element_type=F32)
    c_end = lax.broadcasted_iota(jnp.int32, (ncp, 1), 0) * CMP_STRIDE + (CMP_BLOCK - 1)
    mask = tq >= c_end
    s = jnp.where(mask, s, NEG_INF)
    m = jnp.max(s, axis=0, keepdims=True)
    e = jnp.where(mask, jnp.exp(s - m), 0.0)
    inv = 1.0 / jnp.maximum(jnp.sum(e, axis=0, keepdims=True), 1e-30)
    p = (e * inv).astype(BF16)
    o_cmp = jnp.dot(vct_ref[...], p, preferred_element_type=F32)[0:HEAD_DIM, :]

    imp4 = jnp.dot(ovl_ref[...], p, preferred_element_type=F32)
    imp = imp4[:, 0:TQ]
    for r in range(1, GQA):
        imp = imp + imp4[:, r * TQ:(r + 1) * TQ]
    j = lax.broadcasted_iota(jnp.int32, (ns, TQ), 0)
    cur = (t0 + lax.broadcasted_iota(jnp.int32, (ns, TQ), 1)) >> 6
    forced = (j == 0) | (j == cur) | (j == cur - 1)
    score = jnp.where(forced, FORCE_SCORE, jnp.where(j <= cur, imp, -1.0))
    score_scr[...] = score
    cnt_scr[...] = jnp.zeros((ns, TQ), jnp.int32)

    row8 = lax.broadcasted_iota(jnp.int32, (8, TQ), 0)
    for kg in range(ns // 8):
        @pl.when(8 * kg <= 2 * i + 1)
        def _(kg=kg):
            cnt = [cnt_scr[8 * v:8 * v + 8, :] for v in range(ns // 8)]
            for b in range(8 * kg, 8 * kg + 8):
                sb = score_scr[b:b + 1, :]
                for v in range(ns // 8):
                    sc_v = score[8 * v:8 * v + 8, :]
                    if v < kg:
                        ahead = sb > sc_v
                    elif v > kg:
                        ahead = sb >= sc_v
                    else:
                        ahead = (sb > sc_v) | ((sb == sc_v) & (row8 > b - 8 * kg))
                    cnt[v] = cnt[v] + jnp.where(ahead, 1, 0)
            for v in range(ns // 8):
                cnt_scr[8 * v:8 * v + 8, :] = cnt[v]

    sel = (cnt_scr[...] < N_SLC) & (j <= cur)
    selb = jnp.where(sel, 0.0, NEG_INF).astype(BF16)
    selb = jnp.concatenate([selb, jnp.zeros((LANES - ns, TQ), BF16)], axis=0)
    qa_sel = jnp.concatenate([qa, jnp.concatenate([selb] * GQA, axis=1)], axis=0)

    pos_col = lax.broadcasted_iota(jnp.int32, (TQ, 1), 0)

    def tile(k_ref, kt):
        return k_ref[pl.ds(pl.multiple_of(kt * TQ, TQ), TQ), :]

    def sel_scores(kt):
        lhs = jnp.concatenate([tile(ks_ref, kt), tile(oh_ref, kt)], axis=1)
        return jnp.dot(lhs, qa_sel, preferred_element_type=F32)

    def win_scores(kt):
        return jnp.dot(tile(kw_ref, kt), qa, preferred_element_type=F32)

    chains = ((sel_scores, vst_ref, acc_s), (win_scores, vwt_ref, acc_w))

    def run_tiles(work, m_run):
        m_run = list(m_run)
        score = lambda w: chains[w[0]][0](w[1])
        ahead = [score(w) for w in work[:LOOKAHEAD]]
        for n, (c, kt, mk) in enumerate(work):
            sc = ahead.pop(0)
            if n + LOOKAHEAD < len(work):
                ahead.append(score(work[n + LOOKAHEAD]))
            if mk is not None:
                sc = jnp.where(mk, sc, NEG_INF)
            _, vt_ref, acc_ref = chains[c]
            mx = jnp.max(sc, axis=0, keepdims=True)
            if m_run[c] is None:
                m_new = mx
                pr = jnp.exp(sc - m_new).astype(BF16)
                acc_ref[...] = jnp.dot(vt_ref[kt], pr, preferred_element_type=F32)
            else:
                m_new = jnp.maximum(m_run[c], mx)
                alpha = jnp.exp(m_run[c] - m_new)
                pr = jnp.exp(sc - m_new).astype(BF16)
                acc_ref[...] = alpha * acc_ref[...] + jnp.dot(vt_ref[kt], pr,
                                                               preferred_element_type=F32)
            m_run[c] = m_new
        return m_run

    def finish(acc_ref):
        acc = acc_ref[...]
        return acc[0:HEAD_DIM, :] / jnp.maximum(acc[HEAD_DIM:HEAD_DIM + 1, :], 1e-30)

    def causal(kt):
        return tq >= (kt * TQ + pos_col)

    def interleave(a, b):
        out = []
        for n in range(max(len(a), len(b))):
            out += a[n:n + 1] + b[n:n + 1]
        return out

    half = UNROLL // 2
    n_half = i // half
    n_win = WINDOW // TQ
    sel_first = [(0, n_half * half + u, causal(n_half * half + u)) for u in range(half)]

    def first_block(win_kts, win_masks):
        win = [(1, kt, mk) for kt, mk in zip(win_kts, win_masks)]
        return run_tiles(interleave(sel_first, win), [None, None])[0]

    def win_steady():
        kts = [i - d for d in range(n_win + 1)]
        far = (tq - (kts[-1] * TQ + pos_col)) < WINDOW
        return first_block(kts, [causal(i)] + [None] * (n_win - 1) + [far])

    def win_start():
        kts = list(range(n_win))
        return first_block(kts, [causal(kt) for kt in kts])

    m_run = lax.cond(i >= n_win, win_steady, win_start)
    o_win = finish(acc_w)

    def sel_group(base, n, m_in):
        return run_tiles([(0, base + u, None) for u in range(n)], [m_in, None])[0]

    m_run = lax.cond(n_half % 2 == 1,
                     lambda m_in: sel_group((n_half - 1) * half, half, m_in),
                     lambda m_in: m_in, m_run)
    lax.fori_loop(0, i // UNROLL, lambda grp, m_in: sel_group(grp * UNROLL, UNROLL, m_in), m_run)
    o_slc = finish(acc_s)

    def gate(branch):
        return jnp.concatenate(
            [gnt_ref[branch * GQA + r:branch * GQA + r + 1, :] for r in range(GQA)], axis=1)

    o = gate(0) * o_cmp + gate(1) * o_slc + gate(2) * o_win
    halves = [jnp.concatenate([o[:, (2 * h) * TQ:(2 * h + 1) * TQ],
                               o[:, (2 * h + 1) * TQ:(2 * h + 2) * TQ]], axis=0).T
              for h in range(GQA // 2)]
    o_ref[...] = jnp.concatenate(halves, axis=1).astype(BF16)


def _attn_call(qt, kc, vct, ks, vst, kw, vwt, gnt, ovl, onehot, B, S):
    T = B * S
    nq = S // TQ
    ncp = S // CMP_STRIDE
    ns = S // SLC_BLOCK
    bg4 = lambda b, g, i: (b, g, 0, 0)
    bg5 = lambda b, g, i: (b, g, 0, 0, 0)
    k_spec = pl.BlockSpec((None, None, S, LANES), bg4)
    vt_spec = pl.BlockSpec((None, None, nq, VROWS, TQ), bg5)
    return pl.pallas_call(
        _attn_kernel,
        grid=(B, N_KV, nq),
        in_specs=[
            pl.BlockSpec((None, GQA, HEAD_DIM, TQ), lambda b, g, i: (b, g, 0, i)),
            pl.BlockSpec((None, None, ncp, LANES), bg4),
            pl.BlockSpec((None, None, VROWS, ncp), bg4),
            k_spec, vt_spec, k_spec, vt_spec,
            pl.BlockSpec((None, None, GROWS, TQ), lambda b, g, i: (b, g, 0, i)),
            _const_spec((ns, ncp)),
            _const_spec((S, LANES)),
        ],
        out_specs=pl.BlockSpec((TQ, GQA * HEAD_DIM), lambda b, g, i: (b * nq + i, g)),
        out_shape=jax.ShapeDtypeStruct((T, QW), BF16),
        scratch_shapes=[pltpu.VMEM((ns, TQ), F32), pltpu.VMEM((ns, TQ), jnp.int32),
                        pltpu.VMEM((VROWS, NL), F32), pltpu.VMEM((VROWS, NL), F32)],
        compiler_params=pltpu.CompilerParams(
            dimension_semantics=("arbitrary", "arbitrary", "arbitrary"),
            vmem_limit_bytes=VMEM_LIMIT),
        name="attn",
    )(qt, kc, vct, ks, vst, kw, vwt, gnt, ovl, onehot)


FF_CHUNK = 512


def _mlp_kernel(x_ref, o_ref, sga_ref, ob_ref, wa_ref, wo_ref, g2_ref, wu_ref, wd_ref,
                out_ref, acc_scr):
    a = jnp.dot(o_ref[...], wa_ref[...], preferred_element_type=F32)
    mixed = sga_ref[...] * a + ob_ref[...]
    x1 = x_ref[...] + jnp.dot(mixed.astype(BF16), wo_ref[...], preferred_element_type=F32)
    h = _rms(x1, g2_ref[...]).astype(BF16)
    acc_scr[...] = x1
    for c in range(D_FF // FF_CHUNK):
        lo, hi = c * FF_CHUNK, (c + 1) * FF_CHUNK
        up = jnp.dot(h, wu_ref[:, lo:hi], preferred_element_type=F32)
        act = jnp.square(jnp.maximum(up, 0.0)).astype(BF16)
        acc_scr[...] += jnp.dot(act, wd_ref[lo:hi, :], preferred_element_type=F32)
    out_ref[...] = acc_scr[...]


def _mlp_call(x2, o_nsa, sga, ob, w_a, w_o, g2, w_up, w_down, tm):
    T = x2.shape[0]
    row = lambda i: (i, 0)
    return pl.pallas_call(
        _mlp_kernel,
        grid=(T // tm,),
        in_specs=[
            pl.BlockSpec((tm, D_MODEL), row),
            pl.BlockSpec((tm, QW), row),
            pl.BlockSpec((tm, D_MODEL), row),
            pl.BlockSpec((tm, D_MODEL), row),
            _const_spec((QW, D_MODEL)),
            _const_spec((D_MODEL, D_MODEL)),
            _const_spec((1, D_MODEL)),
            _const_spec((D_MODEL, D_FF)),
            _const_spec((D_FF, D_MODEL)),
        ],
        out_specs=pl.BlockSpec((tm, D_MODEL), row),
        out_shape=jax.ShapeDtypeStruct((T, D_MODEL), F32),
        scratch_shapes=[pltpu.VMEM((tm, D_MODEL), F32)],
        compiler_params=pltpu.CompilerParams(
            dimension_semantics=("arbitrary",), vmem_limit_bytes=VMEM_LIMIT),
        name="mlp",
    )(x2, o_nsa, sga, ob, w_a, w_o, g2, w_up, w_down)


def _pad_heads(w):
    w = w.reshape(D_MODEL, N_KV, HEAD_DIM)
    return jnp.pad(w, ((0, 0), (0, 0), (0, LANES - HEAD_DIM))).reshape(D_MODEL, N_KV * LANES)


def _pack_w_in(w_in):
    o = QW
    k_c, v_c, k_s, v_s, k_w, v_w = (w_in[:, o + n * KVW:o + (n + 1) * KVW] for n in range(6))
    o += 6 * KVW
    gates = w_in[:, o:o + 3 * N_HEADS]
    gates = gates.reshape(D_MODEL, 3, N_KV, GQA).transpose(0, 2, 1, 3)
    gates = gates.reshape(D_MODEL, N_KV, 3 * GQA)
    gates = jnp.pad(gates, ((0, 0), (0, 0), (0, LANES - 3 * GQA))).reshape(D_MODEL, N_KV * LANES)
    rest = w_in[:, o + 3 * N_HEADS:]
    return jnp.concatenate([w_in[:, :QW], k_c, v_c, _pad_heads(k_s), v_s, _pad_heads(k_w), v_w,
                            gates, rest], axis=1).astype(BF16)


def _pack_cmp(w_cmp_k, w_cmp_v, cmp_pos_k, cmp_pos_v):
    wk = w_cmp_k.reshape(CMP_BLOCK, HEAD_DIM, HEAD_DIM)
    wv = w_cmp_v.reshape(CMP_BLOCK, HEAD_DIM, HEAD_DIM)
    w = jnp.zeros((CMP_BLOCK, 2 * KVW, CMP_OUT), F32)
    for slot, (blk, col) in enumerate(((wk, 0), (wk, LANES), (wv, 2 * LANES),
                                       (wv, 2 * LANES + HEAD_DIM))):
        w = w.at[:, slot * HEAD_DIM:(slot + 1) * HEAD_DIM, col:col + HEAD_DIM].set(blk)
    pos4 = jnp.concatenate([cmp_pos_k, cmp_pos_k, cmp_pos_v, cmp_pos_v], axis=1)
    return w.astype(BF16), pos4


def _pad_gain(g):
    return jnp.pad(g, ((0, 0), (0, LANES - HEAD_DIM)))


def _overlap(S):
    ncp = S // CMP_STRIDE
    ns = S // SLC_BLOCK
    c_start = np.arange(ncp) * CMP_STRIDE
    s_start = np.arange(ns) * SLC_BLOCK
    ov = np.clip(np.minimum(c_start[None, :] + CMP_BLOCK, s_start[:, None] + SLC_BLOCK)
                 - np.maximum(c_start[None, :], s_start[:, None]), 0, None)
    return jnp.asarray(ov.astype(np.float32) / CMP_BLOCK, dtype=BF16)


def _block_onehot(S):
    pos = np.arange(S)
    oh = np.zeros((S, LANES), np.float32)
    oh[pos, pos // SLC_BLOCK] = 1.0
    return jnp.asarray(oh, dtype=BF16)


def _layer(x, norm1_g, w_in, q_norm_g, k_norm_g, cmp_pos_k, cmp_pos_v, w_cmp_k, w_cmp_v,
           conv_w, w_branch_a, w_branch_b, w_out, norm2_g, w_up, w_down):
    B, S, _ = x.shape
    assert S % (CMP_STRIDE * LANES) == 0 and S // SLC_BLOCK <= LANES
    assert (S // TQ) % UNROLL == 0 and S >= WINDOW
    x2 = x.reshape(B * S, D_MODEL)
    w_bd, pos4 = _pack_cmp(w_cmp_k, w_cmp_v, cmp_pos_k, cmp_pos_v)
    kg = _pad_gain(k_norm_g)
    qt, kci, vci, ks, vst, kw, vwt, gnt, sga, ob = _proj_call(
        x2, norm1_g[None, :], _pack_w_in(w_in), w_branch_b.astype(BF16),
        q_norm_g[:, None], kg, conv_w, B, S, tm=512)
    kc, vct = _cmp_call(kci, vci, pos4, w_bd, kg, B, S)
    o_nsa = _attn_call(qt, kc, vct, ks, vst, kw, vwt, gnt, _overlap(S), _block_onehot(S), B, S)
    out = _mlp_call(x2, o_nsa, sga, ob, w_branch_a.astype(BF16), w_out.astype(BF16),
                    norm2_g[None, :], w_up.astype(BF16), w_down.astype(BF16), tm=512)
    return out.reshape(B, S, D_MODEL)


@jax.jit
def kernel(x, norm1_g, w_in, q_norm_g, k_norm_g, cmp_pos_k, cmp_pos_v, w_cmp_k, w_cmp_v,
           conv_w, w_branch_a, w_branch_b, w_out, norm2_g, w_up, w_down):
    for l in range(norm1_g.shape[0]):
        x = _layer(x, norm1_g[l], w_in[l], q_norm_g[l], k_norm_g[l], cmp_pos_k[l],
                   cmp_pos_v[l], w_cmp_k[l], w_cmp_v[l], conv_w[l], w_branch_a[l],
                   w_branch_b[l], w_out[l], norm2_g[l], w_up[l], w_down[l])
    return x
```

```python
import numpy as np
import jax
import jax.numpy as jnp
from jax import lax
from jax.experimental import pallas as pl
from jax.experimental.pallas import tpu as pltpu

D_MODEL = 1024
N_HEADS = 8
HEAD_DIM = 64
N_KV = 2
GQA = N_HEADS // N_KV
CMP_BLOCK = 32
CMP_STRIDE = 16
SLC_BLOCK = 64
N_SLC = 16
WINDOW = 512
FORCE_SCORE = 1e4
SCALE = 0.125
CONV_K = 3
D_FF = 4 * D_MODEL
EPS = 1e-6
NEG_INF = -1e30

LANES = 128
QW = N_HEADS * HEAD_DIM
KVW = N_KV * HEAD_DIM

OFF_Q = 0
OFF_KC = OFF_Q + QW
OFF_VC = OFF_KC + KVW
OFF_KS = OFF_VC + KVW
OFF_VS = OFF_KS + N_KV * LANES
OFF_KW = OFF_VS + KVW
OFF_VW = OFF_KW + N_KV * LANES
OFF_GN = OFF_VW + KVW
PA_COLS = OFF_GN + N_KV * LANES
OFF_CONV_B = PA_COLS
OFF_CONV_C = OFF_CONV_B + D_MODEL
OFF_CONV_X = OFF_CONV_C + D_MODEL
OFF_GATE_A = OFF_CONV_X + D_MODEL
OFF_GATE_B = OFF_GATE_A + D_MODEL
W_IN_COLS = OFF_GATE_B + D_MODEL

PROJ_SUB = 256
TQ = 128
NL = GQA * TQ
UNROLL = 8
LOOKAHEAD = 4
VROWS = 80
GROWS = 16
CMP_OUT = 2 * LANES + KVW
VMEM_LIMIT = 56 * 1024 * 1024

F32 = jnp.float32
BF16 = jnp.bfloat16


def _rms(x, g):
    return x * lax.rsqrt(jnp.mean(x * x, axis=-1, keepdims=True) + EPS) * g


def _rms_padded(x, g):
    ms = jnp.sum(x * x, axis=-1, keepdims=True) * (1.0 / HEAD_DIM)
    return x * lax.rsqrt(ms + EPS) * g


def _alibi_key_lanes(pos):
    lane = lax.broadcasted_iota(jnp.int32, (pos.shape[0], LANES), 1)
    hi = (pos >> 6).astype(F32)
    lo = (pos & 63).astype(F32)
    return jnp.where((lane == HEAD_DIM) | (lane == HEAD_DIM + 1) | (lane == HEAD_DIM + 4), 1.0,
                     jnp.where(lane == HEAD_DIM + 2, hi,
                               jnp.where(lane == HEAD_DIM + 3, lo, 0.0)))


def _value_tail():
    r = lax.broadcasted_iota(jnp.int32, (VROWS - HEAD_DIM, LANES), 0)
    return jnp.where(r == 0, 1.0, 0.0).astype(BF16)


def _const_spec(shape):
    zeros = (0,) * len(shape)
    return pl.BlockSpec(shape, lambda *_: zeros, pipeline_mode=pl.Buffered(1))


def _proj_kernel(x_ref, g1_ref, w_ref, wb_ref, qg_ref, kg_ref, cw_ref,
                 qt_ref, kci_ref, vci_ref, ks_ref, vst_ref, kw_ref, vwt_ref, gnt_ref,
                 sga_ref, ob_ref, u_scr):
    i = pl.program_id(1)
    tm = x_ref.shape[0]

    @pl.when(i == 0)
    def _():
        u_scr[0:8, :] = jnp.zeros((8, D_MODEL), F32)

    for r0 in range(0, tm, PROJ_SUB):
        _proj_rows(r0, i * tm + r0, x_ref, g1_ref, w_ref, wb_ref, qg_ref, kg_ref, cw_ref,
                   qt_ref, kci_ref, vci_ref, ks_ref, vst_ref, kw_ref, vwt_ref, gnt_ref,
                   sga_ref, ob_ref, u_scr)
    u_scr[0:8, :] = u_scr[tm:tm + 8, :]


def _proj_rows(r0, pos0, x_ref, g1_ref, w_ref, wb_ref, qg_ref, kg_ref, cw_ref,
               qt_ref, kci_ref, vci_ref, ks_ref, vst_ref, kw_ref, vwt_ref, gnt_ref,
               sga_ref, ob_ref, u_scr):
    n = PROJ_SUB
    rows = slice(r0, r0 + n)
    xn = _rms(x_ref[rows, :], g1_ref[...]).astype(BF16)

    pa = jnp.dot(xn, w_ref[:, 0:PA_COLS], preferred_element_type=F32)

    qg = qg_ref[...]
    for pair in range(N_HEADS // 2):
        qp = pa[:, OFF_Q + pair * LANES:OFF_Q + (pair + 1) * LANES].T
        for half in range(2):
            qh = qp[half * HEAD_DIM:(half + 1) * HEAD_DIM, :]
            ms = jnp.mean(qh * qh, axis=0, keepdims=True)
            qt_ref[2 * pair + half, :, rows] = (qh * lax.rsqrt(ms + EPS) * qg * SCALE).astype(BF16)

    kci_ref[rows, :] = pa[:, OFF_KC:OFF_KC + KVW]
    vci_ref[rows, :] = pa[:, OFF_VC:OFF_VC + KVW]

    pos = pos0 + lax.broadcasted_iota(jnp.int32, (n, 1), 0)
    key_lanes = _alibi_key_lanes(pos)
    tail = _value_tail()
    for g in range(N_KV):
        ks = pa[:, OFF_KS + g * LANES:OFF_KS + (g + 1) * LANES]
        ks_ref[g, rows, :] = (_rms_padded(ks, kg_ref[1:2, :]) + key_lanes).astype(BF16)
        kw = pa[:, OFF_KW + g * LANES:OFF_KW + (g + 1) * LANES]
        kw_ref[g, rows, :] = (_rms_padded(kw, kg_ref[2:3, :]) + key_lanes).astype(BF16)
    vs_t = pa[:, OFF_VS:OFF_VS + KVW].T.astype(BF16)
    vw_t = pa[:, OFF_VW:OFF_VW + KVW].T.astype(BF16)
    for g in range(N_KV):
        for c in range(n // TQ):
            ct = r0 // TQ + c
            vst_ref[g, ct, 0:HEAD_DIM, :] = vs_t[g * HEAD_DIM:(g + 1) * HEAD_DIM, c * TQ:(c + 1) * TQ]
            vst_ref[g, ct, HEAD_DIM:VROWS, :] = tail
            vwt_ref[g, ct, 0:HEAD_DIM, :] = vw_t[g * HEAD_DIM:(g + 1) * HEAD_DIM, c * TQ:(c + 1) * TQ]
            vwt_ref[g, ct, HEAD_DIM:VROWS, :] = tail
        gates = jax.nn.sigmoid(pa[:, OFF_GN + g * LANES:OFF_GN + (g + 1) * LANES])
        gnt_ref[g, :, rows] = gates.T[0:GROWS, :]

    def col(o):
        return jnp.dot(xn, w_ref[:, o:o + D_MODEL], preferred_element_type=F32)

    u = col(OFF_CONV_C) * col(OFF_CONV_X)
    u_scr[8 + r0:8 + r0 + n, :] = u
    cw = cw_ref[...]
    y = (cw[2:3, :] * u + cw[1:2, :] * u_scr[7 + r0:7 + r0 + n, :]
         + cw[0:1, :] * u_scr[6 + r0:6 + r0 + n, :])
    z = col(OFF_CONV_B) * y
    zb = jnp.dot(z.astype(BF16), wb_ref[...], preferred_element_type=F32)
    ob_ref[rows, :] = jax.nn.sigmoid(col(OFF_GATE_B)) * zb
    sga_ref[rows, :] = jax.nn.sigmoid(col(OFF_GATE_A))


def _proj_call(x2, g1, w_in_p, w_b, qg, kg, cw, B, S, tm):
    T = B * S
    nt = S // tm
    row = lambda b, i: (b * nt + i, 0)
    k_shape = jax.ShapeDtypeStruct((B, N_KV, S, LANES), BF16)
    k_spec = pl.BlockSpec((None, N_KV, tm, LANES), lambda b, i: (b, 0, i, 0))
    vt_shape = jax.ShapeDtypeStruct((B, N_KV, S // TQ, VROWS, TQ), BF16)
    vt_spec = pl.BlockSpec((None, N_KV, tm // TQ, VROWS, TQ), lambda b, i: (b, 0, i, 0, 0))
    return pl.pallas_call(
        _proj_kernel,
        grid=(B, nt),
        in_specs=[
            pl.BlockSpec((tm, D_MODEL), row),
            _const_spec((1, D_MODEL)),
            _const_spec((D_MODEL, W_IN_COLS)),
            _const_spec((D_MODEL, D_MODEL)),
            _const_spec((HEAD_DIM, 1)),
            _const_spec((3, LANES)),
            _const_spec((CONV_K, D_MODEL)),
        ],
        out_specs=[
            pl.BlockSpec((None, N_HEADS, HEAD_DIM, tm), lambda b, i: (b, 0, 0, i)),
            pl.BlockSpec((tm, KVW), row),
            pl.BlockSpec((tm, KVW), row),
            k_spec, vt_spec, k_spec, vt_spec,
            pl.BlockSpec((None, N_KV, GROWS, tm), lambda b, i: (b, 0, 0, i)),
            pl.BlockSpec((tm, D_MODEL), row),
            pl.BlockSpec((tm, D_MODEL), row),
        ],
        out_shape=[
            jax.ShapeDtypeStruct((B, N_HEADS, HEAD_DIM, S), BF16),
            jax.ShapeDtypeStruct((T, KVW), F32),
            jax.ShapeDtypeStruct((T, KVW), F32),
            k_shape, vt_shape, k_shape, vt_shape,
            jax.ShapeDtypeStruct((B, N_KV, GROWS, S), F32),
            jax.ShapeDtypeStruct((T, D_MODEL), F32),
            jax.ShapeDtypeStruct((T, D_MODEL), F32),
        ],
        scratch_shapes=[pltpu.VMEM((tm + 8, D_MODEL), F32)],
        compiler_params=pltpu.CompilerParams(
            dimension_semantics=("arbitrary", "arbitrary"),
            vmem_limit_bytes=VMEM_LIMIT),
        name="proj",
    )(x2, g1, w_in_p, w_b, qg, kg, cw)


def _cmp_kernel(kci_ref, vci_ref, pos_ref, w_ref, kg_ref, kc_ref, vct_ref, b_scr):
    ncp = kc_ref.shape[1]
    acc_a = jnp.zeros((ncp, CMP_OUT), F32)
    acc_b = jnp.zeros((ncp, CMP_OUT), F32)
    for l in range(CMP_STRIDE):
        rows = jnp.concatenate([kci_ref[pl.ds(l, ncp, stride=CMP_STRIDE), :],
                                vci_ref[pl.ds(l, ncp, stride=CMP_STRIDE), :]], axis=1)
        xa = (rows + pos_ref[l:l + 1, :]).astype(BF16)
        xb = (rows + pos_ref[CMP_STRIDE + l:CMP_STRIDE + l + 1, :]).astype(BF16)
        acc_a = acc_a + jnp.dot(xa, w_ref[l], preferred_element_type=F32)
        acc_b = acc_b + jnp.dot(xb, w_ref[CMP_STRIDE + l], preferred_element_type=F32)
    b_scr[0:ncp, :] = acc_b
    b_scr[ncp:ncp + 8, :] = jnp.zeros((8, CMP_OUT), F32)
    kcv = acc_a + b_scr[1:ncp + 1, :]
    c_end = lax.broadcasted_iota(jnp.int32, (ncp, 1), 0) * CMP_STRIDE + (CMP_BLOCK - 1)
    key_lanes = _alibi_key_lanes(c_end)
    vc_t = kcv[:, 2 * LANES:CMP_OUT].T.astype(BF16)
    tail = jnp.concatenate([_value_tail()] * (ncp // LANES), axis=1)
    for g in range(N_KV):
        kc = _rms_padded(kcv[:, g * LANES:(g + 1) * LANES], kg_ref[0:1, :])
        kc_ref[g] = (kc + key_lanes).astype(BF16)
        vct_ref[g, 0:HEAD_DIM, :] = vc_t[g * HEAD_DIM:(g + 1) * HEAD_DIM, :]
        vct_ref[g, HEAD_DIM:VROWS, :] = tail


def _cmp_call(kci, vci, pos4, w_bd, kg, B, S):
    ncp = S // CMP_STRIDE
    return pl.pallas_call(
        _cmp_kernel,
        grid=(B,),
        in_specs=[
            pl.BlockSpec((S, KVW), lambda b: (b, 0)),
            pl.BlockSpec((S, KVW), lambda b: (b, 0)),
            _const_spec((CMP_BLOCK, 2 * KVW)),
            _const_spec((CMP_BLOCK, 2 * KVW, CMP_OUT)),
            _const_spec((3, LANES)),
        ],
        out_specs=[
            pl.BlockSpec((None, N_KV, ncp, LANES), lambda b: (b, 0, 0, 0)),
            pl.BlockSpec((None, N_KV, VROWS, ncp), lambda b: (b, 0, 0, 0)),
        ],
        out_shape=[
            jax.ShapeDtypeStruct((B, N_KV, ncp, LANES), BF16),
            jax.ShapeDtypeStruct((B, N_KV, VROWS, ncp), BF16),
        ],
        scratch_shapes=[pltpu.VMEM((ncp + 8, CMP_OUT), F32)],
        compiler_params=pltpu.CompilerParams(
            dimension_semantics=("arbitrary",), vmem_limit_bytes=VMEM_LIMIT),
        name="compress",
    )(kci, vci, pos4, w_bd, kg)


def _attn_kernel(qt_ref, kc_ref, vct_ref, ks_ref, vst_ref, kw_ref, vwt_ref, gnt_ref,
                 ovl_ref, oh_ref, cband_ref, eye_ref, o_ref, score_scr, cnt_scr, acc_s, acc_w):
    g = pl.program_id(1)
    i = pl.program_id(2)
    t0 = i * TQ
    ns = ovl_ref.shape[0]
    ncp = ovl_ref.shape[1]

    lane = lax.broadcasted_iota(jnp.int32, (1, NL), 1)
    tq = t0 + (lane & (TQ - 1))
    head = g * GQA + (lane >> 7) + 1
    slope = lax.bitcast_convert_type((127 - head) << 23, F32)
    a_t = (tq >> 6).astype(F32)
    b_t = (tq & 63).astype(F32)
    sub = lax.broadcasted_iota(jnp.int32, (HEAD_DIM, NL), 0)
    pos_col = lax.broadcasted_iota(jnp.int32, (TQ, 1), 0)
    qt = jnp.concatenate([qt_ref[r] for r in range(GQA)], axis=1)

    def aug_query(ref):
        rows = jnp.where(sub == 0, -(slope * 64.0) * a_t,
                         jnp.where(sub == 1, -slope * b_t,
                                   jnp.where(sub == 2, slope * 64.0,
                                             jnp.where(sub == 3, slope,
                                                       jnp.where(sub == 4, -ref, 0.0)))))
        return jnp.concatenate([qt, rows.astype(BF16)], axis=0)

    def tile(k_ref, kt):
        return k_ref[pl.ds(pl.multiple_of(kt * TQ, TQ), TQ), :]

    def causal(kt):
        return tq >= (kt * TQ + pos_col)

    def self_score(k_ref):
        k_t = tile(k_ref, i).astype(F32).T[0:HEAD_DIM, :]
        return jnp.sum(qt.astype(F32) * jnp.concatenate([k_t] * GQA, axis=1),
                       axis=0, keepdims=True)

    def plain_tiles(score_fn, vt_ref, acc_ref, kts, masks, start, hook=None):
        ahead = [score_fn(kt) for kt in kts[:LOOKAHEAD]]
        if hook is not None:
            hook()
        total = None if start else acc_ref[...]
        for n, (kt, mk) in enumerate(zip(kts, masks)):
            sc = ahead.pop(0)
            if n + LOOKAHEAD < len(kts):
                ahead.append(score_fn(kts[n + LOOKAHEAD]))
            if mk is not None:
                sc = jnp.where(mk, sc, NEG_INF)
            pv = jnp.dot(vt_ref[kt], jnp.exp(sc).astype(BF16), preferred_element_type=F32)
            total = pv if total is None else total + pv
        acc_ref[...] = total

    def online_tile(score_fn, vt_ref, acc_ref, kt, mask, m_run):
        sc = jnp.where(mask, score_fn(kt), NEG_INF)
        mx = jnp.max(sc, axis=0, keepdims=True)
        if m_run is None:
            m_new = mx
            pr = jnp.exp(sc - m_new).astype(BF16)
            acc_ref[...] = jnp.dot(vt_ref[kt], pr, preferred_element_type=F32)
        else:
            m_new = jnp.maximum(m_run, mx)
            pr = jnp.exp(sc - m_new).astype(BF16)
            acc_ref[...] = jnp.exp(m_run - m_new) * acc_ref[...] + jnp.dot(
                vt_ref[kt], pr, preferred_element_type=F32)
        return m_new

    def finish(acc_ref):
        acc = acc_ref[...]
        return acc[0:HEAD_DIM, :] / jnp.maximum(acc[HEAD_DIM:HEAD_DIM + 1, :], 1e-30)

    band = cband_ref[pl.ds(pl.multiple_of(ncp - (TQ // CMP_STRIDE) * i, 8), ncp), :]
    lhs_c = jnp.concatenate([kc_ref[...], band.astype(BF16)], axis=1)
    rhs_c = jnp.concatenate([aug_query(0.0), eye_ref[...]], axis=0)
    has_cmp = tq >= CMP_BLOCK - 1
    qa_win = aug_query(self_score(kw_ref))

    def win_scores(kt):
        return jnp.dot(tile(kw_ref, kt), qa_win, preferred_element_type=F32)

    def cmp_and_window(win_kts, win_masks):
        s = jnp.dot(lhs_c, rhs_c, preferred_element_type=F32)
        out = []

        def cmp_rest():
            e = jnp.exp(s - jnp.max(s, axis=0, keepdims=True))
            den = jnp.maximum(jnp.sum(e, axis=0, keepdims=True), 1e-30)
            p = (e * jnp.where(has_cmp, 1.0 / den, 0.0)).astype(BF16)
            out.append(jnp.dot(vct_ref[...], p, preferred_element_type=F32)[0:HEAD_DIM, :])
            out.append(jnp.dot(ovl_ref[...], p, preferred_element_type=F32))

        plain_tiles(win_scores, vwt_ref, acc_w, win_kts, win_masks, True, hook=cmp_rest)
        return out[0], out[1]

    n_win = WINDOW // TQ

    def win_steady():
        kts = [i - d for d in range(n_win + 1)]
        far = (tq - (kts[-1] * TQ + pos_col)) < WINDOW
        return cmp_and_window(kts, [causal(i)] + [None] * (n_win - 1) + [far])

    def win_start():
        kts = list(range(n_win))
        return cmp_and_window(kts, [causal(kt) for kt in kts])

    o_cmp, imp4 = lax.cond(i >= n_win, win_steady, win_start)
    o_win = finish(acc_w)

    imp = imp4[:, 0:TQ]
    for r in range(1, GQA):
        imp = imp + imp4[:, r * TQ:(r + 1) * TQ]
    j = lax.broadcasted_iota(jnp.int32, (ns, TQ), 0)
    cur = (t0 + lax.broadcasted_iota(jnp.int32, (ns, TQ), 1)) >> 6
    forced = (j == 0) | (j == cur) | (j == cur - 1)
    score = jnp.where(forced, FORCE_SCORE, jnp.where(j <= cur, imp, -1.0))
    score_scr[...] = score
    cnt_scr[...] = jnp.zeros((ns, TQ), jnp.int32)

    row8 = lax.broadcasted_iota(jnp.int32, (8, TQ), 0)
    for kg in range(ns // 8):
        @pl.when(8 * kg <= 2 * i + 1)
        def _(kg=kg):
            cnt = [cnt_scr[8 * v:8 * v + 8, :] for v in range(ns // 8)]
            for b in range(8 * kg, 8 * kg + 8):
                sb = score_scr[b:b + 1, :]
                for v in range(ns // 8):
                    sc_v = score[8 * v:8 * v + 8, :]
                    if v < kg:
                        ahead = sb > sc_v
                    elif v > kg:
                        ahead = sb >= sc_v
                    else:
                        ahead = (sb > sc_v) | ((sb == sc_v) & (row8 > b - 8 * kg))
                    cnt[v] = cnt[v] + jnp.where(ahead, 1, 0)
            for v in range(ns // 8):
                cnt_scr[8 * v:8 * v + 8, :] = cnt[v]

    sel = (cnt_scr[...] < N_SLC) & (j <= cur)
    selb = jnp.where(sel, 0.0, NEG_INF).astype(BF16)
    selb = jnp.concatenate([selb, jnp.zeros((LANES - ns, TQ), BF16)], axis=0)

    sel_rows = jnp.concatenate([selb] * GQA, axis=1)
    qa_sel = jnp.concatenate([aug_query(self_score(ks_ref)), sel_rows], axis=0)

    def sel_scores(kt):
        lhs = jnp.concatenate([tile(ks_ref, kt), tile(oh_ref, kt)], axis=1)
        return jnp.dot(lhs, qa_sel, preferred_element_type=F32)

    def sel_group(base, n, masked, start):
        kts = [base + u for u in range(n)]
        masks = [causal(kt) if masked else None for kt in kts]
        plain_tiles(sel_scores, vst_ref, acc_s, kts, masks, start)

    half = UNROLL // 2
    n_half = i // half
    diag = [n_half * half + u for u in range(half)]

    @pl.when(n_half % 2 == 0)
    def _():
        plain_tiles(sel_scores, vst_ref, acc_s, diag, [causal(kt) for kt in diag], True)

    @pl.when(n_half % 2 == 1)
    def _():
        before = [(n_half - 1) * half + u for u in range(half)]
        plain_tiles(sel_scores, vst_ref, acc_s, diag + before,
                    [causal(kt) for kt in diag] + [None] * half, True)

    def sel_body(grp, carry):
        sel_group(grp * UNROLL, UNROLL, False, False)
        return carry

    lax.fori_loop(0, i // UNROLL, sel_body, 0)
    o_slc = finish(acc_s)

    def gate(branch):
        return jnp.concatenate(
            [gnt_ref[branch * GQA + r:branch * GQA + r + 1, :] for r in range(GQA)], axis=1)

    def emit(o_slc, o_win):
        o = gate(0) * o_cmp + gate(1) * o_slc + gate(2) * o_win
        halves = [jnp.concatenate([o[:, (2 * h) * TQ:(2 * h + 1) * TQ],
                                   o[:, (2 * h + 1) * TQ:(2 * h + 2) * TQ]], axis=0).T
                  for h in range(GQA // 2)]
        o_ref[...] = jnp.concatenate(halves, axis=1).astype(BF16)

    emit(o_slc, o_win)

    bad = jnp.max(jnp.where(jnp.isfinite(o_slc) & jnp.isfinite(o_win), 0.0, 1.0))

    @pl.when(bad > 0.0)
    def _():
        m_run = online_tile(sel_scores, vst_ref, acc_s, i, causal(i), None)
        lax.fori_loop(0, i, lambda kt, m_in: online_tile(sel_scores, vst_ref, acc_s, kt,
                                                         causal(kt), m_in), m_run)
        m_run = None
        for d in range(n_win + 1):
            pos = (i - d) * TQ + pos_col
            inside = (tq >= pos) & (tq - pos < WINDOW) & (pos >= 0)
            m_run = online_tile(win_scores, vwt_ref, acc_w, jnp.maximum(i - d, 0), inside, m_run)
        emit(finish(acc_s), finish(acc_w))


def _attn_call(qt, kc, vct, ks, vst, kw, vwt, gnt, ovl, onehot, cband, eye, B, S):
    T = B * S
    nq = S // TQ
    ncp = S // CMP_STRIDE
    ns = S // SLC_BLOCK
    bg4 = lambda b, g, i: (b, g, 0, 0)
    bg5 = lambda b, g, i: (b, g, 0, 0, 0)
    k_spec = pl.BlockSpec((None, None, S, LANES), bg4)
    vt_spec = pl.BlockSpec((None, None, nq, VROWS, TQ), bg5)
    return pl.pallas_call(
        _attn_kernel,
        grid=(B, N_KV, nq),
        in_specs=[
            pl.BlockSpec((None, GQA, HEAD_DIM, TQ), lambda b, g, i: (b, g, 0, i)),
            pl.BlockSpec((None, None, ncp, LANES), bg4),
            pl.BlockSpec((None, None, VROWS, ncp), bg4),
            k_spec, vt_spec, k_spec, vt_spec,
            pl.BlockSpec((None, None, GROWS, TQ), lambda b, g, i: (b, g, 0, i)),
            _const_spec((ns, ncp)),
            _const_spec((S, LANES)),
            _const_spec((2 * ncp, TQ)),
            _const_spec((TQ, NL)),
        ],
        out_specs=pl.BlockSpec((TQ, GQA * HEAD_DIM), lambda b, g, i: (b * nq + i, g)),
        out_shape=jax.ShapeDtypeStruct((T, QW), BF16),
        scratch_shapes=[pltpu.VMEM((ns, TQ), F32), pltpu.VMEM((ns, TQ), jnp.int32),
                        pltpu.VMEM((VROWS, NL), F32), pltpu.VMEM((VROWS, NL), F32)],
        compiler_params=pltpu.CompilerParams(
            dimension_semantics=("arbitrary", "arbitrary", "arbitrary"),
            vmem_limit_bytes=VMEM_LIMIT),
        name="attn",
    )(qt, kc, vct, ks, vst, kw, vwt, gnt, ovl, onehot, cband, eye)


FF_CHUNK = 512


def _mlp_kernel(x_ref, o_ref, sga_ref, ob_ref, wa_ref, wo_ref, g2_ref, wu_ref, wd_ref,
                out_ref, acc_scr):
    a = jnp.dot(o_ref[...], wa_ref[...], preferred_element_type=F32)
    mixed = sga_ref[...] * a + ob_ref[...]
    x1 = x_ref[...] + jnp.dot(mixed.astype(BF16), wo_ref[...], preferred_element_type=F32)
    h = _rms(x1, g2_ref[...]).astype(BF16)
    acc_scr[...] = x1
    for c in range(D_FF // FF_CHUNK):
        lo, hi = c * FF_CHUNK, (c + 1) * FF_CHUNK
        up = jnp.dot(h, wu_ref[:, lo:hi], preferred_element_type=F32)
        act = jnp.square(jnp.maximum(up, 0.0)).astype(BF16)
        acc_scr[...] += jnp.dot(act, wd_ref[lo:hi, :], preferred_element_type=F32)
    out_ref[...] = acc_scr[...]


def _mlp_call(x2, o_nsa, sga, ob, w_a, w_o, g2, w_up, w_down, tm):
    T = x2.shape[0]
    row = lambda i: (i, 0)
    return pl.pallas_call(
        _mlp_kernel,
        grid=(T // tm,),
        in_specs=[
            pl.BlockSpec((tm, D_MODEL), row),
            pl.BlockSpec((tm, QW), row),
            pl.BlockSpec((tm, D_MODEL), row),
            pl.BlockSpec((tm, D_MODEL), row),
            _const_spec((QW, D_MODEL)),
            _const_spec((D_MODEL, D_MODEL)),
            _const_spec((1, D_MODEL)),
            _const_spec((D_MODEL, D_FF)),
            _const_spec((D_FF, D_MODEL)),
        ],
        out_specs=pl.BlockSpec((tm, D_MODEL), row),
        out_shape=jax.ShapeDtypeStruct((T, D_MODEL), F32),
        scratch_shapes=[pltpu.VMEM((tm, D_MODEL), F32)],
        compiler_params=pltpu.CompilerParams(
            dimension_semantics=("arbitrary",), vmem_limit_bytes=VMEM_LIMIT),
        name="mlp",
    )(x2, o_nsa, sga, ob, w_a, w_o, g2, w_up, w_down)


def _pad_heads(w):
    w = w.reshape(D_MODEL, N_KV, HEAD_DIM)
    return jnp.pad(w, ((0, 0), (0, 0), (0, LANES - HEAD_DIM))).reshape(D_MODEL, N_KV * LANES)


def _pack_w_in(w_in):
    o = QW
    k_c, v_c, k_s, v_s, k_w, v_w = (w_in[:, o + n * KVW:o + (n + 1) * KVW] for n in range(6))
    o += 6 * KVW
    gates = w_in[:, o:o + 3 * N_HEADS]
    gates = gates.reshape(D_MODEL, 3, N_KV, GQA).transpose(0, 2, 1, 3)
    gates = gates.reshape(D_MODEL, N_KV, 3 * GQA)
    gates = jnp.pad(gates, ((0, 0), (0, 0), (0, LANES - 3 * GQA))).reshape(D_MODEL, N_KV * LANES)
    rest = w_in[:, o + 3 * N_HEADS:]
    return jnp.concatenate([w_in[:, :QW], k_c, v_c, _pad_heads(k_s), v_s, _pad_heads(k_w), v_w,
                            gates, rest], axis=1).astype(BF16)


def _pack_cmp(w_cmp_k, w_cmp_v, cmp_pos_k, cmp_pos_v):
    wk = w_cmp_k.reshape(CMP_BLOCK, HEAD_DIM, HEAD_DIM)
    wv = w_cmp_v.reshape(CMP_BLOCK, HEAD_DIM, HEAD_DIM)
    slabs = [jnp.pad(blk.astype(BF16), ((0, 0), (0, 0), (col, CMP_OUT - col - HEAD_DIM)))
             for blk, col in ((wk, 0), (wk, LANES), (wv, 2 * LANES), (wv, 2 * LANES + HEAD_DIM))]
    pos4 = jnp.concatenate([cmp_pos_k, cmp_pos_k, cmp_pos_v, cmp_pos_v], axis=1)
    return jnp.concatenate(slabs, axis=1), pos4


def _pad_gain(g):
    return jnp.pad(g, ((0, 0), (0, LANES - HEAD_DIM)))


def _overlap(S):
    ncp = S // CMP_STRIDE
    ns = S // SLC_BLOCK
    c_start = np.arange(ncp) * CMP_STRIDE
    s_start = np.arange(ns) * SLC_BLOCK
    ov = np.clip(np.minimum(c_start[None, :] + CMP_BLOCK, s_start[:, None] + SLC_BLOCK)
                 - np.maximum(c_start[None, :], s_start[:, None]), 0, None)
    return jnp.asarray(ov.astype(np.float32) / CMP_BLOCK, dtype=BF16)


def _block_onehot(S):
    pos = np.arange(S)
    oh = np.zeros((S, LANES), np.float32)
    oh[pos, pos // SLC_BLOCK] = 1.0
    return jnp.asarray(oh, dtype=BF16)


def _cmp_band(S):
    ncp = S // CMP_STRIDE
    c_rel = np.arange(2 * ncp)[:, None] - ncp
    visible = c_rel * CMP_STRIDE + (CMP_BLOCK - 1) <= np.arange(TQ)[None, :]
    return jnp.asarray(np.where(visible, 0.0, NEG_INF), dtype=F32)


def _tiled_eye():
    return jnp.asarray(np.tile(np.eye(TQ, dtype=np.float32), (1, GQA)), dtype=BF16)


def _layer(x, norm1_g, w_in, q_norm_g, k_norm_g, cmp_pos_k, cmp_pos_v, w_cmp_k, w_cmp_v,
           conv_w, w_branch_a, w_branch_b, w_out, norm2_g, w_up, w_down):
    B, S, _ = x.shape
    assert S % (CMP_STRIDE * LANES) == 0 and S // SLC_BLOCK <= LANES
    assert (S // TQ) % UNROLL == 0 and S >= WINDOW
    x2 = x.reshape(B * S, D_MODEL)
    w_bd, pos4 = _pack_cmp(w_cmp_k, w_cmp_v, cmp_pos_k, cmp_pos_v)
    kg = _pad_gain(k_norm_g)
    qt, kci, vci, ks, vst, kw, vwt, gnt, sga, ob = _proj_call(
        x2, norm1_g[None, :], _pack_w_in(w_in), w_branch_b.astype(BF16),
        q_norm_g[:, None], kg, conv_w, B, S, tm=512)
    kc, vct = _cmp_call(kci, vci, pos4, w_bd, kg, B, S)
    o_nsa = _attn_call(qt, kc, vct, ks, vst, kw, vwt, gnt, _overlap(S), _block_onehot(S),
                       _cmp_band(S), _tiled_eye(), B, S)
    out = _mlp_call(x2, o_nsa, sga, ob, w_branch_a.astype(BF16), w_out.astype(BF16),
                    norm2_g[None, :], w_up.astype(BF16), w_down.astype(BF16), tm=512)
    return out.reshape(B, S, D_MODEL)


@jax.jit
def kernel(x, norm1_g, w_in, q_norm_g, k_norm_g, cmp_pos_k, cmp_pos_v, w_cmp_k, w_cmp_v,
           conv_w, w_branch_a, w_branch_b, w_out, norm2_g, w_up, w_down):
    for l in range(norm1_g.shape[0]):
        x = _layer(x, norm1_g[l], w_in[l], q_norm_g[l], k_norm_g[l], cmp_pos_k[l],
                   cmp_pos_v[l], w_cmp_k[l], w_cmp_v[l], conv_w[l], w_branch_a[l],
                   w_branch_b[l], w_out[l], norm2_g[l], w_up[l], w_down[l])
    return x
```

```python
import numpy as np
import jax
import jax.numpy as jnp
from jax import lax
from jax.experimental import pallas as pl
from jax.experimental.pallas import tpu as pltpu

D_MODEL = 1024
N_HEADS = 8
HEAD_DIM = 64
N_KV = 2
GQA = N_HEADS // N_KV
CMP_BLOCK = 32
CMP_STRIDE = 16
SLC_BLOCK = 64
N_SLC = 16
WINDOW = 512
FORCE_SCORE = 1e4
SCALE = 0.125
CONV_K = 3
D_FF = 4 * D_MODEL
EPS = 1e-6
NEG_INF = -1e30

LANES = 128
QW = N_HEADS * HEAD_DIM
KVW = N_KV * HEAD_DIM

OFF_Q = 0
OFF_KC = OFF_Q + QW
OFF_VC = OFF_KC + KVW
OFF_KS = OFF_VC + KVW
OFF_VS = OFF_KS + N_KV * LANES
OFF_KW = OFF_VS + KVW
OFF_VW = OFF_KW + N_KV * LANES
OFF_GN = OFF_VW + KVW
PA_COLS = OFF_GN + N_KV * LANES
OFF_CONV_B = PA_COLS
OFF_CONV_C = OFF_CONV_B + D_MODEL
OFF_CONV_X = OFF_CONV_C + D_MODEL
OFF_GATE_A = OFF_CONV_X + D_MODEL
OFF_GATE_B = OFF_GATE_A + D_MODEL
W_IN_COLS = OFF_GATE_B + D_MODEL

PROJ_SUB = 256
TQ = 128
QT = 256
QT_SHIFT = QT.bit_length() - 1
TILES_Q = QT // TQ
NL = GQA * QT
UNROLL = 8
LOOKAHEAD = 4
VROWS = 80
GROWS = 16
CMP_OUT = 2 * LANES + KVW
VMEM_LIMIT = 56 * 1024 * 1024

F32 = jnp.float32
BF16 = jnp.bfloat16


def _rms(x, g):
    return x * lax.rsqrt(jnp.mean(x * x, axis=-1, keepdims=True) + EPS) * g


def _rms_padded(x, g):
    ms = jnp.sum(x * x, axis=-1, keepdims=True) * (1.0 / HEAD_DIM)
    return x * lax.rsqrt(ms + EPS) * g


def _alibi_key_lanes(pos):
    lane = lax.broadcasted_iota(jnp.int32, (pos.shape[0], LANES), 1)
    hi = (pos >> 6).astype(F32)
    lo = (pos & 63).astype(F32)
    return jnp.where((lane == HEAD_DIM) | (lane == HEAD_DIM + 1) | (lane == HEAD_DIM + 4), 1.0,
                     jnp.where(lane == HEAD_DIM + 2, hi,
                               jnp.where(lane == HEAD_DIM + 3, lo, 0.0)))


def _value_tail():
    r = lax.broadcasted_iota(jnp.int32, (VROWS - HEAD_DIM, LANES), 0)
    return jnp.where(r == 0, 1.0, 0.0).astype(BF16)


def _const_spec(shape):
    zeros = (0,) * len(shape)
    return pl.BlockSpec(shape, lambda *_: zeros, pipeline_mode=pl.Buffered(1))


def _proj_kernel(x_ref, g1_ref, w_ref, wb_ref, qg_ref, kg_ref, cw_ref,
                 qt_ref, kci_ref, vci_ref, ks_ref, vst_ref, kw_ref, vwt_ref, gnt_ref,
                 sga_ref, ob_ref, u_scr):
    i = pl.program_id(1)
    tm = x_ref.shape[0]

    @pl.when(i == 0)
    def _():
        u_scr[0:8, :] = jnp.zeros((8, D_MODEL), F32)

    for r0 in range(0, tm, PROJ_SUB):
        _proj_rows(r0, i * tm + r0, x_ref, g1_ref, w_ref, wb_ref, qg_ref, kg_ref, cw_ref,
                   qt_ref, kci_ref, vci_ref, ks_ref, vst_ref, kw_ref, vwt_ref, gnt_ref,
                   sga_ref, ob_ref, u_scr)
    u_scr[0:8, :] = u_scr[tm:tm + 8, :]


def _proj_rows(r0, pos0, x_ref, g1_ref, w_ref, wb_ref, qg_ref, kg_ref, cw_ref,
               qt_ref, kci_ref, vci_ref, ks_ref, vst_ref, kw_ref, vwt_ref, gnt_ref,
               sga_ref, ob_ref, u_scr):
    n = PROJ_SUB
    rows = slice(r0, r0 + n)
    xn = _rms(x_ref[rows, :], g1_ref[...]).astype(BF16)

    pa = jnp.dot(xn, w_ref[:, 0:PA_COLS], preferred_element_type=F32)

    qg = qg_ref[...]
    for pair in range(N_HEADS // 2):
        qp = pa[:, OFF_Q + pair * LANES:OFF_Q + (pair + 1) * LANES].T
        for half in range(2):
            qh = qp[half * HEAD_DIM:(half + 1) * HEAD_DIM, :]
            ms = jnp.mean(qh * qh, axis=0, keepdims=True)
            qt_ref[2 * pair + half, :, rows] = (qh * lax.rsqrt(ms + EPS) * qg * SCALE).astype(BF16)

    kci_ref[rows, :] = pa[:, OFF_KC:OFF_KC + KVW]
    vci_ref[rows, :] = pa[:, OFF_VC:OFF_VC + KVW]

    pos = pos0 + lax.broadcasted_iota(jnp.int32, (n, 1), 0)
    key_lanes = _alibi_key_lanes(pos)
    tail = _value_tail()
    for g in range(N_KV):
        ks = pa[:, OFF_KS + g * LANES:OFF_KS + (g + 1) * LANES]
        ks_ref[g, rows, :] = (_rms_padded(ks, kg_ref[1:2, :]) + key_lanes).astype(BF16)
        kw = pa[:, OFF_KW + g * LANES:OFF_KW + (g + 1) * LANES]
        kw_ref[g, rows, :] = (_rms_padded(kw, kg_ref[2:3, :]) + key_lanes).astype(BF16)
    vs_t = pa[:, OFF_VS:OFF_VS + KVW].T.astype(BF16)
    vw_t = pa[:, OFF_VW:OFF_VW + KVW].T.astype(BF16)
    for g in range(N_KV):
        for c in range(n // TQ):
            ct = r0 // TQ + c
            vst_ref[g, ct, 0:HEAD_DIM, :] = vs_t[g * HEAD_DIM:(g + 1) * HEAD_DIM, c * TQ:(c + 1) * TQ]
            vst_ref[g, ct, HEAD_DIM:VROWS, :] = tail
            vwt_ref[g, ct, 0:HEAD_DIM, :] = vw_t[g * HEAD_DIM:(g + 1) * HEAD_DIM, c * TQ:(c + 1) * TQ]
            vwt_ref[g, ct, HEAD_DIM:VROWS, :] = tail
        gates = jax.nn.sigmoid(pa[:, OFF_GN + g * LANES:OFF_GN + (g + 1) * LANES])
        gnt_ref[g, :, rows] = gates.T[0:GROWS, :]

    def col(o):
        return jnp.dot(xn, w_ref[:, o:o + D_MODEL], preferred_element_type=F32)

    u = col(OFF_CONV_C) * col(OFF_CONV_X)
    u_scr[8 + r0:8 + r0 + n, :] = u
    cw = cw_ref[...]
    y = (cw[2:3, :] * u + cw[1:2, :] * u_scr[7 + r0:7 + r0 + n, :]
         + cw[0:1, :] * u_scr[6 + r0:6 + r0 + n, :])
    z = col(OFF_CONV_B) * y
    zb = jnp.dot(z.astype(BF16), wb_ref[...], preferred_element_type=F32)
    ob_ref[rows, :] = jax.nn.sigmoid(col(OFF_GATE_B)) * zb
    sga_ref[rows, :] = jax.nn.sigmoid(col(OFF_GATE_A))


def _proj_call(x2, g1, w_in_p, w_b, qg, kg, cw, B, S, tm):
    T = B * S
    nt = S // tm
    row = lambda b, i: (b * nt + i, 0)
    k_shape = jax.ShapeDtypeStruct((B, N_KV, S, LANES), BF16)
    k_spec = pl.BlockSpec((None, N_KV, tm, LANES), lambda b, i: (b, 0, i, 0))
    vt_shape = jax.ShapeDtypeStruct((B, N_KV, S // TQ, VROWS, TQ), BF16)
    vt_spec = pl.BlockSpec((None, N_KV, tm // TQ, VROWS, TQ), lambda b, i: (b, 0, i, 0, 0))
    return pl.pallas_call(
        _proj_kernel,
        grid=(B, nt),
        in_specs=[
            pl.BlockSpec((tm, D_MODEL), row),
            _const_spec((1, D_MODEL)),
            _const_spec((D_MODEL, W_IN_COLS)),
            _const_spec((D_MODEL, D_MODEL)),
            _const_spec((HEAD_DIM, 1)),
            _const_spec((3, LANES)),
            _const_spec((CONV_K, D_MODEL)),
        ],
        out_specs=[
            pl.BlockSpec((None, N_HEADS, HEAD_DIM, tm), lambda b, i: (b, 0, 0, i)),
            pl.BlockSpec((tm, KVW), row),
            pl.BlockSpec((tm, KVW), row),
            k_spec, vt_spec, k_spec, vt_spec,
            pl.BlockSpec((None, N_KV, GROWS, tm), lambda b, i: (b, 0, 0, i)),
            pl.BlockSpec((tm, D_MODEL), row),
            pl.BlockSpec((tm, D_MODEL), row),
        ],
        out_shape=[
            jax.ShapeDtypeStruct((B, N_HEADS, HEAD_DIM, S), BF16),
            jax.ShapeDtypeStruct((T, KVW), F32),
            jax.ShapeDtypeStruct((T, KVW), F32),
            k_shape, vt_shape, k_shape, vt_shape,
            jax.ShapeDtypeStruct((B, N_KV, GROWS, S), F32),
            jax.ShapeDtypeStruct((T, D_MODEL), F32),
            jax.ShapeDtypeStruct((T, D_MODEL), F32),
        ],
        scratch_shapes=[pltpu.VMEM((tm + 8, D_MODEL), F32)],
        compiler_params=pltpu.CompilerParams(
            dimension_semantics=("arbitrary", "arbitrary"),
            vmem_limit_bytes=VMEM_LIMIT),
        name="proj",
    )(x2, g1, w_in_p, w_b, qg, kg, cw)


def _cmp_kernel(kci_ref, vci_ref, pos_ref, w_ref, kg_ref, kc_ref, vct_ref, b_scr):
    ncp = kc_ref.shape[1]
    acc_a = jnp.zeros((ncp, CMP_OUT), F32)
    acc_b = jnp.zeros((ncp, CMP_OUT), F32)
    for l in range(CMP_STRIDE):
        rows = jnp.concatenate([kci_ref[pl.ds(l, ncp, stride=CMP_STRIDE), :],
                                vci_ref[pl.ds(l, ncp, stride=CMP_STRIDE), :]], axis=1)
        xa = (rows + pos_ref[l:l + 1, :]).astype(BF16)
        xb = (rows + pos_ref[CMP_STRIDE + l:CMP_STRIDE + l + 1, :]).astype(BF16)
        acc_a = acc_a + jnp.dot(xa, w_ref[l], preferred_element_type=F32)
        acc_b = acc_b + jnp.dot(xb, w_ref[CMP_STRIDE + l], preferred_element_type=F32)
    b_scr[0:ncp, :] = acc_b
    b_scr[ncp:ncp + 8, :] = jnp.zeros((8, CMP_OUT), F32)
    kcv = acc_a + b_scr[1:ncp + 1, :]
    c_end = lax.broadcasted_iota(jnp.int32, (ncp, 1), 0) * CMP_STRIDE + (CMP_BLOCK - 1)
    key_lanes = _alibi_key_lanes(c_end)
    vc_t = kcv[:, 2 * LANES:CMP_OUT].T.astype(BF16)
    tail = jnp.concatenate([_value_tail()] * (ncp // LANES), axis=1)
    for g in range(N_KV):
        kc = _rms_padded(kcv[:, g * LANES:(g + 1) * LANES], kg_ref[0:1, :])
        kc_ref[g] = (kc + key_lanes).astype(BF16)
        vct_ref[g, 0:HEAD_DIM, :] = vc_t[g * HEAD_DIM:(g + 1) * HEAD_DIM, :]
        vct_ref[g, HEAD_DIM:VROWS, :] = tail


def _cmp_call(kci, vci, pos4, w_bd, kg, B, S):
    ncp = S // CMP_STRIDE
    return pl.pallas_call(
        _cmp_kernel,
        grid=(B,),
        in_specs=[
            pl.BlockSpec((S, KVW), lambda b: (b, 0)),
            pl.BlockSpec((S, KVW), lambda b: (b, 0)),
            _const_spec((CMP_BLOCK, 2 * KVW)),
            _const_spec((CMP_BLOCK, 2 * KVW, CMP_OUT)),
            _const_spec((3, LANES)),
        ],
        out_specs=[
            pl.BlockSpec((None, N_KV, ncp, LANES), lambda b: (b, 0, 0, 0)),
            pl.BlockSpec((None, N_KV, VROWS, ncp), lambda b: (b, 0, 0, 0)),
        ],
        out_shape=[
            jax.ShapeDtypeStruct((B, N_KV, ncp, LANES), BF16),
            jax.ShapeDtypeStruct((B, N_KV, VROWS, ncp), BF16),
        ],
        scratch_shapes=[pltpu.VMEM((ncp + 8, CMP_OUT), F32)],
        compiler_params=pltpu.CompilerParams(
            dimension_semantics=("arbitrary",), vmem_limit_bytes=VMEM_LIMIT),
        name="compress",
    )(kci, vci, pos4, w_bd, kg)


def _attn_kernel(qt_ref, kc_ref, vct_ref, ks_ref, vst_ref, kw_ref, vwt_ref, gnt_ref,
                 ovl_ref, oh_ref, cband_ref, eye_ref, o_ref, score_scr, cnt_scr, acc_s, acc_w):
    g = pl.program_id(1)
    i = pl.program_id(2)
    t0 = i * QT
    kt_last = TILES_Q * (i + 1) - 1
    ns = ovl_ref.shape[0]
    ncp = ovl_ref.shape[1]

    lane = lax.broadcasted_iota(jnp.int32, (1, NL), 1)
    tq = t0 + (lane & (QT - 1))
    head = g * GQA + (lane >> QT_SHIFT) + 1
    slope = lax.bitcast_convert_type((127 - head) << 23, F32)
    a_t = (tq >> 6).astype(F32)
    b_t = (tq & 63).astype(F32)
    sub = lax.broadcasted_iota(jnp.int32, (HEAD_DIM, NL), 0)
    pos_col = lax.broadcasted_iota(jnp.int32, (TQ, 1), 0)
    qt = jnp.concatenate([qt_ref[r] for r in range(GQA)], axis=1)

    def aug_query(ref):
        rows = jnp.where(sub == 0, -(slope * 64.0) * a_t,
                         jnp.where(sub == 1, -slope * b_t,
                                   jnp.where(sub == 2, slope * 64.0,
                                             jnp.where(sub == 3, slope,
                                                       jnp.where(sub == 4, -ref, 0.0)))))
        return jnp.concatenate([qt, rows.astype(BF16)], axis=0)

    def tile(k_ref, kt):
        return k_ref[pl.ds(pl.multiple_of(kt * TQ, TQ), TQ), :]

    def causal(kt):
        return tq >= (kt * TQ + pos_col)

    def self_score(k_ref):
        k_t = jnp.concatenate([tile(k_ref, TILES_Q * i + c).astype(F32).T[0:HEAD_DIM, :]
                               for c in range(TILES_Q)], axis=1)
        return jnp.sum(qt.astype(F32) * jnp.concatenate([k_t] * GQA, axis=1),
                       axis=0, keepdims=True)

    def plain_tiles(score_fn, vt_ref, acc_ref, kts, masks, start, hook=None):
        ahead = [score_fn(kt) for kt in kts[:LOOKAHEAD]]
        if hook is not None:
            hook()
        total = None if start else acc_ref[...]
        for n, (kt, mk) in enumerate(zip(kts, masks)):
            sc = ahead.pop(0)
            if n + LOOKAHEAD < len(kts):
                ahead.append(score_fn(kts[n + LOOKAHEAD]))
            if mk is not None:
                sc = jnp.where(mk, sc, NEG_INF)
            pv = jnp.dot(vt_ref[kt], jnp.exp(sc).astype(BF16), preferred_element_type=F32)
            total = pv if total is None else total + pv
        acc_ref[...] = total

    def online_tile(score_fn, vt_ref, acc_ref, kt, mask, m_run):
        sc = jnp.where(mask, score_fn(kt), NEG_INF)
        mx = jnp.max(sc, axis=0, keepdims=True)
        if m_run is None:
            m_new = mx
            pr = jnp.exp(sc - m_new).astype(BF16)
            acc_ref[...] = jnp.dot(vt_ref[kt], pr, preferred_element_type=F32)
        else:
            m_new = jnp.maximum(m_run, mx)
            pr = jnp.exp(sc - m_new).astype(BF16)
            acc_ref[...] = jnp.exp(m_run - m_new) * acc_ref[...] + jnp.dot(
                vt_ref[kt], pr, preferred_element_type=F32)
        return m_new

    def finish(acc_ref):
        acc = acc_ref[...]
        return acc[0:HEAD_DIM, :] / jnp.maximum(acc[HEAD_DIM:HEAD_DIM + 1, :], 1e-30)

    band = cband_ref[pl.ds(pl.multiple_of(ncp - (QT // CMP_STRIDE) * i, 8), ncp), :]
    lhs_c = jnp.concatenate([kc_ref[...], band.astype(BF16)], axis=1)
    rhs_c = jnp.concatenate([aug_query(0.0), eye_ref[...]], axis=0)
    has_cmp = tq >= CMP_BLOCK - 1
    qa_win = aug_query(self_score(kw_ref))

    def win_scores(kt):
        return jnp.dot(tile(kw_ref, kt), qa_win, preferred_element_type=F32)

    def cmp_and_window(win_kts, win_masks):
        s = jnp.dot(lhs_c, rhs_c, preferred_element_type=F32)
        out = []

        def cmp_rest():
            e = jnp.exp(s - jnp.max(s, axis=0, keepdims=True))
            den = jnp.maximum(jnp.sum(e, axis=0, keepdims=True), 1e-30)
            p = (e * jnp.where(has_cmp, 1.0 / den, 0.0)).astype(BF16)
            out.append(jnp.dot(vct_ref[...], p, preferred_element_type=F32)[0:HEAD_DIM, :])
            out.append(jnp.dot(ovl_ref[...], p, preferred_element_type=F32))

        plain_tiles(win_scores, vwt_ref, acc_w, win_kts, win_masks, True, hook=cmp_rest)
        return out[0], out[1]

    n_win = WINDOW // TQ
    n_wt = n_win + TILES_Q

    def win_steady():
        kts = [kt_last - d for d in range(n_wt)]
        masks = [causal(kt) for kt in kts[:TILES_Q]] + [None] * (n_win - TILES_Q)
        masks += [(tq - (kt * TQ + pos_col)) < WINDOW for kt in kts[n_win:]]
        return cmp_and_window(kts, masks)

    def win_start():
        kts = list(range(n_wt - 1))
        return cmp_and_window(kts, [causal(kt) for kt in kts])

    o_cmp, imp4 = lax.cond(kt_last >= n_wt - 1, win_steady, win_start)
    o_win = finish(acc_w)

    imp = imp4[:, 0:QT]
    for r in range(1, GQA):
        imp = imp + imp4[:, r * QT:(r + 1) * QT]
    j = lax.broadcasted_iota(jnp.int32, (ns, QT), 0)
    cur = (t0 + lax.broadcasted_iota(jnp.int32, (ns, QT), 1)) >> 6
    forced = (j == 0) | (j == cur) | (j == cur - 1)
    score = jnp.where(forced, FORCE_SCORE, jnp.where(j <= cur, imp, -1.0))
    score_scr[...] = score
    cnt_scr[...] = jnp.zeros((ns, QT), jnp.int32)

    row8 = lax.broadcasted_iota(jnp.int32, (8, QT), 0)
    for kg in range(ns // 8):
        @pl.when(8 * kg <= (TQ // SLC_BLOCK) * kt_last + 1)
        def _(kg=kg):
            cnt = [cnt_scr[8 * v:8 * v + 8, :] for v in range(ns // 8)]
            for b in range(8 * kg, 8 * kg + 8):
                sb = score_scr[b:b + 1, :]
                for v in range(ns // 8):
                    sc_v = score[8 * v:8 * v + 8, :]
                    if v < kg:
                        ahead = sb > sc_v
                    elif v > kg:
                        ahead = sb >= sc_v
                    else:
                        ahead = (sb > sc_v) | ((sb == sc_v) & (row8 > b - 8 * kg))
                    cnt[v] = cnt[v] + jnp.where(ahead, 1, 0)
            for v in range(ns // 8):
                cnt_scr[8 * v:8 * v + 8, :] = cnt[v]

    sel = (cnt_scr[...] < N_SLC) & (j <= cur)
    selb = jnp.where(sel, 0.0, NEG_INF).astype(BF16)
    selb = jnp.concatenate([selb, jnp.zeros((LANES - ns, QT), BF16)], axis=0)

    sel_rows = jnp.concatenate([selb] * GQA, axis=1)
    qa_sel = jnp.concatenate([aug_query(self_score(ks_ref)), sel_rows], axis=0)

    def sel_scores(kt):
        lhs = jnp.concatenate([tile(ks_ref, kt), tile(oh_ref, kt)], axis=1)
        return jnp.dot(lhs, qa_sel, preferred_element_type=F32)

    def sel_group(base, n, masked, start):
        kts = [base + u for u in range(n)]
        masks = [causal(kt) if masked else None for kt in kts]
        plain_tiles(sel_scores, vst_ref, acc_s, kts, masks, start)

    half = UNROLL // 2
    n_half = kt_last // half
    diag = [n_half * half + u for u in range(half)]

    @pl.when(n_half % 2 == 0)
    def _():
        plain_tiles(sel_scores, vst_ref, acc_s, diag, [causal(kt) for kt in diag], True)

    @pl.when(n_half % 2 == 1)
    def _():
        before = [(n_half - 1) * half + u for u in range(half)]
        plain_tiles(sel_scores, vst_ref, acc_s, diag + before,
                    [causal(kt) for kt in diag] + [None] * half, True)

    def sel_body(grp, carry):
        sel_group(grp * UNROLL, UNROLL, False, False)
        return carry

    lax.fori_loop(0, kt_last // UNROLL, sel_body, 0)
    o_slc = finish(acc_s)

    def gate(branch):
        return jnp.concatenate(
            [gnt_ref[branch * GQA + r:branch * GQA + r + 1, :] for r in range(GQA)], axis=1)

    def emit(o_slc, o_win):
        o = gate(0) * o_cmp + gate(1) * o_slc + gate(2) * o_win
        for h in range(GQA // 2):
            pair = jnp.concatenate([o[:, (2 * h) * QT:(2 * h + 1) * QT],
                                    o[:, (2 * h + 1) * QT:(2 * h + 2) * QT]], axis=0)
            for c in range(TILES_Q):
                o_ref[c * TQ:(c + 1) * TQ, h * LANES:(h + 1) * LANES] = (
                    pair[:, c * TQ:(c + 1) * TQ].T.astype(BF16))

    emit(o_slc, o_win)

    bad = jnp.max(jnp.where(jnp.isfinite(o_slc) & jnp.isfinite(o_win), 0.0, 1.0))

    @pl.when(bad > 0.0)
    def _():
        m_run = online_tile(sel_scores, vst_ref, acc_s, kt_last, causal(kt_last), None)
        lax.fori_loop(0, kt_last, lambda kt, m_in: online_tile(sel_scores, vst_ref, acc_s, kt,
                                                               causal(kt), m_in), m_run)
        m_run = None
        for d in list(range(TILES_Q - 1, n_wt)) + list(range(TILES_Q - 1)):
            pos = (kt_last - d) * TQ + pos_col
            inside = (tq >= pos) & (tq - pos < WINDOW) & (pos >= 0)
            m_run = online_tile(win_scores, vwt_ref, acc_w, jnp.maximum(kt_last - d, 0), inside,
                                m_run)
        emit(finish(acc_s), finish(acc_w))


def _attn_call(qt, kc, vct, ks, vst, kw, vwt, gnt, ovl, onehot, cband, eye, B, S):
    T = B * S
    nq = S // QT
    ncp = S // CMP_STRIDE
    ns = S // SLC_BLOCK
    bg4 = lambda b, g, i: (b, g, 0, 0)
    bg5 = lambda b, g, i: (b, g, 0, 0, 0)
    k_spec = pl.BlockSpec((None, None, S, LANES), bg4)
    vt_spec = pl.BlockSpec((None, None, S // TQ, VROWS, TQ), bg5)
    return pl.pallas_call(
        _attn_kernel,
        grid=(B, N_KV, nq),
        in_specs=[
            pl.BlockSpec((None, GQA, HEAD_DIM, QT), lambda b, g, i: (b, g, 0, i)),
            pl.BlockSpec((None, None, ncp, LANES), bg4),
            pl.BlockSpec((None, None, VROWS, ncp), bg4),
            k_spec, vt_spec, k_spec, vt_spec,
            pl.BlockSpec((None, None, GROWS, QT), lambda b, g, i: (b, g, 0, i)),
            _const_spec((ns, ncp)),
            _const_spec((S, LANES)),
            _const_spec((2 * ncp, QT)),
            _const_spec((QT, NL)),
        ],
        out_specs=pl.BlockSpec((QT, GQA * HEAD_DIM), lambda b, g, i: (b * nq + i, g)),
        out_shape=jax.ShapeDtypeStruct((T, QW), BF16),
        scratch_shapes=[pltpu.VMEM((ns, QT), F32), pltpu.VMEM((ns, QT), jnp.int32),
                        pltpu.VMEM((VROWS, NL), F32), pltpu.VMEM((VROWS, NL), F32)],
        compiler_params=pltpu.CompilerParams(
            dimension_semantics=("arbitrary", "arbitrary", "arbitrary"),
            vmem_limit_bytes=VMEM_LIMIT),
        name="attn",
    )(qt, kc, vct, ks, vst, kw, vwt, gnt, ovl, onehot, cband, eye)


FF_CHUNK = 512


def _mlp_kernel(x_ref, o_ref, sga_ref, ob_ref, wa_ref, wo_ref, g2_ref, wu_ref, wd_ref,
                out_ref, acc_scr):
    a = jnp.dot(o_ref[...], wa_ref[...], preferred_element_type=F32)
    mixed = sga_ref[...] * a + ob_ref[...]
    x1 = x_ref[...] + jnp.dot(mixed.astype(BF16), wo_ref[...], preferred_element_type=F32)
    h = _rms(x1, g2_ref[...]).astype(BF16)
    acc_scr[...] = x1
    for c in range(D_FF // FF_CHUNK):
        lo, hi = c * FF_CHUNK, (c + 1) * FF_CHUNK
        up = jnp.dot(h, wu_ref[:, lo:hi], preferred_element_type=F32)
        act = jnp.square(jnp.maximum(up, 0.0)).astype(BF16)
        acc_scr[...] += jnp.dot(act, wd_ref[lo:hi, :], preferred_element_type=F32)
    out_ref[...] = acc_scr[...]


def _mlp_call(x2, o_nsa, sga, ob, w_a, w_o, g2, w_up, w_down, tm):
    T = x2.shape[0]
    row = lambda i: (i, 0)
    return pl.pallas_call(
        _mlp_kernel,
        grid=(T // tm,),
        in_specs=[
            pl.BlockSpec((tm, D_MODEL), row),
            pl.BlockSpec((tm, QW), row),
            pl.BlockSpec((tm, D_MODEL), row),
            pl.BlockSpec((tm, D_MODEL), row),
            _const_spec((QW, D_MODEL)),
            _const_spec((D_MODEL, D_MODEL)),
            _const_spec((1, D_MODEL)),
            _const_spec((D_MODEL, D_FF)),
            _const_spec((D_FF, D_MODEL)),
        ],
        out_specs=pl.BlockSpec((tm, D_MODEL), row),
        out_shape=jax.ShapeDtypeStruct((T, D_MODEL), F32),
        scratch_shapes=[pltpu.VMEM((tm, D_MODEL), F32)],
        compiler_params=pltpu.CompilerParams(
            dimension_semantics=("arbitrary",), vmem_limit_bytes=VMEM_LIMIT),
        name="mlp",
    )(x2, o_nsa, sga, ob, w_a, w_o, g2, w_up, w_down)


def _pad_heads(w):
    w = w.reshape(D_MODEL, N_KV, HEAD_DIM)
    return jnp.pad(w, ((0, 0), (0, 0), (0, LANES - HEAD_DIM))).reshape(D_MODEL, N_KV * LANES)


def _pack_w_in(w_in):
    o = QW
    k_c, v_c, k_s, v_s, k_w, v_w = (w_in[:, o + n * KVW:o + (n + 1) * KVW] for n in range(6))
    o += 6 * KVW
    gates = w_in[:, o:o + 3 * N_HEADS]
    gates = gates.reshape(D_MODEL, 3, N_KV, GQA).transpose(0, 2, 1, 3)
    gates = gates.reshape(D_MODEL, N_KV, 3 * GQA)
    gates = jnp.pad(gates, ((0, 0), (0, 0), (0, LANES - 3 * GQA))).reshape(D_MODEL, N_KV * LANES)
    rest = w_in[:, o + 3 * N_HEADS:]
    return jnp.concatenate([w_in[:, :QW], k_c, v_c, _pad_heads(k_s), v_s, _pad_heads(k_w), v_w,
                            gates, rest], axis=1).astype(BF16)


def _pack_cmp(w_cmp_k, w_cmp_v, cmp_pos_k, cmp_pos_v):
    wk = w_cmp_k.reshape(CMP_BLOCK, HEAD_DIM, HEAD_DIM)
    wv = w_cmp_v.reshape(CMP_BLOCK, HEAD_DIM, HEAD_DIM)
    slabs = [jnp.pad(blk.astype(BF16), ((0, 0), (0, 0), (col, CMP_OUT - col - HEAD_DIM)))
             for blk, col in ((wk, 0), (wk, LANES), (wv, 2 * LANES), (wv, 2 * LANES + HEAD_DIM))]
    pos4 = jnp.concatenate([cmp_pos_k, cmp_pos_k, cmp_pos_v, cmp_pos_v], axis=1)
    return jnp.concatenate(slabs, axis=1), pos4


def _pad_gain(g):
    return jnp.pad(g, ((0, 0), (0, LANES - HEAD_DIM)))


def _overlap(S):
    ncp = S // CMP_STRIDE
    ns = S // SLC_BLOCK
    c_start = np.arange(ncp) * CMP_STRIDE
    s_start = np.arange(ns) * SLC_BLOCK
    ov = np.clip(np.minimum(c_start[None, :] + CMP_BLOCK, s_start[:, None] + SLC_BLOCK)
                 - np.maximum(c_start[None, :], s_start[:, None]), 0, None)
    return jnp.asarray(ov.astype(np.float32) / CMP_BLOCK, dtype=BF16)


def _block_onehot(S):
    pos = np.arange(S)
    oh = np.zeros((S, LANES), np.float32)
    oh[pos, pos // SLC_BLOCK] = 1.0
    return jnp.asarray(oh, dtype=BF16)


def _cmp_band(S):
    ncp = S // CMP_STRIDE
    c_rel = np.arange(2 * ncp)[:, None] - ncp
    visible = c_rel * CMP_STRIDE + (CMP_BLOCK - 1) <= np.arange(QT)[None, :]
    return jnp.asarray(np.where(visible, 0.0, NEG_INF), dtype=F32)


def _tiled_eye():
    return jnp.asarray(np.tile(np.eye(QT, dtype=np.float32), (1, GQA)), dtype=BF16)


def _layer(x, norm1_g, w_in, q_norm_g, k_norm_g, cmp_pos_k, cmp_pos_v, w_cmp_k, w_cmp_v,
           conv_w, w_branch_a, w_branch_b, w_out, norm2_g, w_up, w_down):
    B, S, _ = x.shape
    assert S % (CMP_STRIDE * LANES) == 0 and S // SLC_BLOCK <= LANES
    assert (S // TQ) % UNROLL == 0 and S >= WINDOW
    x2 = x.reshape(B * S, D_MODEL)
    w_bd, pos4 = _pack_cmp(w_cmp_k, w_cmp_v, cmp_pos_k, cmp_pos_v)
    kg = _pad_gain(k_norm_g)
    qt, kci, vci, ks, vst, kw, vwt, gnt, sga, ob = _proj_call(
        x2, norm1_g[None, :], _pack_w_in(w_in), w_branch_b.astype(BF16),
        q_norm_g[:, None], kg, conv_w, B, S, tm=512)
    kc, vct = _cmp_call(kci, vci, pos4, w_bd, kg, B, S)
    o_nsa = _attn_call(qt, kc, vct, ks, vst, kw, vwt, gnt, _overlap(S), _block_onehot(S),
                       _cmp_band(S), _tiled_eye(), B, S)
    out = _mlp_call(x2, o_nsa, sga, ob, w_branch_a.astype(BF16), w_out.astype(BF16),
                    norm2_g[None, :], w_up.astype(BF16), w_down.astype(BF16), tm=512)
    return out.reshape(B, S, D_MODEL)


@jax.jit
def kernel(x, norm1_g, w_in, q_norm_g, k_norm_g, cmp_pos_k, cmp_pos_v, w_cmp_k, w_cmp_v,
           conv_w, w_branch_a, w_branch_b, w_out, norm2_g, w_up, w_down):
    for l in range(norm1_g.shape[0]):
        x = _layer(x, norm1_g[l], w_in[l], q_norm_g[l], k_norm_g[l], cmp_pos_k[l],
                   cmp_pos_v[l], w_cmp_k[l], w_cmp_v[l], conv_w[l], w_branch_a[l],
                   w_branch_b[l], w_out[l], norm2_g[l], w_up[l], w_down[l])
    return x
```

```python
import numpy as np
import jax
import jax.numpy as jnp
from jax import lax
from jax.experimental import pallas as pl
from jax.experimental.pallas import tpu as pltpu

D_MODEL = 1024
N_HEADS = 8
HEAD_DIM = 64
N_KV = 2
GQA = N_HEADS // N_KV
CMP_BLOCK = 32
CMP_STRIDE = 16
SLC_BLOCK = 64
N_SLC = 16
WINDOW = 512
FORCE_SCORE = 1e4
SCALE = 0.125
CONV_K = 3
D_FF = 4 * D_MODEL
EPS = 1e-6
NEG_INF = -1e30

LANES = 128
QW = N_HEADS * HEAD_DIM
KVW = N_KV * HEAD_DIM

OFF_Q = 0
OFF_KC = OFF_Q + QW
OFF_VC = OFF_KC + KVW
OFF_KS = OFF_VC + KVW
OFF_VS = OFF_KS + KVW
OFF_KW = OFF_VS + KVW
OFF_VW = OFF_KW + KVW
OFF_GN = OFF_VW + KVW
PA_COLS = OFF_GN + LANES
OFF_CONV_B = PA_COLS
OFF_CONV_C = OFF_CONV_B + D_MODEL
OFF_CONV_X = OFF_CONV_C + D_MODEL
OFF_GATE_A = OFF_CONV_X + D_MODEL
OFF_GATE_B = OFF_GATE_A + D_MODEL
W_IN_COLS = OFF_GATE_B + D_MODEL

PROJ_SUB = 256
TQ = 128
QT = 256
QT_SHIFT = QT.bit_length() - 1
TILES_Q = QT // TQ
NL = GQA * QT
UNROLL = 8
LOOKAHEAD = 4
GROWS = 16
CMP_OUT = 2 * LANES + KVW
VMEM_LIMIT = 56 * 1024 * 1024

F32 = jnp.float32
BF16 = jnp.bfloat16


def _rms(x, g):
    return x * lax.rsqrt(jnp.mean(x * x, axis=-1, keepdims=True) + EPS) * g


def _rms_padded(x, g):
    ms = jnp.sum(x * x, axis=-1, keepdims=True) * (1.0 / HEAD_DIM)
    return x * lax.rsqrt(ms + EPS) * g


def _alibi_key_lanes(pos):
    lane = lax.broadcasted_iota(jnp.int32, (pos.shape[0], LANES), 1)
    hi = (pos >> 6).astype(F32)
    lo = (pos & 63).astype(F32)
    return jnp.where((lane == HEAD_DIM) | (lane == HEAD_DIM + 1) | (lane == HEAD_DIM + 4), 1.0,
                     jnp.where(lane == HEAD_DIM + 2, hi,
                               jnp.where(lane == HEAD_DIM + 3, lo, 0.0)))


def _const_spec(shape):
    zeros = (0,) * len(shape)
    return pl.BlockSpec(shape, lambda *_: zeros, pipeline_mode=pl.Buffered(1))


def _proj_kernel(x_ref, g1_ref, w_ref, wb_ref, qg_ref, kg_ref, cw_ref,
                 qt_ref, kci_ref, vci_ref, ks_ref, vst_ref, kw_ref, vwt_ref, gnt_ref,
                 sga_ref, ob_ref, u_scr):
    i = pl.program_id(1)
    tm = x_ref.shape[0]

    @pl.when(i == 0)
    def _():
        u_scr[0:8, :] = jnp.zeros((8, D_MODEL), F32)

    for r0 in range(0, tm, PROJ_SUB):
        _proj_rows(r0, i * tm + r0, x_ref, g1_ref, w_ref, wb_ref, qg_ref, kg_ref, cw_ref,
                   qt_ref, kci_ref, vci_ref, ks_ref, vst_ref, kw_ref, vwt_ref, gnt_ref,
                   sga_ref, ob_ref, u_scr)
    u_scr[0:8, :] = u_scr[tm:tm + 8, :]


def _proj_rows(r0, pos0, x_ref, g1_ref, w_ref, wb_ref, qg_ref, kg_ref, cw_ref,
               qt_ref, kci_ref, vci_ref, ks_ref, vst_ref, kw_ref, vwt_ref, gnt_ref,
               sga_ref, ob_ref, u_scr):
    n = PROJ_SUB
    rows = slice(r0, r0 + n)
    xn = _rms(x_ref[rows, :], g1_ref[...]).astype(BF16)

    pa = jnp.dot(xn, w_ref[:, 0:PA_COLS], preferred_element_type=F32)

    qg = qg_ref[...]
    for pair in range(N_HEADS // 2):
        qp = pa[:, OFF_Q + pair * LANES:OFF_Q + (pair + 1) * LANES].T
        for half in range(2):
            qh = qp[half * HEAD_DIM:(half + 1) * HEAD_DIM, :]
            ms = jnp.mean(qh * qh, axis=0, keepdims=True)
            qt_ref[2 * pair + half, :, rows] = (qh * lax.rsqrt(ms + EPS) * qg * SCALE).astype(BF16)

    kci_ref[rows, :] = pa[:, OFF_KC:OFF_KC + KVW]
    vci_ref[rows, :] = pa[:, OFF_VC:OFF_VC + KVW]

    pos = pos0 + lax.broadcasted_iota(jnp.int32, (n, 1), 0)
    key_lanes = _alibi_key_lanes(pos)
    low = lax.broadcasted_iota(jnp.int32, (n, LANES), 1) < HEAD_DIM
    ks2 = pa[:, OFF_KS:OFF_KS + KVW]
    kw2 = pa[:, OFF_KW:OFF_KW + KVW]
    vs_t = pa[:, OFF_VS:OFF_VS + KVW].T.astype(BF16)
    vw_t = pa[:, OFF_VW:OFF_VW + KVW].T.astype(BF16)
    gates_t = jax.nn.sigmoid(pa[:, OFF_GN:OFF_GN + LANES]).T
    for g in range(N_KV):
        ks = jnp.where(low, ks2 if g == 0 else pltpu.roll(ks2, HEAD_DIM, axis=1), 0.0)
        kw = jnp.where(low, kw2 if g == 0 else pltpu.roll(kw2, HEAD_DIM, axis=1), 0.0)
        ks_ref[g, rows, :] = (_rms_padded(ks, kg_ref[1:2, :]) + key_lanes).astype(BF16)
        kw_ref[g, rows, :] = (_rms_padded(kw, kg_ref[2:3, :]) + key_lanes).astype(BF16)
        for c in range(n // TQ):
            ct = r0 // TQ + c
            vst_ref[g, ct] = vs_t[g * HEAD_DIM:(g + 1) * HEAD_DIM, c * TQ:(c + 1) * TQ]
            vwt_ref[g, ct] = vw_t[g * HEAD_DIM:(g + 1) * HEAD_DIM, c * TQ:(c + 1) * TQ]
        gnt_ref[g, :, rows] = gates_t[g * HEAD_DIM:g * HEAD_DIM + GROWS, :]

    def col(o):
        return jnp.dot(xn, w_ref[:, o:o + D_MODEL], preferred_element_type=F32)

    u = col(OFF_CONV_C) * col(OFF_CONV_X)
    u_scr[8 + r0:8 + r0 + n, :] = u
    cw = cw_ref[...]
    y = (cw[2:3, :] * u + cw[1:2, :] * u_scr[7 + r0:7 + r0 + n, :]
         + cw[0:1, :] * u_scr[6 + r0:6 + r0 + n, :])
    z = col(OFF_CONV_B) * y
    zb = jnp.dot(z.astype(BF16), wb_ref[...], preferred_element_type=F32)
    ob_ref[rows, :] = jax.nn.sigmoid(col(OFF_GATE_B)) * zb
    sga_ref[rows, :] = jax.nn.sigmoid(col(OFF_GATE_A))


def _proj_call(x2, g1, w_in_p, w_b, qg, kg, cw, B, S, tm):
    T = B * S
    nt = S // tm
    row = lambda b, i: (b * nt + i, 0)
    k_shape = jax.ShapeDtypeStruct((B, N_KV, S, LANES), BF16)
    k_spec = pl.BlockSpec((None, N_KV, tm, LANES), lambda b, i: (b, 0, i, 0))
    vt_shape = jax.ShapeDtypeStruct((B, N_KV, S // TQ, HEAD_DIM, TQ), BF16)
    vt_spec = pl.BlockSpec((None, N_KV, tm // TQ, HEAD_DIM, TQ), lambda b, i: (b, 0, i, 0, 0))
    return pl.pallas_call(
        _proj_kernel,
        grid=(B, nt),
        in_specs=[
            pl.BlockSpec((tm, D_MODEL), row),
            _const_spec((1, D_MODEL)),
            _const_spec((D_MODEL, W_IN_COLS)),
            _const_spec((D_MODEL, D_MODEL)),
            _const_spec((HEAD_DIM, 1)),
            _const_spec((3, LANES)),
            _const_spec((CONV_K, D_MODEL)),
        ],
        out_specs=[
            pl.BlockSpec((None, N_HEADS, HEAD_DIM, tm), lambda b, i: (b, 0, 0, i)),
            pl.BlockSpec((tm, KVW), row),
            pl.BlockSpec((tm, KVW), row),
            k_spec, vt_spec, k_spec, vt_spec,
            pl.BlockSpec((None, N_KV, GROWS, tm), lambda b, i: (b, 0, 0, i)),
            pl.BlockSpec((tm, D_MODEL), row),
            pl.BlockSpec((tm, D_MODEL), row),
        ],
        out_shape=[
            jax.ShapeDtypeStruct((B, N_HEADS, HEAD_DIM, S), BF16),
            jax.ShapeDtypeStruct((T, KVW), F32),
            jax.ShapeDtypeStruct((T, KVW), F32),
            k_shape, vt_shape, k_shape, vt_shape,
            jax.ShapeDtypeStruct((B, N_KV, GROWS, S), F32),
            jax.ShapeDtypeStruct((T, D_MODEL), F32),
            jax.ShapeDtypeStruct((T, D_MODEL), F32),
        ],
        scratch_shapes=[pltpu.VMEM((tm + 8, D_MODEL), F32)],
        compiler_params=pltpu.CompilerParams(
            dimension_semantics=("arbitrary", "arbitrary"),
            vmem_limit_bytes=VMEM_LIMIT),
        name="proj",
    )(x2, g1, w_in_p, w_b, qg, kg, cw)


def _cmp_kernel(kci_ref, vci_ref, pos_ref, w_ref, kg_ref, kc_ref, vct_ref, b_scr):
    ncp = kc_ref.shape[1]
    acc_a = jnp.zeros((ncp, CMP_OUT), F32)
    acc_b = jnp.zeros((ncp, CMP_OUT), F32)
    for l in range(CMP_STRIDE):
        rows = jnp.concatenate([kci_ref[pl.ds(l, ncp, stride=CMP_STRIDE), :],
                                vci_ref[pl.ds(l, ncp, stride=CMP_STRIDE), :]], axis=1)
        xa = (rows + pos_ref[l:l + 1, :]).astype(BF16)
        xb = (rows + pos_ref[CMP_STRIDE + l:CMP_STRIDE + l + 1, :]).astype(BF16)
        acc_a = acc_a + jnp.dot(xa, w_ref[l], preferred_element_type=F32)
        acc_b = acc_b + jnp.dot(xb, w_ref[CMP_STRIDE + l], preferred_element_type=F32)
    b_scr[0:ncp, :] = acc_b
    b_scr[ncp:ncp + 8, :] = jnp.zeros((8, CMP_OUT), F32)
    kcv = acc_a + b_scr[1:ncp + 1, :]
    c_end = lax.broadcasted_iota(jnp.int32, (ncp, 1), 0) * CMP_STRIDE + (CMP_BLOCK - 1)
    key_lanes = _alibi_key_lanes(c_end)
    vc_t = kcv[:, 2 * LANES:CMP_OUT].T.astype(BF16)
    for g in range(N_KV):
        kc = _rms_padded(kcv[:, g * LANES:(g + 1) * LANES], kg_ref[0:1, :])
        kc_ref[g] = (kc + key_lanes).astype(BF16)
        vct_ref[g] = vc_t[g * HEAD_DIM:(g + 1) * HEAD_DIM, :]


def _cmp_call(kci, vci, pos4, w_bd, kg, B, S):
    ncp = S // CMP_STRIDE
    return pl.pallas_call(
        _cmp_kernel,
        grid=(B,),
        in_specs=[
            pl.BlockSpec((S, KVW), lambda b: (b, 0)),
            pl.BlockSpec((S, KVW), lambda b: (b, 0)),
            _const_spec((CMP_BLOCK, 2 * KVW)),
            _const_spec((CMP_BLOCK, 2 * KVW, CMP_OUT)),
            _const_spec((3, LANES)),
        ],
        out_specs=[
            pl.BlockSpec((None, N_KV, ncp, LANES), lambda b: (b, 0, 0, 0)),
            pl.BlockSpec((None, N_KV, HEAD_DIM, ncp), lambda b: (b, 0, 0, 0)),
        ],
        out_shape=[
            jax.ShapeDtypeStruct((B, N_KV, ncp, LANES), BF16),
            jax.ShapeDtypeStruct((B, N_KV, HEAD_DIM, ncp), BF16),
        ],
        scratch_shapes=[pltpu.VMEM((ncp + 8, CMP_OUT), F32)],
        compiler_params=pltpu.CompilerParams(
            dimension_semantics=("arbitrary",), vmem_limit_bytes=VMEM_LIMIT),
        name="compress",
    )(kci, vci, pos4, w_bd, kg)


def _attn_kernel(qt_ref, kc_ref, vct_ref, ks_ref, vst_ref, kw_ref, vwt_ref, gnt_ref,
                 ovl_ref, oh_ref, cband_ref, eye_ref, o_ref, score_scr, cnt_scr,
                 num_s, den_s, num_w, den_w):
    acc_s = (num_s, den_s)
    acc_w = (num_w, den_w)
    g = pl.program_id(1)
    i = pl.program_id(2)
    t0 = i * QT
    kt_last = TILES_Q * (i + 1) - 1
    ns = ovl_ref.shape[0]
    ncp = ovl_ref.shape[1]

    lane = lax.broadcasted_iota(jnp.int32, (1, NL), 1)
    tq = t0 + (lane & (QT - 1))
    head = g * GQA + (lane >> QT_SHIFT) + 1
    slope = lax.bitcast_convert_type((127 - head) << 23, F32)
    a_t = (tq >> 6).astype(F32)
    b_t = (tq & 63).astype(F32)
    sub = lax.broadcasted_iota(jnp.int32, (HEAD_DIM, NL), 0)
    pos_col = lax.broadcasted_iota(jnp.int32, (TQ, 1), 0)
    qt = jnp.concatenate([qt_ref[r] for r in range(GQA)], axis=1)

    def aug_query(ref):
        rows = jnp.where(sub == 0, -(slope * 64.0) * a_t,
                         jnp.where(sub == 1, -slope * b_t,
                                   jnp.where(sub == 2, slope * 64.0,
                                             jnp.where(sub == 3, slope,
                                                       jnp.where(sub == 4, -ref, 0.0)))))
        return jnp.concatenate([qt, rows.astype(BF16)], axis=0)

    def tile(k_ref, kt):
        return k_ref[pl.ds(pl.multiple_of(kt * TQ, TQ), TQ), :]

    def causal(kt):
        return tq >= (kt * TQ + pos_col)

    def self_score(k_ref):
        k_t = jnp.concatenate([tile(k_ref, TILES_Q * i + c).astype(F32).T[0:HEAD_DIM, :]
                               for c in range(TILES_Q)], axis=1)
        return jnp.sum(qt.astype(F32) * jnp.concatenate([k_t] * GQA, axis=1),
                       axis=0, keepdims=True)

    def fold8(p):
        out = p[0:8, :]
        for k in range(1, TQ // 8):
            out = out + p[8 * k:8 * k + 8, :]
        return out

    def plain_tiles(score_fn, vt_ref, state, kts, masks, start, hook=None):
        acc_ref, den_ref = state
        ahead = [score_fn(kt) for kt in kts[:LOOKAHEAD]]
        if hook is not None:
            hook()
        total = None if start else acc_ref[...]
        den = None if start else den_ref[...]
        for n, (kt, mk) in enumerate(zip(kts, masks)):
            sc = ahead.pop(0)
            if n + LOOKAHEAD < len(kts):
                ahead.append(score_fn(kts[n + LOOKAHEAD]))
            if mk is not None:
                sc = jnp.where(mk, sc, NEG_INF)
            pr = jnp.exp(sc)
            pv = jnp.dot(vt_ref[kt], pr.astype(BF16), preferred_element_type=F32)
            total = pv if total is None else total + pv
            den = fold8(pr) if den is None else den + fold8(pr)
        acc_ref[...] = total
        den_ref[...] = den

    def online_tile(score_fn, vt_ref, state, kt, mask, m_run):
        acc_ref, den_ref = state
        sc = jnp.where(mask, score_fn(kt), NEG_INF)
        mx = jnp.max(sc, axis=0, keepdims=True)
        m_new = mx if m_run is None else jnp.maximum(m_run, mx)
        pr = jnp.exp(sc - m_new)
        pv = jnp.dot(vt_ref[kt], pr.astype(BF16), preferred_element_type=F32)
        if m_run is None:
            acc_ref[...] = pv
            den_ref[...] = fold8(pr)
        else:
            alpha = jnp.exp(m_run - m_new)
            acc_ref[...] = alpha * acc_ref[...] + pv
            den_ref[...] = alpha * den_ref[...] + fold8(pr)
        return m_new

    def finish(state):
        acc_ref, den_ref = state
        den = jnp.sum(den_ref[...], axis=0, keepdims=True)
        return acc_ref[...] / jnp.maximum(den, 1e-30)

    band = cband_ref[pl.ds(pl.multiple_of(ncp - (QT // CMP_STRIDE) * i, 8), ncp), :]
    lhs_c = jnp.concatenate([kc_ref[...], band.astype(BF16)], axis=1)
    rhs_c = jnp.concatenate([aug_query(0.0), eye_ref[...]], axis=0)
    has_cmp = tq >= CMP_BLOCK - 1
    qa_win = aug_query(self_score(kw_ref))

    def win_scores(kt):
        return jnp.dot(tile(kw_ref, kt), qa_win, preferred_element_type=F32)

    def cmp_and_window(win_kts, win_masks):
        s = jnp.dot(lhs_c, rhs_c, preferred_element_type=F32)
        out = []

        def cmp_rest():
            e = jnp.exp(s - jnp.max(s, axis=0, keepdims=True))
            den = jnp.maximum(jnp.sum(e, axis=0, keepdims=True), 1e-30)
            p = (e * jnp.where(has_cmp, 1.0 / den, 0.0)).astype(BF16)
            out.append(jnp.dot(vct_ref[...], p, preferred_element_type=F32))
            out.append(jnp.dot(ovl_ref[...], p, preferred_element_type=F32))

        plain_tiles(win_scores, vwt_ref, acc_w, win_kts, win_masks, True, hook=cmp_rest)
        return out[0], out[1]

    n_win = WINDOW // TQ
    n_wt = n_win + TILES_Q

    def win_steady():
        kts = [kt_last - d for d in range(n_wt)]
        masks = [causal(kt) for kt in kts[:TILES_Q]] + [None] * (n_win - TILES_Q)
        masks += [(tq - (kt * TQ + pos_col)) < WINDOW for kt in kts[n_win:]]
        return cmp_and_window(kts, masks)

    def win_start():
        kts = list(range(n_wt - 1))
        return cmp_and_window(kts, [causal(kt) for kt in kts])

    o_cmp, imp4 = lax.cond(kt_last >= n_wt - 1, win_steady, win_start)
    o_win = finish(acc_w)

    imp = imp4[:, 0:QT]
    for r in range(1, GQA):
        imp = imp + imp4[:, r * QT:(r + 1) * QT]
    j = lax.broadcasted_iota(jnp.int32, (ns, QT), 0)
    cur = (t0 + lax.broadcasted_iota(jnp.int32, (ns, QT), 1)) >> 6
    forced = (j == 0) | (j == cur) | (j == cur - 1)
    score = jnp.where(forced, FORCE_SCORE, jnp.where(j <= cur, imp, -1.0))
    score_scr[...] = score
    cnt_scr[...] = jnp.zeros((ns, QT), jnp.int32)

    row8 = lax.broadcasted_iota(jnp.int32, (8, QT), 0)
    for kg in range(ns // 8):
        @pl.when(8 * kg <= (TQ // SLC_BLOCK) * kt_last + 1)
        def _(kg=kg):
            cnt = [cnt_scr[8 * v:8 * v + 8, :] for v in range(ns // 8)]
            for b in range(8 * kg, 8 * kg + 8):
                sb = score_scr[b:b + 1, :]
                for v in range(ns // 8):
                    sc_v = score[8 * v:8 * v + 8, :]
                    if v < kg:
                        ahead = sb > sc_v
                    elif v > kg:
                        ahead = sb >= sc_v
                    else:
                        ahead = (sb > sc_v) | ((sb == sc_v) & (row8 > b - 8 * kg))
                    cnt[v] = cnt[v] + jnp.where(ahead, 1, 0)
            for v in range(ns // 8):
                cnt_scr[8 * v:8 * v + 8, :] = cnt[v]

    sel = (cnt_scr[...] < N_SLC) & (j <= cur)
    selb = jnp.where(sel, 0.0, NEG_INF).astype(BF16)
    selb = jnp.concatenate([selb, jnp.zeros((LANES - ns, QT), BF16)], axis=0)

    sel_rows = jnp.concatenate([selb] * GQA, axis=1)
    qa_sel = jnp.concatenate([aug_query(self_score(ks_ref)), sel_rows], axis=0)

    def sel_scores(kt):
        lhs = jnp.concatenate([tile(ks_ref, kt), tile(oh_ref, kt)], axis=1)
        return jnp.dot(lhs, qa_sel, preferred_element_type=F32)

    def sel_group(base, n, masked, start):
        kts = [base + u for u in range(n)]
        masks = [causal(kt) if masked else None for kt in kts]
        plain_tiles(sel_scores, vst_ref, acc_s, kts, masks, start)

    base = (kt_last // UNROLL) * UNROLL
    for v in range(UNROLL // TILES_Q):
        @pl.when((kt_last % UNROLL) // TILES_Q == v)
        def _(v=v):
            kts = [base + u for u in range(TILES_Q * (v + 1))]
            masks = [None] * (TILES_Q * v) + [causal(kt) for kt in kts[TILES_Q * v:]]
            plain_tiles(sel_scores, vst_ref, acc_s, kts, masks, True)

    def sel_body(grp, carry):
        sel_group(grp * UNROLL, UNROLL, False, False)
        return carry

    lax.fori_loop(0, kt_last // UNROLL, sel_body, 0)
    o_slc = finish(acc_s)

    def gate(branch):
        return jnp.concatenate(
            [gnt_ref[branch * GQA + r:branch * GQA + r + 1, :] for r in range(GQA)], axis=1)

    def emit(o_slc, o_win):
        o = gate(0) * o_cmp + gate(1) * o_slc + gate(2) * o_win
        for h in range(GQA // 2):
            pair = jnp.concatenate([o[:, (2 * h) * QT:(2 * h + 1) * QT],
                                    o[:, (2 * h + 1) * QT:(2 * h + 2) * QT]], axis=0)
            for c in range(TILES_Q):
                o_ref[c * TQ:(c + 1) * TQ, h * LANES:(h + 1) * LANES] = (
                    pair[:, c * TQ:(c + 1) * TQ].T.astype(BF16))

    emit(o_slc, o_win)

    bad = jnp.max(jnp.where(jnp.isfinite(o_slc) & jnp.isfinite(o_win), 0.0, 1.0))

    @pl.when(bad > 0.0)
    def _():
        m_run = online_tile(sel_scores, vst_ref, acc_s, kt_last, causal(kt_last), None)
        lax.fori_loop(0, kt_last, lambda kt, m_in: online_tile(sel_scores, vst_ref, acc_s, kt,
                                                               causal(kt), m_in), m_run)
        m_run = None
        for d in list(range(TILES_Q - 1, n_wt)) + list(range(TILES_Q - 1)):
            pos = (kt_last - d) * TQ + pos_col
            inside = (tq >= pos) & (tq - pos < WINDOW) & (pos >= 0)
            m_run = online_tile(win_scores, vwt_ref, acc_w, jnp.maximum(kt_last - d, 0), inside,
                                m_run)
        emit(finish(acc_s), finish(acc_w))


def _attn_call(qt, kc, vct, ks, vst, kw, vwt, gnt, ovl, onehot, cband, eye, B, S):
    T = B * S
    nq = S // QT
    ncp = S // CMP_STRIDE
    ns = S // SLC_BLOCK
    bg4 = lambda b, g, i: (b, g, 0, 0)
    bg5 = lambda b, g, i: (b, g, 0, 0, 0)
    k_spec = pl.BlockSpec((None, None, S, LANES), bg4)
    vt_spec = pl.BlockSpec((None, None, S // TQ, HEAD_DIM, TQ), bg5)
    return pl.pallas_call(
        _attn_kernel,
        grid=(B, N_KV, nq),
        in_specs=[
            pl.BlockSpec((None, GQA, HEAD_DIM, QT), lambda b, g, i: (b, g, 0, i)),
            pl.BlockSpec((None, None, ncp, LANES), bg4),
            pl.BlockSpec((None, None, HEAD_DIM, ncp), bg4),
            k_spec, vt_spec, k_spec, vt_spec,
            pl.BlockSpec((None, None, GROWS, QT), lambda b, g, i: (b, g, 0, i)),
            _const_spec((ns, ncp)),
            _const_spec((S, LANES)),
            _const_spec((2 * ncp, QT)),
            _const_spec((QT, NL)),
        ],
        out_specs=pl.BlockSpec((QT, GQA * HEAD_DIM), lambda b, g, i: (b * nq + i, g)),
        out_shape=jax.ShapeDtypeStruct((T, QW), BF16),
        scratch_shapes=[pltpu.VMEM((ns, QT), F32), pltpu.VMEM((ns, QT), jnp.int32),
                        pltpu.VMEM((HEAD_DIM, NL), F32), pltpu.VMEM((8, NL), F32),
                        pltpu.VMEM((HEAD_DIM, NL), F32), pltpu.VMEM((8, NL), F32)],
        compiler_params=pltpu.CompilerParams(
            dimension_semantics=("arbitrary", "arbitrary", "arbitrary"),
            vmem_limit_bytes=VMEM_LIMIT),
        name="attn",
    )(qt, kc, vct, ks, vst, kw, vwt, gnt, ovl, onehot, cband, eye)


FF_CHUNK = 512


def _mlp_kernel(x_ref, o_ref, sga_ref, ob_ref, wa_ref, wo_ref, g2_ref, wu_ref, wd_ref,
                out_ref, acc_scr):
    a = jnp.dot(o_ref[...], wa_ref[...], preferred_element_type=F32)
    mixed = sga_ref[...] * a + ob_ref[...]
    x1 = x_ref[...] + jnp.dot(mixed.astype(BF16), wo_ref[...], preferred_element_type=F32)
    h = _rms(x1, g2_ref[...]).astype(BF16)
    acc_scr[...] = x1
    for c in range(D_FF // FF_CHUNK):
        lo, hi = c * FF_CHUNK, (c + 1) * FF_CHUNK
        up = jnp.dot(h, wu_ref[:, lo:hi], preferred_element_type=F32)
        act = jnp.square(jnp.maximum(up, 0.0)).astype(BF16)
        acc_scr[...] += jnp.dot(act, wd_ref[lo:hi, :], preferred_element_type=F32)
    out_ref[...] = acc_scr[...]


def _mlp_call(x2, o_nsa, sga, ob, w_a, w_o, g2, w_up, w_down, tm):
    T = x2.shape[0]
    row = lambda i: (i, 0)
    return pl.pallas_call(
        _mlp_kernel,
        grid=(T // tm,),
        in_specs=[
            pl.BlockSpec((tm, D_MODEL), row),
            pl.BlockSpec((tm, QW), row),
            pl.BlockSpec((tm, D_MODEL), row),
            pl.BlockSpec((tm, D_MODEL), row),
            _const_spec((QW, D_MODEL)),
            _const_spec((D_MODEL, D_MODEL)),
            _const_spec((1, D_MODEL)),
            _const_spec((D_MODEL, D_FF)),
            _const_spec((D_FF, D_MODEL)),
        ],
        out_specs=pl.BlockSpec((tm, D_MODEL), row),
        out_shape=jax.ShapeDtypeStruct((T, D_MODEL), F32),
        scratch_shapes=[pltpu.VMEM((tm, D_MODEL), F32)],
        compiler_params=pltpu.CompilerParams(
            dimension_semantics=("arbitrary",), vmem_limit_bytes=VMEM_LIMIT),
        name="mlp",
    )(x2, o_nsa, sga, ob, w_a, w_o, g2, w_up, w_down)


def _pack_w_in(w_in):
    w_in = w_in.astype(BF16)
    o = OFF_GN
    gates = w_in[:, o:o + 3 * N_HEADS]
    gates = gates.reshape(D_MODEL, 3, N_KV, GQA).transpose(0, 2, 1, 3)
    gates = gates.reshape(D_MODEL, N_KV, 3 * GQA)
    gates = jnp.pad(gates, ((0, 0), (0, 0), (0, HEAD_DIM - 3 * GQA))).reshape(D_MODEL, LANES)
    return jnp.concatenate([w_in[:, :o], gates, w_in[:, o + 3 * N_HEADS:]], axis=1)


def _pack_cmp(w_cmp_k, w_cmp_v, cmp_pos_k, cmp_pos_v):
    wk = w_cmp_k.reshape(CMP_BLOCK, HEAD_DIM, HEAD_DIM)
    wv = w_cmp_v.reshape(CMP_BLOCK, HEAD_DIM, HEAD_DIM)
    slabs = [jnp.pad(blk.astype(BF16), ((0, 0), (0, 0), (col, CMP_OUT - col - HEAD_DIM)))
             for blk, col in ((wk, 0), (wk, LANES), (wv, 2 * LANES), (wv, 2 * LANES + HEAD_DIM))]
    pos4 = jnp.concatenate([cmp_pos_k, cmp_pos_k, cmp_pos_v, cmp_pos_v], axis=1)
    return jnp.concatenate(slabs, axis=1), pos4


def _pad_gain(g):
    return jnp.pad(g, ((0, 0), (0, LANES - HEAD_DIM)))


def _overlap(S):
    ncp = S // CMP_STRIDE
    ns = S // SLC_BLOCK
    c_start = np.arange(ncp) * CMP_STRIDE
    s_start = np.arange(ns) * SLC_BLOCK
    ov = np.clip(np.minimum(c_start[None, :] + CMP_BLOCK, s_start[:, None] + SLC_BLOCK)
                 - np.maximum(c_start[None, :], s_start[:, None]), 0, None)
    return jnp.asarray(ov.astype(np.float32) / CMP_BLOCK, dtype=BF16)


def _block_onehot(S):
    pos = np.arange(S)
    oh = np.zeros((S, LANES), np.float32)
    oh[pos, pos // SLC_BLOCK] = 1.0
    return jnp.asarray(oh, dtype=BF16)


def _cmp_band(S):
    ncp = S // CMP_STRIDE
    c_rel = np.arange(2 * ncp)[:, None] - ncp
    visible = c_rel * CMP_STRIDE + (CMP_BLOCK - 1) <= np.arange(QT)[None, :]
    return jnp.asarray(np.where(visible, 0.0, NEG_INF), dtype=F32)


def _tiled_eye():
    return jnp.asarray(np.tile(np.eye(QT, dtype=np.float32), (1, GQA)), dtype=BF16)


def _layer(x, norm1_g, w_in, q_norm_g, k_norm_g, cmp_pos_k, cmp_pos_v, w_cmp_k, w_cmp_v,
           conv_w, w_branch_a, w_branch_b, w_out, norm2_g, w_up, w_down):
    B, S, _ = x.shape
    assert S % (CMP_STRIDE * LANES) == 0 and S // SLC_BLOCK <= LANES
    assert (S // TQ) % UNROLL == 0 and S >= WINDOW
    x2 = x.reshape(B * S, D_MODEL)
    w_bd, pos4 = _pack_cmp(w_cmp_k, w_cmp_v, cmp_pos_k, cmp_pos_v)
    kg = _pad_gain(k_norm_g)
    qt, kci, vci, ks, vst, kw, vwt, gnt, sga, ob = _proj_call(
        x2, norm1_g[None, :], _pack_w_in(w_in), w_branch_b.astype(BF16),
        q_norm_g[:, None], kg, conv_w, B, S, tm=512)
    kc, vct = _cmp_call(kci, vci, pos4, w_bd, kg, B, S)
    o_nsa = _attn_call(qt, kc, vct, ks, vst, kw, vwt, gnt, _overlap(S), _block_onehot(S),
                       _cmp_band(S), _tiled_eye(), B, S)
    out = _mlp_call(x2, o_nsa, sga, ob, w_branch_a.astype(BF16), w_out.astype(BF16),
                    norm2_g[None, :], w_up.astype(BF16), w_down.astype(BF16), tm=512)
    return out.reshape(B, S, D_MODEL)


@jax.jit
def kernel(x, norm1_g, w_in, q_norm_g, k_norm_g, cmp_pos_k, cmp_pos_v, w_cmp_k, w_cmp_v,
           conv_w, w_branch_a, w_branch_b, w_out, norm2_g, w_up, w_down):
    for l in range(norm1_g.shape[0]):
        x = _layer(x, norm1_g[l], w_in[l], q_norm_g[l], k_norm_g[l], cmp_pos_k[l],
                   cmp_pos_v[l], w_cmp_k[l], w_cmp_v[l], conv_w[l], w_branch_a[l],
                   w_branch_b[l], w_out[l], norm2_g[l], w_up[l], w_down[l])
    return x
```

```python
import numpy as np
import jax
import jax.numpy as jnp
from jax import lax
from jax.experimental import pallas as pl
from jax.experimental.pallas import tpu as pltpu

D_MODEL = 1024
N_HEADS = 8
HEAD_DIM = 64
N_KV = 2
GQA = N_HEADS // N_KV
CMP_BLOCK = 32
CMP_STRIDE = 16
SLC_BLOCK = 64
N_SLC = 16
WINDOW = 512
FORCE_SCORE = 1e4
SCALE = 0.125
CONV_K = 3
D_FF = 4 * D_MODEL
EPS = 1e-6
NEG_INF = -1e30

LANES = 128
QW = N_HEADS * HEAD_DIM
KVW = N_KV * HEAD_DIM

OFF_Q = 0
OFF_KC = OFF_Q + QW
OFF_VC = OFF_KC + KVW
OFF_KS = OFF_VC + KVW
OFF_VS = OFF_KS + KVW
OFF_KW = OFF_VS + KVW
OFF_VW = OFF_KW + KVW
OFF_GN = OFF_VW + KVW
PA_COLS = OFF_GN + LANES
OFF_CONV_B = PA_COLS
OFF_CONV_C = OFF_CONV_B + D_MODEL
OFF_CONV_X = OFF_CONV_C + D_MODEL
OFF_GATE_A = OFF_CONV_X + D_MODEL
OFF_GATE_B = OFF_GATE_A + D_MODEL
W_IN_COLS = OFF_GATE_B + D_MODEL

PROJ_SUB = 256
TQ = 256
QT = 256
QT_SHIFT = QT.bit_length() - 1
TILES_Q = QT // TQ
NL = GQA * QT
UNROLL = 4
LOOKAHEAD = 1
GROWS = 16
CMP_OUT = 2 * LANES + KVW
VMEM_LIMIT = 56 * 1024 * 1024

F32 = jnp.float32
BF16 = jnp.bfloat16


def _rms(x, g):
    return x * lax.rsqrt(jnp.mean(x * x, axis=-1, keepdims=True) + EPS) * g


def _rms_padded(x, g):
    ms = jnp.sum(x * x, axis=-1, keepdims=True) * (1.0 / HEAD_DIM)
    return x * lax.rsqrt(ms + EPS) * g


def _alibi_key_lanes(pos):
    lane = lax.broadcasted_iota(jnp.int32, (pos.shape[0], LANES), 1)
    hi = (pos >> 6).astype(F32)
    lo = (pos & 63).astype(F32)
    return jnp.where((lane == HEAD_DIM) | (lane == HEAD_DIM + 1) | (lane == HEAD_DIM + 4), 1.0,
                     jnp.where(lane == HEAD_DIM + 2, hi,
                               jnp.where(lane == HEAD_DIM + 3, lo, 0.0)))


def _const_spec(shape):
    zeros = (0,) * len(shape)
    return pl.BlockSpec(shape, lambda *_: zeros, pipeline_mode=pl.Buffered(1))


def _proj_kernel(x_ref, g1_ref, w_ref, wb_ref, qg_ref, kg_ref, cw_ref,
                 qt_ref, kci_ref, vci_ref, ks_ref, vst_ref, kw_ref, vwt_ref, gnt_ref,
                 sga_ref, ob_ref, u_scr):
    i = pl.program_id(1)
    tm = x_ref.shape[0]

    @pl.when(i == 0)
    def _():
        u_scr[0:8, :] = jnp.zeros((8, D_MODEL), F32)

    for r0 in range(0, tm, PROJ_SUB):
        _proj_rows(r0, i * tm + r0, x_ref, g1_ref, w_ref, wb_ref, qg_ref, kg_ref, cw_ref,
                   qt_ref, kci_ref, vci_ref, ks_ref, vst_ref, kw_ref, vwt_ref, gnt_ref,
                   sga_ref, ob_ref, u_scr)
    u_scr[0:8, :] = u_scr[tm:tm + 8, :]


def _proj_rows(r0, pos0, x_ref, g1_ref, w_ref, wb_ref, qg_ref, kg_ref, cw_ref,
               qt_ref, kci_ref, vci_ref, ks_ref, vst_ref, kw_ref, vwt_ref, gnt_ref,
               sga_ref, ob_ref, u_scr):
    n = PROJ_SUB
    rows = slice(r0, r0 + n)
    xn = _rms(x_ref[rows, :], g1_ref[...]).astype(BF16)

    pa = jnp.dot(xn, w_ref[:, 0:PA_COLS], preferred_element_type=F32)

    qg = qg_ref[...]
    for pair in range(N_HEADS // 2):
        qp = pa[:, OFF_Q + pair * LANES:OFF_Q + (pair + 1) * LANES].T
        for half in range(2):
            qh = qp[half * HEAD_DIM:(half + 1) * HEAD_DIM, :]
            ms = jnp.mean(qh * qh, axis=0, keepdims=True)
            qt_ref[2 * pair + half, :, rows] = (qh * lax.rsqrt(ms + EPS) * qg * SCALE).astype(BF16)

    kci_ref[rows, :] = pa[:, OFF_KC:OFF_KC + KVW]
    vci_ref[rows, :] = pa[:, OFF_VC:OFF_VC + KVW]

    pos = pos0 + lax.broadcasted_iota(jnp.int32, (n, 1), 0)
    key_lanes = _alibi_key_lanes(pos)
    low = lax.broadcasted_iota(jnp.int32, (n, LANES), 1) < HEAD_DIM
    ks2 = pa[:, OFF_KS:OFF_KS + KVW]
    kw2 = pa[:, OFF_KW:OFF_KW + KVW]
    vs_t = pa[:, OFF_VS:OFF_VS + KVW].T.astype(BF16)
    vw_t = pa[:, OFF_VW:OFF_VW + KVW].T.astype(BF16)
    gates_t = jax.nn.sigmoid(pa[:, OFF_GN:OFF_GN + LANES]).T
    for g in range(N_KV):
        ks = jnp.where(low, ks2 if g == 0 else pltpu.roll(ks2, HEAD_DIM, axis=1), 0.0)
        kw = jnp.where(low, kw2 if g == 0 else pltpu.roll(kw2, HEAD_DIM, axis=1), 0.0)
        ks_ref[g, rows, :] = (_rms_padded(ks, kg_ref[1:2, :]) + key_lanes).astype(BF16)
        kw_ref[g, rows, :] = (_rms_padded(kw, kg_ref[2:3, :]) + key_lanes).astype(BF16)
        for c in range(n // TQ):
            ct = r0 // TQ + c
            vst_ref[g, ct] = vs_t[g * HEAD_DIM:(g + 1) * HEAD_DIM, c * TQ:(c + 1) * TQ]
            vwt_ref[g, ct] = vw_t[g * HEAD_DIM:(g + 1) * HEAD_DIM, c * TQ:(c + 1) * TQ]
        gnt_ref[g, :, rows] = gates_t[g * HEAD_DIM:g * HEAD_DIM + GROWS, :]

    def col(o):
        return jnp.dot(xn, w_ref[:, o:o + D_MODEL], preferred_element_type=F32)

    u = col(OFF_CONV_C) * col(OFF_CONV_X)
    u_scr[8 + r0:8 + r0 + n, :] = u
    cw = cw_ref[...]
    y = (cw[2:3, :] * u + cw[1:2, :] * u_scr[7 + r0:7 + r0 + n, :]
         + cw[0:1, :] * u_scr[6 + r0:6 + r0 + n, :])
    z = col(OFF_CONV_B) * y
    zb = jnp.dot(z.astype(BF16), wb_ref[...], preferred_element_type=F32)
    ob_ref[rows, :] = jax.nn.sigmoid(col(OFF_GATE_B)) * zb
    sga_ref[rows, :] = jax.nn.sigmoid(col(OFF_GATE_A))


def _proj_call(x2, g1, w_in_p, w_b, qg, kg, cw, B, S, tm):
    T = B * S
    nt = S // tm
    row = lambda b, i: (b * nt + i, 0)
    k_shape = jax.ShapeDtypeStruct((B, N_KV, S, LANES), BF16)
    k_spec = pl.BlockSpec((None, N_KV, tm, LANES), lambda b, i: (b, 0, i, 0))
    vt_shape = jax.ShapeDtypeStruct((B, N_KV, S // TQ, HEAD_DIM, TQ), BF16)
    vt_spec = pl.BlockSpec((None, N_KV, tm // TQ, HEAD_DIM, TQ), lambda b, i: (b, 0, i, 0, 0))
    return pl.pallas_call(
        _proj_kernel,
        grid=(B, nt),
        in_specs=[
            pl.BlockSpec((tm, D_MODEL), row),
            _const_spec((1, D_MODEL)),
            _const_spec((D_MODEL, W_IN_COLS)),
            _const_spec((D_MODEL, D_MODEL)),
            _const_spec((HEAD_DIM, 1)),
            _const_spec((3, LANES)),
            _const_spec((CONV_K, D_MODEL)),
        ],
        out_specs=[
            pl.BlockSpec((None, N_HEADS, HEAD_DIM, tm), lambda b, i: (b, 0, 0, i)),
            pl.BlockSpec((tm, KVW), row),
            pl.BlockSpec((tm, KVW), row),
            k_spec, vt_spec, k_spec, vt_spec,
            pl.BlockSpec((None, N_KV, GROWS, tm), lambda b, i: (b, 0, 0, i)),
            pl.BlockSpec((tm, D_MODEL), row),
            pl.BlockSpec((tm, D_MODEL), row),
        ],
        out_shape=[
            jax.ShapeDtypeStruct((B, N_HEADS, HEAD_DIM, S), BF16),
            jax.ShapeDtypeStruct((T, KVW), F32),
            jax.ShapeDtypeStruct((T, KVW), F32),
            k_shape, vt_shape, k_shape, vt_shape,
            jax.ShapeDtypeStruct((B, N_KV, GROWS, S), F32),
            jax.ShapeDtypeStruct((T, D_MODEL), F32),
            jax.ShapeDtypeStruct((T, D_MODEL), F32),
        ],
        scratch_shapes=[pltpu.VMEM((tm + 8, D_MODEL), F32)],
        compiler_params=pltpu.CompilerParams(
            dimension_semantics=("arbitrary", "arbitrary"),
            vmem_limit_bytes=VMEM_LIMIT),
        name="proj",
    )(x2, g1, w_in_p, w_b, qg, kg, cw)


def _cmp_kernel(kci_ref, vci_ref, pos_ref, w_ref, kg_ref, kc_ref, vct_ref, b_scr):
    ncp = kc_ref.shape[1]
    acc_a = jnp.zeros((ncp, CMP_OUT), F32)
    acc_b = jnp.zeros((ncp, CMP_OUT), F32)
    for l in range(CMP_STRIDE):
        rows = jnp.concatenate([kci_ref[pl.ds(l, ncp, stride=CMP_STRIDE), :],
                                vci_ref[pl.ds(l, ncp, stride=CMP_STRIDE), :]], axis=1)
        xa = (rows + pos_ref[l:l + 1, :]).astype(BF16)
        xb = (rows + pos_ref[CMP_STRIDE + l:CMP_STRIDE + l + 1, :]).astype(BF16)
        acc_a = acc_a + jnp.dot(xa, w_ref[l], preferred_element_type=F32)
        acc_b = acc_b + jnp.dot(xb, w_ref[CMP_STRIDE + l], preferred_element_type=F32)
    b_scr[0:ncp, :] = acc_b
    b_scr[ncp:ncp + 8, :] = jnp.zeros((8, CMP_OUT), F32)
    kcv = acc_a + b_scr[1:ncp + 1, :]
    c_end = lax.broadcasted_iota(jnp.int32, (ncp, 1), 0) * CMP_STRIDE + (CMP_BLOCK - 1)
    key_lanes = _alibi_key_lanes(c_end)
    vc_t = kcv[:, 2 * LANES:CMP_OUT].T.astype(BF16)
    for g in range(N_KV):
        kc = _rms_padded(kcv[:, g * LANES:(g + 1) * LANES], kg_ref[0:1, :])
        kc_ref[g] = (kc + key_lanes).astype(BF16)
        vct_ref[g] = vc_t[g * HEAD_DIM:(g + 1) * HEAD_DIM, :]


def _cmp_call(kci, vci, pos4, w_bd, kg, B, S):
    ncp = S // CMP_STRIDE
    return pl.pallas_call(
        _cmp_kernel,
        grid=(B,),
        in_specs=[
            pl.BlockSpec((S, KVW), lambda b: (b, 0)),
            pl.BlockSpec((S, KVW), lambda b: (b, 0)),
            _const_spec((CMP_BLOCK, 2 * KVW)),
            _const_spec((CMP_BLOCK, 2 * KVW, CMP_OUT)),
            _const_spec((3, LANES)),
        ],
        out_specs=[
            pl.BlockSpec((None, N_KV, ncp, LANES), lambda b: (b, 0, 0, 0)),
            pl.BlockSpec((None, N_KV, HEAD_DIM, ncp), lambda b: (b, 0, 0, 0)),
        ],
        out_shape=[
            jax.ShapeDtypeStruct((B, N_KV, ncp, LANES), BF16),
            jax.ShapeDtypeStruct((B, N_KV, HEAD_DIM, ncp), BF16),
        ],
        scratch_shapes=[pltpu.VMEM((ncp + 8, CMP_OUT), F32)],
        compiler_params=pltpu.CompilerParams(
            dimension_semantics=("arbitrary",), vmem_limit_bytes=VMEM_LIMIT),
        name="compress",
    )(kci, vci, pos4, w_bd, kg)


def _attn_kernel(qt_ref, kc_ref, vct_ref, ks_ref, vst_ref, kw_ref, vwt_ref, gnt_ref,
                 ovl_ref, oh_ref, cband_ref, eye_ref, o_ref, score_scr, cnt_scr,
                 num_s, den_s, num_w, den_w):
    acc_s = (num_s, den_s)
    acc_w = (num_w, den_w)
    g = pl.program_id(1)
    i = pl.program_id(2)
    t0 = i * QT
    kt_last = TILES_Q * (i + 1) - 1
    ns = ovl_ref.shape[0]
    ncp = ovl_ref.shape[1]

    lane = lax.broadcasted_iota(jnp.int32, (1, NL), 1)
    tq = t0 + (lane & (QT - 1))
    head = g * GQA + (lane >> QT_SHIFT) + 1
    slope = lax.bitcast_convert_type((127 - head) << 23, F32)
    a_t = (tq >> 6).astype(F32)
    b_t = (tq & 63).astype(F32)
    sub = lax.broadcasted_iota(jnp.int32, (HEAD_DIM, NL), 0)
    pos_col = lax.broadcasted_iota(jnp.int32, (TQ, 1), 0)
    qt = jnp.concatenate([qt_ref[r] for r in range(GQA)], axis=1)

    def aug_query(ref):
        rows = jnp.where(sub == 0, -(slope * 64.0) * a_t,
                         jnp.where(sub == 1, -slope * b_t,
                                   jnp.where(sub == 2, slope * 64.0,
                                             jnp.where(sub == 3, slope,
                                                       jnp.where(sub == 4, -ref, 0.0)))))
        return jnp.concatenate([qt, rows.astype(BF16)], axis=0)

    def tile(k_ref, kt):
        return k_ref[pl.ds(pl.multiple_of(kt * TQ, TQ), TQ), :]

    def causal(kt):
        return tq >= (kt * TQ + pos_col)

    def self_score(k_ref):
        k_t = jnp.concatenate([tile(k_ref, TILES_Q * i + c).astype(F32).T[0:HEAD_DIM, :]
                               for c in range(TILES_Q)], axis=1)
        return jnp.sum(qt.astype(F32) * jnp.concatenate([k_t] * GQA, axis=1),
                       axis=0, keepdims=True)

    def fold8(p):
        out = p[0:8, :]
        for k in range(1, TQ // 8):
            out = out + p[8 * k:8 * k + 8, :]
        return out

    def plain_tiles(score_fn, vt_ref, state, kts, masks, start, hook=None):
        acc_ref, den_ref = state
        ahead = [score_fn(kt) for kt in kts[:LOOKAHEAD]]
        if hook is not None:
            hook()
        total = None if start else acc_ref[...]
        den = None if start else den_ref[...]
        for n, (kt, mk) in enumerate(zip(kts, masks)):
            sc = ahead.pop(0)
            if n + LOOKAHEAD < len(kts):
                ahead.append(score_fn(kts[n + LOOKAHEAD]))
            if mk is not None:
                sc = jnp.where(mk, sc, NEG_INF)
            pr = jnp.exp(sc)
            pv = jnp.dot(vt_ref[kt], pr.astype(BF16), preferred_element_type=F32)
            total = pv if total is None else total + pv
            den = fold8(pr) if den is None else den + fold8(pr)
        acc_ref[...] = total
        den_ref[...] = den

    def online_tile(score_fn, vt_ref, state, kt, mask, m_run):
        acc_ref, den_ref = state
        sc = jnp.where(mask, score_fn(kt), NEG_INF)
        mx = jnp.max(sc, axis=0, keepdims=True)
        m_new = mx if m_run is None else jnp.maximum(m_run, mx)
        pr = jnp.exp(sc - m_new)
        pv = jnp.dot(vt_ref[kt], pr.astype(BF16), preferred_element_type=F32)
        if m_run is None:
            acc_ref[...] = pv
            den_ref[...] = fold8(pr)
        else:
            alpha = jnp.exp(m_run - m_new)
            acc_ref[...] = alpha * acc_ref[...] + pv
            den_ref[...] = alpha * den_ref[...] + fold8(pr)
        return m_new

    def finish(state):
        acc_ref, den_ref = state
        den = jnp.sum(den_ref[...], axis=0, keepdims=True)
        return acc_ref[...] / jnp.maximum(den, 1e-30)

    band = cband_ref[pl.ds(pl.multiple_of(ncp - (QT // CMP_STRIDE) * i, 8), ncp), :]
    lhs_c = jnp.concatenate([kc_ref[...], band.astype(BF16)], axis=1)
    rhs_c = jnp.concatenate([aug_query(0.0), eye_ref[...]], axis=0)
    has_cmp = tq >= CMP_BLOCK - 1
    qa_win = aug_query(self_score(kw_ref))

    def win_scores(kt):
        return jnp.dot(tile(kw_ref, kt), qa_win, preferred_element_type=F32)

    def cmp_and_window(win_kts, win_masks):
        s = jnp.dot(lhs_c, rhs_c, preferred_element_type=F32)
        out = []

        def cmp_rest():
            e = jnp.exp(s - jnp.max(s, axis=0, keepdims=True))
            den = jnp.maximum(jnp.sum(e, axis=0, keepdims=True), 1e-30)
            p = (e * jnp.where(has_cmp, 1.0 / den, 0.0)).astype(BF16)
            out.append(jnp.dot(vct_ref[...], p, preferred_element_type=F32))
            out.append(jnp.dot(ovl_ref[...], p, preferred_element_type=F32))

        plain_tiles(win_scores, vwt_ref, acc_w, win_kts, win_masks, True, hook=cmp_rest)
        return out[0], out[1]

    n_win = WINDOW // TQ
    n_wt = n_win + TILES_Q

    def win_steady():
        kts = [kt_last - d for d in range(n_wt)]
        masks = [causal(kt) for kt in kts[:TILES_Q]] + [None] * (n_win - TILES_Q)
        masks += [(tq - (kt * TQ + pos_col)) < WINDOW for kt in kts[n_win:]]
        return cmp_and_window(kts, masks)

    def win_start():
        kts = list(range(n_wt - 1))
        return cmp_and_window(kts, [causal(kt) for kt in kts])

    o_cmp, imp4 = lax.cond(kt_last >= n_wt - 1, win_steady, win_start)
    o_win = finish(acc_w)

    imp = imp4[:, 0:QT]
    for r in range(1, GQA):
        imp = imp + imp4[:, r * QT:(r + 1) * QT]
    j = lax.broadcasted_iota(jnp.int32, (ns, QT), 0)
    cur = (t0 + lax.broadcasted_iota(jnp.int32, (ns, QT), 1)) >> 6
    forced = (j == 0) | (j == cur) | (j == cur - 1)
    score = jnp.where(forced, FORCE_SCORE, jnp.where(j <= cur, imp, -1.0))
    score_scr[...] = score
    cnt_scr[...] = jnp.zeros((ns, QT), jnp.int32)

    row8 = lax.broadcasted_iota(jnp.int32, (8, QT), 0)
    for kg in range(ns // 8):
        @pl.when(8 * kg <= (TQ // SLC_BLOCK) * (kt_last + 1) - 1)
        def _(kg=kg):
            cnt = [cnt_scr[8 * v:8 * v + 8, :] for v in range(ns // 8)]
            for b in range(8 * kg, 8 * kg + 8):
                sb = score_scr[b:b + 1, :]
                for v in range(ns // 8):
                    sc_v = score[8 * v:8 * v + 8, :]
                    if v < kg:
                        ahead = sb > sc_v
                    elif v > kg:
                        ahead = sb >= sc_v
                    else:
                        ahead = (sb > sc_v) | ((sb == sc_v) & (row8 > b - 8 * kg))
                    cnt[v] = cnt[v] + jnp.where(ahead, 1, 0)
            for v in range(ns // 8):
                cnt_scr[8 * v:8 * v + 8, :] = cnt[v]

    sel = (cnt_scr[...] < N_SLC) & (j <= cur)
    selb = jnp.where(sel, 0.0, NEG_INF).astype(BF16)
    selb = jnp.concatenate([selb, jnp.zeros((LANES - ns, QT), BF16)], axis=0)

    sel_rows = jnp.concatenate([selb] * GQA, axis=1)
    qa_sel = jnp.concatenate([aug_query(self_score(ks_ref)), sel_rows], axis=0)

    def sel_scores(kt):
        lhs = jnp.concatenate([tile(ks_ref, kt), tile(oh_ref, kt)], axis=1)
        return jnp.dot(lhs, qa_sel, preferred_element_type=F32)

    def sel_group(base, n, masked, start):
        kts = [base + u for u in range(n)]
        masks = [causal(kt) if masked else None for kt in kts]
        plain_tiles(sel_scores, vst_ref, acc_s, kts, masks, start)

    base = (kt_last // UNROLL) * UNROLL
    for v in range(UNROLL // TILES_Q):
        @pl.when((kt_last % UNROLL) // TILES_Q == v)
        def _(v=v):
            kts = [base + u for u in range(TILES_Q * (v + 1))]
            masks = [None] * (TILES_Q * v) + [causal(kt) for kt in kts[TILES_Q * v:]]
            plain_tiles(sel_scores, vst_ref, acc_s, kts, masks, True)

    def sel_body(grp, carry):
        sel_group(grp * UNROLL, UNROLL, False, False)
        return carry

    lax.fori_loop(0, kt_last // UNROLL, sel_body, 0)
    o_slc = finish(acc_s)

    def gate(branch):
        return jnp.concatenate(
            [gnt_ref[branch * GQA + r:branch * GQA + r + 1, :] for r in range(GQA)], axis=1)

    def emit(o_slc, o_win):
        o = gate(0) * o_cmp + gate(1) * o_slc + gate(2) * o_win
        for h in range(GQA // 2):
            pair = jnp.concatenate([o[:, (2 * h) * QT:(2 * h + 1) * QT],
                                    o[:, (2 * h + 1) * QT:(2 * h + 2) * QT]], axis=0)
            for c in range(TILES_Q):
                o_ref[c * TQ:(c + 1) * TQ, h * LANES:(h + 1) * LANES] = (
                    pair[:, c * TQ:(c + 1) * TQ].T.astype(BF16))

    emit(o_slc, o_win)

    bad = jnp.max(jnp.where(jnp.isfinite(o_slc) & jnp.isfinite(o_win), 0.0, 1.0))

    @pl.when(bad > 0.0)
    def _():
        m_run = online_tile(sel_scores, vst_ref, acc_s, kt_last, causal(kt_last), None)
        lax.fori_loop(0, kt_last, lambda kt, m_in: online_tile(sel_scores, vst_ref, acc_s, kt,
                                                               causal(kt), m_in), m_run)
        m_run = None
        for d in list(range(TILES_Q - 1, n_wt)) + list(range(TILES_Q - 1)):
            pos = (kt_last - d) * TQ + pos_col
            inside = (tq >= pos) & (tq - pos < WINDOW) & (pos >= 0)
            m_run = online_tile(win_scores, vwt_ref, acc_w, jnp.maximum(kt_last - d, 0), inside,
                                m_run)
        emit(finish(acc_s), finish(acc_w))


def _attn_call(qt, kc, vct, ks, vst, kw, vwt, gnt, ovl, onehot, cband, eye, B, S):
    T = B * S
    nq = S // QT
    ncp = S // CMP_STRIDE
    ns = S // SLC_BLOCK
    bg4 = lambda b, g, i: (b, g, 0, 0)
    bg5 = lambda b, g, i: (b, g, 0, 0, 0)
    k_spec = pl.BlockSpec((None, None, S, LANES), bg4)
    vt_spec = pl.BlockSpec((None, None, S // TQ, HEAD_DIM, TQ), bg5)
    return pl.pallas_call(
        _attn_kernel,
        grid=(B, N_KV, nq),
        in_specs=[
            pl.BlockSpec((None, GQA, HEAD_DIM, QT), lambda b, g, i: (b, g, 0, i)),
            pl.BlockSpec((None, None, ncp, LANES), bg4),
            pl.BlockSpec((None, None, HEAD_DIM, ncp), bg4),
            k_spec, vt_spec, k_spec, vt_spec,
            pl.BlockSpec((None, None, GROWS, QT), lambda b, g, i: (b, g, 0, i)),
            _const_spec((ns, ncp)),
            _const_spec((S, LANES)),
            _const_spec((2 * ncp, QT)),
            _const_spec((QT, NL)),
        ],
        out_specs=pl.BlockSpec((QT, GQA * HEAD_DIM), lambda b, g, i: (b * nq + i, g)),
        out_shape=jax.ShapeDtypeStruct((T, QW), BF16),
        scratch_shapes=[pltpu.VMEM((ns, QT), F32), pltpu.VMEM((ns, QT), jnp.int32),
                        pltpu.VMEM((HEAD_DIM, NL), F32), pltpu.VMEM((8, NL), F32),
                        pltpu.VMEM((HEAD_DIM, NL), F32), pltpu.VMEM((8, NL), F32)],
        compiler_params=pltpu.CompilerParams(
            dimension_semantics=("arbitrary", "arbitrary", "arbitrary"),
            vmem_limit_bytes=VMEM_LIMIT),
        name="attn",
    )(qt, kc, vct, ks, vst, kw, vwt, gnt, ovl, onehot, cband, eye)


FF_CHUNK = 512


def _mlp_kernel(x_ref, o_ref, sga_ref, ob_ref, wa_ref, wo_ref, g2_ref, wu_ref, wd_ref,
                out_ref, acc_scr):
    a = jnp.dot(o_ref[...], wa_ref[...], preferred_element_type=F32)
    mixed = sga_ref[...] * a + ob_ref[...]
    x1 = x_ref[...] + jnp.dot(mixed.astype(BF16), wo_ref[...], preferred_element_type=F32)
    h = _rms(x1, g2_ref[...]).astype(BF16)
    acc_scr[...] = x1
    for c in range(D_FF // FF_CHUNK):
        lo, hi = c * FF_CHUNK, (c + 1) * FF_CHUNK
        up = jnp.dot(h, wu_ref[:, lo:hi], preferred_element_type=F32)
        act = jnp.square(jnp.maximum(up, 0.0)).astype(BF16)
        acc_scr[...] += jnp.dot(act, wd_ref[lo:hi, :], preferred_element_type=F32)
    out_ref[...] = acc_scr[...]


def _mlp_call(x2, o_nsa, sga, ob, w_a, w_o, g2, w_up, w_down, tm):
    T = x2.shape[0]
    row = lambda i: (i, 0)
    return pl.pallas_call(
        _mlp_kernel,
        grid=(T // tm,),
        in_specs=[
            pl.BlockSpec((tm, D_MODEL), row),
            pl.BlockSpec((tm, QW), row),
            pl.BlockSpec((tm, D_MODEL), row),
            pl.BlockSpec((tm, D_MODEL), row),
            _const_spec((QW, D_MODEL)),
            _const_spec((D_MODEL, D_MODEL)),
            _const_spec((1, D_MODEL)),
            _const_spec((D_MODEL, D_FF)),
            _const_spec((D_FF, D_MODEL)),
        ],
        out_specs=pl.BlockSpec((tm, D_MODEL), row),
        out_shape=jax.ShapeDtypeStruct((T, D_MODEL), F32),
        scratch_shapes=[pltpu.VMEM((tm, D_MODEL), F32)],
        compiler_params=pltpu.CompilerParams(
            dimension_semantics=("arbitrary",), vmem_limit_bytes=VMEM_LIMIT),
        name="mlp",
    )(x2, o_nsa, sga, ob, w_a, w_o, g2, w_up, w_down)


def _pack_w_in(w_in):
    w_in = w_in.astype(BF16)
    o = OFF_GN
    gates = w_in[:, o:o + 3 * N_HEADS]
    gates = gates.reshape(D_MODEL, 3, N_KV, GQA).transpose(0, 2, 1, 3)
    gates = gates.reshape(D_MODEL, N_KV, 3 * GQA)
    gates = jnp.pad(gates, ((0, 0), (0, 0), (0, HEAD_DIM - 3 * GQA))).reshape(D_MODEL, LANES)
    return jnp.concatenate([w_in[:, :o], gates, w_in[:, o + 3 * N_HEADS:]], axis=1)


def _pack_cmp(w_cmp_k, w_cmp_v, cmp_pos_k, cmp_pos_v):
    wk = w_cmp_k.reshape(CMP_BLOCK, HEAD_DIM, HEAD_DIM)
    wv = w_cmp_v.reshape(CMP_BLOCK, HEAD_DIM, HEAD_DIM)
    slabs = [jnp.pad(blk.astype(BF16), ((0, 0), (0, 0), (col, CMP_OUT - col - HEAD_DIM)))
             for blk, col in ((wk, 0), (wk, LANES), (wv, 2 * LANES), (wv, 2 * LANES + HEAD_DIM))]
    pos4 = jnp.concatenate([cmp_pos_k, cmp_pos_k, cmp_pos_v, cmp_pos_v], axis=1)
    return jnp.concatenate(slabs, axis=1), pos4


def _pad_gain(g):
    return jnp.pad(g, ((0, 0), (0, LANES - HEAD_DIM)))


def _overlap(S):
    ncp = S // CMP_STRIDE
    ns = S // SLC_BLOCK
    c_start = np.arange(ncp) * CMP_STRIDE
    s_start = np.arange(ns) * SLC_BLOCK
    ov = np.clip(np.minimum(c_start[None, :] + CMP_BLOCK, s_start[:, None] + SLC_BLOCK)
                 - np.maximum(c_start[None, :], s_start[:, None]), 0, None)
    return jnp.asarray(ov.astype(np.float32) / CMP_BLOCK, dtype=BF16)


def _block_onehot(S):
    pos = np.arange(S)
    oh = np.zeros((S, LANES), np.float32)
    oh[pos, pos // SLC_BLOCK] = 1.0
    return jnp.asarray(oh, dtype=BF16)


def _cmp_band(S):
    ncp = S // CMP_STRIDE
    c_rel = np.arange(2 * ncp)[:, None] - ncp
    visible = c_rel * CMP_STRIDE + (CMP_BLOCK - 1) <= np.arange(QT)[None, :]
    return jnp.asarray(np.where(visible, 0.0, NEG_INF), dtype=F32)


def _tiled_eye():
    return jnp.asarray(np.tile(np.eye(QT, dtype=np.float32), (1, GQA)), dtype=BF16)


def _layer(x, norm1_g, w_in, q_norm_g, k_norm_g, cmp_pos_k, cmp_pos_v, w_cmp_k, w_cmp_v,
           conv_w, w_branch_a, w_branch_b, w_out, norm2_g, w_up, w_down):
    B, S, _ = x.shape
    assert S % (CMP_STRIDE * LANES) == 0 and S // SLC_BLOCK <= LANES
    assert (S // TQ) % UNROLL == 0 and S >= WINDOW
    x2 = x.reshape(B * S, D_MODEL)
    w_bd, pos4 = _pack_cmp(w_cmp_k, w_cmp_v, cmp_pos_k, cmp_pos_v)
    kg = _pad_gain(k_norm_g)
    qt, kci, vci, ks, vst, kw, vwt, gnt, sga, ob = _proj_call(
        x2, norm1_g[None, :], _pack_w_in(w_in), w_branch_b.astype(BF16),
        q_norm_g[:, None], kg, conv_w, B, S, tm=512)
    kc, vct = _cmp_call(kci, vci, pos4, w_bd, kg, B, S)
    o_nsa = _attn_call(qt, kc, vct, ks, vst, kw, vwt, gnt, _overlap(S), _block_onehot(S),
                       _cmp_band(S), _tiled_eye(), B, S)
    out = _mlp_call(x2, o_nsa, sga, ob, w_branch_a.astype(BF16), w_out.astype(BF16),
                    norm2_g[None, :], w_up.astype(BF16), w_down.astype(BF16), tm=512)
    return out.reshape(B, S, D_MODEL)


@jax.jit
def kernel(x, norm1_g, w_in, q_norm_g, k_norm_g, cmp_pos_k, cmp_pos_v, w_cmp_k, w_cmp_v,
           conv_w, w_branch_a, w_branch_b, w_out, norm2_g, w_up, w_down):
    for l in range(norm1_g.shape[0]):
        x = _layer(x, norm1_g[l], w_in[l], q_norm_g[l], k_norm_g[l], cmp_pos_k[l],
                   cmp_pos_v[l], w_cmp_k[l], w_cmp_v[l], conv_w[l], w_branch_a[l],
                   w_branch_b[l], w_out[l], norm2_g[l], w_up[l], w_down[l])
    return x
```

```python
import numpy as np
import jax
import jax.numpy as jnp
from jax import lax
from jax.experimental import pallas as pl
from jax.experimental.pallas import tpu as pltpu

D_MODEL = 1024
N_HEADS = 8
HEAD_DIM = 64
N_KV = 2
GQA = N_HEADS // N_KV
CMP_BLOCK = 32
CMP_STRIDE = 16
SLC_BLOCK = 64
N_SLC = 16
WINDOW = 512
FORCE_SCORE = 1e4
SCALE = 0.125
CONV_K = 3
D_FF = 4 * D_MODEL
EPS = 1e-6
NEG_INF = -1e30

LANES = 128
QW = N_HEADS * HEAD_DIM
KVW = N_KV * HEAD_DIM

OFF_Q = 0
OFF_KC = OFF_Q + QW
OFF_VC = OFF_KC + KVW
OFF_KS = OFF_VC + KVW
OFF_VS = OFF_KS + KVW
OFF_KW = OFF_VS + KVW
OFF_VW = OFF_KW + KVW
OFF_GN = OFF_VW + KVW
PA_COLS = OFF_GN + LANES
OFF_CONV_B = PA_COLS
OFF_CONV_C = OFF_CONV_B + D_MODEL
OFF_CONV_X = OFF_CONV_C + D_MODEL
OFF_GATE_A = OFF_CONV_X + D_MODEL
OFF_GATE_B = OFF_GATE_A + D_MODEL
W_IN_COLS = OFF_GATE_B + D_MODEL

PROJ_SUB = 256
TQ = 256
QT = 256
QT_SHIFT = QT.bit_length() - 1
TILES_Q = QT // TQ
NL = GQA * QT
UNROLL = 4
LOOKAHEAD = 1
GROWS = 16
CMP_OUT = 2 * LANES + KVW
VMEM_LIMIT = 56 * 1024 * 1024

F32 = jnp.float32
BF16 = jnp.bfloat16


def _rms(x, g):
    return x * lax.rsqrt(jnp.mean(x * x, axis=-1, keepdims=True) + EPS) * g


def _rms_padded(x, g):
    ms = jnp.sum(x * x, axis=-1, keepdims=True) * (1.0 / HEAD_DIM)
    return x * lax.rsqrt(ms + EPS) * g


def _alibi_key_lanes(pos):
    lane = lax.broadcasted_iota(jnp.int32, (pos.shape[0], LANES), 1)
    hi = (pos >> 6).astype(F32)
    lo = (pos & 63).astype(F32)
    return jnp.where((lane == HEAD_DIM) | (lane == HEAD_DIM + 1) | (lane == HEAD_DIM + 4), 1.0,
                     jnp.where(lane == HEAD_DIM + 2, hi,
                               jnp.where(lane == HEAD_DIM + 3, lo, 0.0)))


def _const_spec(shape):
    zeros = (0,) * len(shape)
    return pl.BlockSpec(shape, lambda *_: zeros, pipeline_mode=pl.Buffered(1))


def _proj_kernel(x_ref, g1_ref, w_ref, wb_ref, qg_ref, kg_ref, cw_ref,
                 qt_ref, kci_ref, vci_ref, ks_ref, vst_ref, kw_ref, vwt_ref, gnt_ref,
                 sga_ref, ob_ref, u_scr):
    i = pl.program_id(1)
    tm = x_ref.shape[0]

    @pl.when(i == 0)
    def _():
        u_scr[0:8, :] = jnp.zeros((8, D_MODEL), F32)

    for r0 in range(0, tm, PROJ_SUB):
        _proj_rows(r0, i * tm + r0, x_ref, g1_ref, w_ref, wb_ref, qg_ref, kg_ref, cw_ref,
                   qt_ref, kci_ref, vci_ref, ks_ref, vst_ref, kw_ref, vwt_ref, gnt_ref,
                   sga_ref, ob_ref, u_scr)
    u_scr[0:8, :] = u_scr[tm:tm + 8, :]


def _proj_rows(r0, pos0, x_ref, g1_ref, w_ref, wb_ref, qg_ref, kg_ref, cw_ref,
               qt_ref, kci_ref, vci_ref, ks_ref, vst_ref, kw_ref, vwt_ref, gnt_ref,
               sga_ref, ob_ref, u_scr):
    n = PROJ_SUB
    rows = slice(r0, r0 + n)
    xn = _rms(x_ref[rows, :], g1_ref[...]).astype(BF16)

    pa = jnp.dot(xn, w_ref[:, 0:PA_COLS], preferred_element_type=F32)

    qg = qg_ref[...]
    for pair in range(N_HEADS // 2):
        qp = pa[:, OFF_Q + pair * LANES:OFF_Q + (pair + 1) * LANES].T
        for half in range(2):
            qh = qp[half * HEAD_DIM:(half + 1) * HEAD_DIM, :]
            ms = jnp.mean(qh * qh, axis=0, keepdims=True)
            qt_ref[2 * pair + half, :, rows] = (qh * lax.rsqrt(ms + EPS) * qg * SCALE).astype(BF16)

    kci_ref[rows, :] = pa[:, OFF_KC:OFF_KC + KVW]
    vci_ref[rows, :] = pa[:, OFF_VC:OFF_VC + KVW]

    pos = pos0 + lax.broadcasted_iota(jnp.int32, (n, 1), 0)
    key_lanes = _alibi_key_lanes(pos)
    low = lax.broadcasted_iota(jnp.int32, (n, LANES), 1) < HEAD_DIM
    ks2 = pa[:, OFF_KS:OFF_KS + KVW]
    kw2 = pa[:, OFF_KW:OFF_KW + KVW]
    vs_t = pa[:, OFF_VS:OFF_VS + KVW].T.astype(BF16)
    vw_t = pa[:, OFF_VW:OFF_VW + KVW].T.astype(BF16)
    gates_t = jax.nn.sigmoid(pa[:, OFF_GN:OFF_GN + LANES]).T
    for g in range(N_KV):
        ks = jnp.where(low, ks2 if g == 0 else pltpu.roll(ks2, HEAD_DIM, axis=1), 0.0)
        kw = jnp.where(low, kw2 if g == 0 else pltpu.roll(kw2, HEAD_DIM, axis=1), 0.0)
        ks_ref[g, rows, :] = (_rms_padded(ks, kg_ref[1:2, :]) + key_lanes).astype(BF16)
        kw_ref[g, rows, :] = (_rms_padded(kw, kg_ref[2:3, :]) + key_lanes).astype(BF16)
        for c in range(n // TQ):
            ct = r0 // TQ + c
            vst_ref[g, ct] = vs_t[g * HEAD_DIM:(g + 1) * HEAD_DIM, c * TQ:(c + 1) * TQ]
            vwt_ref[g, ct] = vw_t[g * HEAD_DIM:(g + 1) * HEAD_DIM, c * TQ:(c + 1) * TQ]
        for branch in range(GROWS // GQA):
            src = branch * N_HEADS + g * GQA
            gnt_ref[g, branch * GQA:(branch + 1) * GQA, rows] = gates_t[src:src + GQA, :]

    def col(o):
        return jnp.dot(xn, w_ref[:, o:o + D_MODEL], preferred_element_type=F32)

    u = col(OFF_CONV_C) * col(OFF_CONV_X)
    u_scr[8 + r0:8 + r0 + n, :] = u
    cw = cw_ref[...]
    y = (cw[2:3, :] * u + cw[1:2, :] * u_scr[7 + r0:7 + r0 + n, :]
         + cw[0:1, :] * u_scr[6 + r0:6 + r0 + n, :])
    z = col(OFF_CONV_B) * y
    zb = jnp.dot(z.astype(BF16), wb_ref[...], preferred_element_type=F32)
    ob_ref[rows, :] = jax.nn.sigmoid(col(OFF_GATE_B)) * zb
    sga_ref[rows, :] = jax.nn.sigmoid(col(OFF_GATE_A))


def _proj_call(x2, g1, w_in_p, w_b, qg, kg, cw, B, S, tm):
    T = B * S
    nt = S // tm
    row = lambda b, i: (b * nt + i, 0)
    k_shape = jax.ShapeDtypeStruct((B, N_KV, S, LANES), BF16)
    k_spec = pl.BlockSpec((None, N_KV, tm, LANES), lambda b, i: (b, 0, i, 0))
    vt_shape = jax.ShapeDtypeStruct((B, N_KV, S // TQ, HEAD_DIM, TQ), BF16)
    vt_spec = pl.BlockSpec((None, N_KV, tm // TQ, HEAD_DIM, TQ), lambda b, i: (b, 0, i, 0, 0))
    return pl.pallas_call(
        _proj_kernel,
        grid=(B, nt),
        in_specs=[
            pl.BlockSpec((tm, D_MODEL), row),
            _const_spec((1, D_MODEL)),
            _const_spec((D_MODEL, W_IN_COLS)),
            _const_spec((D_MODEL, D_MODEL)),
            _const_spec((HEAD_DIM, 1)),
            _const_spec((3, LANES)),
            _const_spec((CONV_K, D_MODEL)),
        ],
        out_specs=[
            pl.BlockSpec((None, N_HEADS, HEAD_DIM, tm), lambda b, i: (b, 0, 0, i)),
            pl.BlockSpec((tm, KVW), row),
            pl.BlockSpec((tm, KVW), row),
            k_spec, vt_spec, k_spec, vt_spec,
            pl.BlockSpec((None, N_KV, GROWS, tm), lambda b, i: (b, 0, 0, i)),
            pl.BlockSpec((tm, D_MODEL), row),
            pl.BlockSpec((tm, D_MODEL), row),
        ],
        out_shape=[
            jax.ShapeDtypeStruct((B, N_HEADS, HEAD_DIM, S), BF16),
            jax.ShapeDtypeStruct((T, KVW), F32),
            jax.ShapeDtypeStruct((T, KVW), F32),
            k_shape, vt_shape, k_shape, vt_shape,
            jax.ShapeDtypeStruct((B, N_KV, GROWS, S), F32),
            jax.ShapeDtypeStruct((T, D_MODEL), F32),
            jax.ShapeDtypeStruct((T, D_MODEL), F32),
        ],
        scratch_shapes=[pltpu.VMEM((tm + 8, D_MODEL), F32)],
        compiler_params=pltpu.CompilerParams(
            dimension_semantics=("arbitrary", "arbitrary"),
            vmem_limit_bytes=VMEM_LIMIT),
        name="proj",
    )(x2, g1, w_in_p, w_b, qg, kg, cw)


def _cmp_kernel(kci_ref, vci_ref, pos_ref, w_ref, kg_ref, kc_ref, vct_ref, b_scr):
    ncp = kc_ref.shape[1]
    acc_a = jnp.zeros((ncp, CMP_OUT), F32)
    acc_b = jnp.zeros((ncp, CMP_OUT), F32)
    for l in range(CMP_STRIDE):
        rows = jnp.concatenate([kci_ref[pl.ds(l, ncp, stride=CMP_STRIDE), :],
                                vci_ref[pl.ds(l, ncp, stride=CMP_STRIDE), :]], axis=1)
        xa = (rows + pos_ref[l:l + 1, :]).astype(BF16)
        xb = (rows + pos_ref[CMP_STRIDE + l:CMP_STRIDE + l + 1, :]).astype(BF16)
        acc_a = acc_a + jnp.dot(xa, w_ref[l], preferred_element_type=F32)
        acc_b = acc_b + jnp.dot(xb, w_ref[CMP_STRIDE + l], preferred_element_type=F32)
    b_scr[0:ncp, :] = acc_b
    b_scr[ncp:ncp + 8, :] = jnp.zeros((8, CMP_OUT), F32)
    kcv = acc_a + b_scr[1:ncp + 1, :]
    c_end = lax.broadcasted_iota(jnp.int32, (ncp, 1), 0) * CMP_STRIDE + (CMP_BLOCK - 1)
    key_lanes = _alibi_key_lanes(c_end)
    vc_t = kcv[:, 2 * LANES:CMP_OUT].T.astype(BF16)
    for g in range(N_KV):
        kc = _rms_padded(kcv[:, g * LANES:(g + 1) * LANES], kg_ref[0:1, :])
        kc_ref[g] = (kc + key_lanes).astype(BF16)
        vct_ref[g] = vc_t[g * HEAD_DIM:(g + 1) * HEAD_DIM, :]


def _cmp_call(kci, vci, pos4, w_bd, kg, B, S):
    ncp = S // CMP_STRIDE
    return pl.pallas_call(
        _cmp_kernel,
        grid=(B,),
        in_specs=[
            pl.BlockSpec((S, KVW), lambda b: (b, 0)),
            pl.BlockSpec((S, KVW), lambda b: (b, 0)),
            _const_spec((CMP_BLOCK, 2 * KVW)),
            _const_spec((CMP_BLOCK, 2 * KVW, CMP_OUT)),
            _const_spec((3, LANES)),
        ],
        out_specs=[
            pl.BlockSpec((None, N_KV, ncp, LANES), lambda b: (b, 0, 0, 0)),
            pl.BlockSpec((None, N_KV, HEAD_DIM, ncp), lambda b: (b, 0, 0, 0)),
        ],
        out_shape=[
            jax.ShapeDtypeStruct((B, N_KV, ncp, LANES), BF16),
            jax.ShapeDtypeStruct((B, N_KV, HEAD_DIM, ncp), BF16),
        ],
        scratch_shapes=[pltpu.VMEM((ncp + 8, CMP_OUT), F32)],
        compiler_params=pltpu.CompilerParams(
            dimension_semantics=("arbitrary",), vmem_limit_bytes=VMEM_LIMIT),
        name="compress",
    )(kci, vci, pos4, w_bd, kg)


def _attn_kernel(qt_ref, kc_ref, vct_ref, ks_ref, vst_ref, kw_ref, vwt_ref, gnt_ref,
                 ovl_ref, oh_ref, cband_ref, eye_ref, o_ref, score_scr, cnt_scr,
                 num_s, den_s, num_w, den_w):
    acc_s = (num_s, den_s)
    acc_w = (num_w, den_w)
    g = pl.program_id(1)
    i = pl.program_id(2)
    t0 = i * QT
    kt_last = TILES_Q * (i + 1) - 1
    ns = ovl_ref.shape[0]
    ncp = ovl_ref.shape[1]

    lane = lax.broadcasted_iota(jnp.int32, (1, NL), 1)
    tq = t0 + (lane & (QT - 1))
    head = g * GQA + (lane >> QT_SHIFT) + 1
    slope = lax.bitcast_convert_type((127 - head) << 23, F32)
    a_t = (tq >> 6).astype(F32)
    b_t = (tq & 63).astype(F32)
    sub = lax.broadcasted_iota(jnp.int32, (HEAD_DIM, NL), 0)
    pos_col = lax.broadcasted_iota(jnp.int32, (TQ, 1), 0)
    qt = jnp.concatenate([qt_ref[r] for r in range(GQA)], axis=1)

    def aug_query(ref):
        rows = jnp.where(sub == 0, -(slope * 64.0) * a_t,
                         jnp.where(sub == 1, -slope * b_t,
                                   jnp.where(sub == 2, slope * 64.0,
                                             jnp.where(sub == 3, slope,
                                                       jnp.where(sub == 4, -ref, 0.0)))))
        return jnp.concatenate([qt, rows.astype(BF16)], axis=0)

    def tile(k_ref, kt):
        return k_ref[pl.ds(pl.multiple_of(kt * TQ, TQ), TQ), :]

    def causal(kt):
        return tq >= (kt * TQ + pos_col)

    def self_score(k_ref):
        k_t = jnp.concatenate([tile(k_ref, TILES_Q * i + c).astype(F32).T[0:HEAD_DIM, :]
                               for c in range(TILES_Q)], axis=1)
        return jnp.sum(qt.astype(F32) * jnp.concatenate([k_t] * GQA, axis=1),
                       axis=0, keepdims=True)

    def fold8(p):
        out = p[0:8, :]
        for k in range(1, TQ // 8):
            out = out + p[8 * k:8 * k + 8, :]
        return out

    def plain_tiles(score_fn, vt_ref, state, kts, masks, start, hook=None):
        acc_ref, den_ref = state
        ahead = [score_fn(kt) for kt in kts[:LOOKAHEAD]]
        if hook is not None:
            hook()
        total = None if start else acc_ref[...]
        den = None if start else den_ref[...]
        for n, (kt, mk) in enumerate(zip(kts, masks)):
            sc = ahead.pop(0)
            if n + LOOKAHEAD < len(kts):
                ahead.append(score_fn(kts[n + LOOKAHEAD]))
            if mk is not None:
                sc = jnp.where(mk, sc, NEG_INF)
            pr = jnp.exp(sc)
            pv = jnp.dot(vt_ref[kt], pr.astype(BF16), preferred_element_type=F32)
            total = pv if total is None else total + pv
            den = fold8(pr) if den is None else den + fold8(pr)
        acc_ref[...] = total
        den_ref[...] = den

    def online_tile(score_fn, vt_ref, state, kt, mask, m_run):
        acc_ref, den_ref = state
        sc = jnp.where(mask, score_fn(kt), NEG_INF)
        mx = jnp.max(sc, axis=0, keepdims=True)
        m_new = mx if m_run is None else jnp.maximum(m_run, mx)
        pr = jnp.exp(sc - m_new)
        pv = jnp.dot(vt_ref[kt], pr.astype(BF16), preferred_element_type=F32)
        if m_run is None:
            acc_ref[...] = pv
            den_ref[...] = fold8(pr)
        else:
            alpha = jnp.exp(m_run - m_new)
            acc_ref[...] = alpha * acc_ref[...] + pv
            den_ref[...] = alpha * den_ref[...] + fold8(pr)
        return m_new

    def finish(state):
        acc_ref, den_ref = state
        den = jnp.sum(den_ref[...], axis=0, keepdims=True)
        return acc_ref[...] / jnp.maximum(den, 1e-30)

    band = cband_ref[pl.ds(pl.multiple_of(ncp - (QT // CMP_STRIDE) * i, 8), ncp), :]
    lhs_c = jnp.concatenate([kc_ref[...], band.astype(BF16)], axis=1)
    rhs_c = jnp.concatenate([aug_query(0.0), eye_ref[...]], axis=0)
    has_cmp = tq >= CMP_BLOCK - 1
    qa_win = aug_query(self_score(kw_ref))

    def win_scores(kt):
        return jnp.dot(tile(kw_ref, kt), qa_win, preferred_element_type=F32)

    def cmp_and_window(win_kts, win_masks):
        s = jnp.dot(lhs_c, rhs_c, preferred_element_type=F32)
        out = []

        def cmp_rest():
            e = jnp.exp(s - jnp.max(s, axis=0, keepdims=True))
            den = jnp.maximum(jnp.sum(e, axis=0, keepdims=True), 1e-30)
            p = (e * jnp.where(has_cmp, 1.0 / den, 0.0)).astype(BF16)
            out.append(jnp.dot(vct_ref[...], p, preferred_element_type=F32))
            out.append(jnp.dot(ovl_ref[...], p, preferred_element_type=F32))

        plain_tiles(win_scores, vwt_ref, acc_w, win_kts, win_masks, True, hook=cmp_rest)
        return out[0], out[1]

    n_win = WINDOW // TQ
    n_wt = n_win + TILES_Q

    def win_steady():
        kts = [kt_last - d for d in range(n_wt)]
        masks = [causal(kt) for kt in kts[:TILES_Q]] + [None] * (n_win - TILES_Q)
        masks += [(tq - (kt * TQ + pos_col)) < WINDOW for kt in kts[n_win:]]
        return cmp_and_window(kts, masks)

    def win_start():
        kts = list(range(n_wt - 1))
        return cmp_and_window(kts, [causal(kt) for kt in kts])

    o_cmp, imp4 = lax.cond(kt_last >= n_wt - 1, win_steady, win_start)
    o_win = finish(acc_w)

    imp = imp4[:, 0:QT]
    for r in range(1, GQA):
        imp = imp + imp4[:, r * QT:(r + 1) * QT]
    j = lax.broadcasted_iota(jnp.int32, (ns, QT), 0)
    cur = (t0 + lax.broadcasted_iota(jnp.int32, (ns, QT), 1)) >> 6
    forced = (j == 0) | (j == cur) | (j == cur - 1)
    score = jnp.where(forced, FORCE_SCORE, jnp.where(j <= cur, imp, -1.0))
    score_scr[...] = score
    cnt_scr[...] = jnp.zeros((ns, QT), jnp.int32)

    row8 = lax.broadcasted_iota(jnp.int32, (8, QT), 0)
    for kg in range(ns // 8):
        @pl.when(8 * kg <= (TQ // SLC_BLOCK) * (kt_last + 1) - 1)
        def _(kg=kg):
            cnt = [cnt_scr[8 * v:8 * v + 8, :] for v in range(ns // 8)]
            for b in range(8 * kg, 8 * kg + 8):
                sb = score_scr[b:b + 1, :]
                for v in range(ns // 8):
                    sc_v = score[8 * v:8 * v + 8, :]
                    if v < kg:
                        ahead = sb > sc_v
                    elif v > kg:
                        ahead = sb >= sc_v
                    else:
                        ahead = (sb > sc_v) | ((sb == sc_v) & (row8 > b - 8 * kg))
                    cnt[v] = cnt[v] + jnp.where(ahead, 1, 0)
            for v in range(ns // 8):
                cnt_scr[8 * v:8 * v + 8, :] = cnt[v]

    sel = (cnt_scr[...] < N_SLC) & (j <= cur)
    selb = jnp.where(sel, 0.0, NEG_INF).astype(BF16)
    selb = jnp.concatenate([selb, jnp.zeros((LANES - ns, QT), BF16)], axis=0)

    sel_rows = jnp.concatenate([selb] * GQA, axis=1)
    qa_sel = jnp.concatenate([aug_query(self_score(ks_ref)), sel_rows], axis=0)

    def sel_scores(kt):
        lhs = jnp.concatenate([tile(ks_ref, kt), tile(oh_ref, kt)], axis=1)
        return jnp.dot(lhs, qa_sel, preferred_element_type=F32)

    def sel_group(base, n, masked, start):
        kts = [base + u for u in range(n)]
        masks = [causal(kt) if masked else None for kt in kts]
        plain_tiles(sel_scores, vst_ref, acc_s, kts, masks, start)

    n_loop = jnp.maximum(kt_last // UNROLL - 1, 0)
    base = n_loop * UNROLL
    for v in range(2 * UNROLL // TILES_Q):
        @pl.when((kt_last - base) // TILES_Q == v)
        def _(v=v):
            kts = [base + u for u in range(TILES_Q * (v + 1))]
            masks = [None] * (TILES_Q * v) + [causal(kt) for kt in kts[TILES_Q * v:]]
            plain_tiles(sel_scores, vst_ref, acc_s, kts, masks, True)

    def sel_body(grp, carry):
        sel_group(grp * UNROLL, UNROLL, False, False)
        return carry

    lax.fori_loop(0, n_loop, sel_body, 0)
    o_slc = finish(acc_s)

    def gate(branch):
        return jnp.concatenate(
            [gnt_ref[branch * GQA + r:branch * GQA + r + 1, :] for r in range(GQA)], axis=1)

    def emit(o_slc, o_win):
        o = gate(0) * o_cmp + gate(1) * o_slc + gate(2) * o_win
        for h in range(GQA // 2):
            pair = jnp.concatenate([o[:, (2 * h) * QT:(2 * h + 1) * QT],
                                    o[:, (2 * h + 1) * QT:(2 * h + 2) * QT]], axis=0)
            for c in range(TILES_Q):
                o_ref[c * TQ:(c + 1) * TQ, h * LANES:(h + 1) * LANES] = (
                    pair[:, c * TQ:(c + 1) * TQ].T.astype(BF16))

    emit(o_slc, o_win)

    bad = jnp.max(jnp.where(jnp.isfinite(o_slc) & jnp.isfinite(o_win), 0.0, 1.0))

    @pl.when(bad > 0.0)
    def _():
        m_run = online_tile(sel_scores, vst_ref, acc_s, kt_last, causal(kt_last), None)
        lax.fori_loop(0, kt_last, lambda kt, m_in: online_tile(sel_scores, vst_ref, acc_s, kt,
                                                               causal(kt), m_in), m_run)
        m_run = None
        for d in list(range(TILES_Q - 1, n_wt)) + list(range(TILES_Q - 1)):
            pos = (kt_last - d) * TQ + pos_col
            inside = (tq >= pos) & (tq - pos < WINDOW) & (pos >= 0)
            m_run = online_tile(win_scores, vwt_ref, acc_w, jnp.maximum(kt_last - d, 0), inside,
                                m_run)
        emit(finish(acc_s), finish(acc_w))


def _attn_call(qt, kc, vct, ks, vst, kw, vwt, gnt, ovl, onehot, cband, eye, B, S):
    T = B * S
    nq = S // QT
    ncp = S // CMP_STRIDE
    ns = S // SLC_BLOCK
    bg4 = lambda b, g, i: (b, g, 0, 0)
    bg5 = lambda b, g, i: (b, g, 0, 0, 0)
    k_spec = pl.BlockSpec((None, None, S, LANES), bg4)
    vt_spec = pl.BlockSpec((None, None, S // TQ, HEAD_DIM, TQ), bg5)
    return pl.pallas_call(
        _attn_kernel,
        grid=(B, N_KV, nq),
        in_specs=[
            pl.BlockSpec((None, GQA, HEAD_DIM, QT), lambda b, g, i: (b, g, 0, i)),
            pl.BlockSpec((None, None, ncp, LANES), bg4),
            pl.BlockSpec((None, None, HEAD_DIM, ncp), bg4),
            k_spec, vt_spec, k_spec, vt_spec,
            pl.BlockSpec((None, None, GROWS, QT), lambda b, g, i: (b, g, 0, i)),
            _const_spec((ns, ncp)),
            _const_spec((S, LANES)),
            _const_spec((2 * ncp, QT)),
            _const_spec((QT, NL)),
        ],
        out_specs=pl.BlockSpec((QT, GQA * HEAD_DIM), lambda b, g, i: (b * nq + i, g)),
        out_shape=jax.ShapeDtypeStruct((T, QW), BF16),
        scratch_shapes=[pltpu.VMEM((ns, QT), F32), pltpu.VMEM((ns, QT), jnp.int32),
                        pltpu.VMEM((HEAD_DIM, NL), F32), pltpu.VMEM((8, NL), F32),
                        pltpu.VMEM((HEAD_DIM, NL), F32), pltpu.VMEM((8, NL), F32)],
        compiler_params=pltpu.CompilerParams(
            dimension_semantics=("arbitrary", "arbitrary", "arbitrary"),
            vmem_limit_bytes=VMEM_LIMIT),
        name="attn",
    )(qt, kc, vct, ks, vst, kw, vwt, gnt, ovl, onehot, cband, eye)


FF_CHUNK = 512


def _mlp_kernel(x_ref, o_ref, sga_ref, ob_ref, wa_ref, wo_ref, g2_ref, wu_ref, wd_ref,
                out_ref, acc_scr):
    a = jnp.dot(o_ref[...], wa_ref[...], preferred_element_type=F32)
    mixed = sga_ref[...] * a + ob_ref[...]
    x1 = x_ref[...] + jnp.dot(mixed.astype(BF16), wo_ref[...], preferred_element_type=F32)
    h = _rms(x1, g2_ref[...]).astype(BF16)
    acc_scr[...] = x1
    for c in range(D_FF // FF_CHUNK):
        lo, hi = c * FF_CHUNK, (c + 1) * FF_CHUNK
        up = jnp.dot(h, wu_ref[:, lo:hi], preferred_element_type=F32)
        act = jnp.square(jnp.maximum(up, 0.0)).astype(BF16)
        acc_scr[...] += jnp.dot(act, wd_ref[lo:hi, :], preferred_element_type=F32)
    out_ref[...] = acc_scr[...]


def _mlp_call(x2, o_nsa, sga, ob, w_a, w_o, g2, w_up, w_down, tm):
    T = x2.shape[0]
    row = lambda i: (i, 0)
    return pl.pallas_call(
        _mlp_kernel,
        grid=(T // tm,),
        in_specs=[
            pl.BlockSpec((tm, D_MODEL), row),
            pl.BlockSpec((tm, QW), row),
            pl.BlockSpec((tm, D_MODEL), row),
            pl.BlockSpec((tm, D_MODEL), row),
            _const_spec((QW, D_MODEL)),
            _const_spec((D_MODEL, D_MODEL)),
            _const_spec((1, D_MODEL)),
            _const_spec((D_MODEL, D_FF)),
            _const_spec((D_FF, D_MODEL)),
        ],
        out_specs=pl.BlockSpec((tm, D_MODEL), row),
        out_shape=jax.ShapeDtypeStruct((T, D_MODEL), F32),
        scratch_shapes=[pltpu.VMEM((tm, D_MODEL), F32)],
        compiler_params=pltpu.CompilerParams(
            dimension_semantics=("arbitrary",), vmem_limit_bytes=VMEM_LIMIT),
        name="mlp",
    )(x2, o_nsa, sga, ob, w_a, w_o, g2, w_up, w_down)


def _pack_w_in(w_in):
    w_in = w_in.astype(BF16)
    o = OFF_GN + 3 * N_HEADS
    return jnp.concatenate([w_in[:, :o], jnp.zeros((D_MODEL, LANES - 3 * N_HEADS), BF16),
                            w_in[:, o:]], axis=1)


def _pack_cmp(w_cmp_k, w_cmp_v, cmp_pos_k, cmp_pos_v):
    wk = w_cmp_k.reshape(CMP_BLOCK, HEAD_DIM, HEAD_DIM)
    wv = w_cmp_v.reshape(CMP_BLOCK, HEAD_DIM, HEAD_DIM)
    slabs = [jnp.pad(blk.astype(BF16), ((0, 0), (0, 0), (col, CMP_OUT - col - HEAD_DIM)))
             for blk, col in ((wk, 0), (wk, LANES), (wv, 2 * LANES), (wv, 2 * LANES + HEAD_DIM))]
    pos4 = jnp.concatenate([cmp_pos_k, cmp_pos_k, cmp_pos_v, cmp_pos_v], axis=1)
    return jnp.concatenate(slabs, axis=1), pos4


def _pad_gain(g):
    return jnp.pad(g, ((0, 0), (0, LANES - HEAD_DIM)))


def _overlap(S):
    ncp = S // CMP_STRIDE
    ns = S // SLC_BLOCK
    c_start = np.arange(ncp) * CMP_STRIDE
    s_start = np.arange(ns) * SLC_BLOCK
    ov = np.clip(np.minimum(c_start[None, :] + CMP_BLOCK, s_start[:, None] + SLC_BLOCK)
                 - np.maximum(c_start[None, :], s_start[:, None]), 0, None)
    return jnp.asarray(ov.astype(np.float32) / CMP_BLOCK, dtype=BF16)


def _block_onehot(S):
    pos = np.arange(S)
    oh = np.zeros((S, LANES), np.float32)
    oh[pos, pos // SLC_BLOCK] = 1.0
    return jnp.asarray(oh, dtype=BF16)


def _cmp_band(S):
    ncp = S // CMP_STRIDE
    c_rel = np.arange(2 * ncp)[:, None] - ncp
    visible = c_rel * CMP_STRIDE + (CMP_BLOCK - 1) <= np.arange(QT)[None, :]
    return jnp.asarray(np.where(visible, 0.0, NEG_INF), dtype=F32)


def _tiled_eye():
    return jnp.asarray(np.tile(np.eye(QT, dtype=np.float32), (1, GQA)), dtype=BF16)


def _layer(x, norm1_g, w_in, q_norm_g, k_norm_g, cmp_pos_k, cmp_pos_v, w_cmp_k, w_cmp_v,
           conv_w, w_branch_a, w_branch_b, w_out, norm2_g, w_up, w_down):
    B, S, _ = x.shape
    assert S % (CMP_STRIDE * LANES) == 0 and S // SLC_BLOCK <= LANES
    assert (S // TQ) % UNROLL == 0 and S >= WINDOW
    x2 = x.reshape(B * S, D_MODEL)
    w_bd, pos4 = _pack_cmp(w_cmp_k, w_cmp_v, cmp_pos_k, cmp_pos_v)
    kg = _pad_gain(k_norm_g)
    qt, kci, vci, ks, vst, kw, vwt, gnt, sga, ob = _proj_call(
        x2, norm1_g[None, :], _pack_w_in(w_in), w_branch_b.astype(BF16),
        q_norm_g[:, None], kg, conv_w, B, S, tm=512)
    kc, vct = _cmp_call(kci, vci, pos4, w_bd, kg, B, S)
    o_nsa = _attn_call(qt, kc, vct, ks, vst, kw, vwt, gnt, _overlap(S), _block_onehot(S),
                       _cmp_band(S), _tiled_eye(), B, S)
    out = _mlp_call(x2, o_nsa, sga, ob, w_branch_a.astype(BF16), w_out.astype(BF16),
                    norm2_g[None, :], w_up.astype(BF16), w_down.astype(BF16), tm=512)
    return out.reshape(B, S, D_MODEL)


@jax.jit
def kernel(x, norm1_g, w_in, q_norm_g, k_norm_g, cmp_pos_k, cmp_pos_v, w_cmp_k, w_cmp_v,
           conv_w, w_branch_a, w_branch_b, w_out, norm2_g, w_up, w_down):
    for l in range(norm1_g.shape[0]):
        x = _layer(x, norm1_g[l], w_in[l], q_norm_g[l], k_norm_g[l], cmp_pos_k[l],
                   cmp_pos_v[l], w_cmp_k[l], w_cmp_v[l], conv_w[l], w_branch_a[l],
                   w_branch_b[l], w_out[l], norm2_g[l], w_up[l], w_down[l])
    return x
```

```python
import numpy as np
import jax
import jax.numpy as jnp
from jax import lax
from jax.experimental import pallas as pl
from jax.experimental.pallas import tpu as pltpu

D_MODEL = 1024
N_HEADS = 8
HEAD_DIM = 64
N_KV = 2
GQA = N_HEADS // N_KV
CMP_BLOCK = 32
CMP_STRIDE = 16
SLC_BLOCK = 64
N_SLC = 16
WINDOW = 512
FORCE_SCORE = 1e4
SCALE = 0.125
CONV_K = 3
D_FF = 4 * D_MODEL
EPS = 1e-6
NEG_INF = -1e30

LANES = 128
QW = N_HEADS * HEAD_DIM
KVW = N_KV * HEAD_DIM

OFF_Q = 0
OFF_KC = OFF_Q + QW
OFF_VC = OFF_KC + KVW
OFF_KS = OFF_VC + KVW
OFF_VS = OFF_KS + KVW
OFF_KW = OFF_VS + KVW
OFF_VW = OFF_KW + KVW
OFF_GN = OFF_VW + KVW
PA_COLS = OFF_GN + LANES
OFF_CONV_B = PA_COLS
OFF_CONV_C = OFF_CONV_B + D_MODEL
OFF_CONV_X = OFF_CONV_C + D_MODEL
OFF_GATE_A = OFF_CONV_X + D_MODEL
OFF_GATE_B = OFF_GATE_A + D_MODEL
W_IN_COLS = OFF_GATE_B + D_MODEL

PROJ_SUB = 256
TQ = 256
QT = 256
QT_SHIFT = QT.bit_length() - 1
TILES_Q = QT // TQ
NL = GQA * QT
UNROLL = 4
LOOKAHEAD = 1
GROWS = 16
CMP_OUT = 2 * LANES + KVW
VMEM_LIMIT = 56 * 1024 * 1024

F32 = jnp.float32
BF16 = jnp.bfloat16


def _rms(x, g):
    return x * lax.rsqrt(jnp.mean(x * x, axis=-1, keepdims=True) + EPS) * g


def _rms_padded(x, g):
    ms = jnp.sum(x * x, axis=-1, keepdims=True) * (1.0 / HEAD_DIM)
    return x * lax.rsqrt(ms + EPS) * g


def _alibi_key_lanes(pos):
    lane = lax.broadcasted_iota(jnp.int32, (pos.shape[0], LANES), 1)
    hi = (pos >> 6).astype(F32)
    lo = (pos & 63).astype(F32)
    return jnp.where((lane == HEAD_DIM) | (lane == HEAD_DIM + 1) | (lane == HEAD_DIM + 4), 1.0,
                     jnp.where(lane == HEAD_DIM + 2, hi,
                               jnp.where(lane == HEAD_DIM + 3, lo, 0.0)))


def _const_spec(shape):
    zeros = (0,) * len(shape)
    return pl.BlockSpec(shape, lambda *_: zeros, pipeline_mode=pl.Buffered(1))


def _proj_kernel(x_ref, g1_ref, w_ref, wb_ref, qg_ref, kg_ref, cw_ref,
                 qt_ref, kci_ref, vci_ref, ks_ref, vst_ref, kw_ref, vwt_ref, gnt_ref,
                 sga_ref, ob_ref, u_scr):
    i = pl.program_id(1)
    tm = x_ref.shape[0]

    @pl.when(i == 0)
    def _():
        u_scr[0:8, :] = jnp.zeros((8, D_MODEL), F32)

    for r0 in range(0, tm, PROJ_SUB):
        _proj_rows(r0, i * tm + r0, x_ref, g1_ref, w_ref, wb_ref, qg_ref, kg_ref, cw_ref,
                   qt_ref, kci_ref, vci_ref, ks_ref, vst_ref, kw_ref, vwt_ref, gnt_ref,
                   sga_ref, ob_ref, u_scr)
    u_scr[0:8, :] = u_scr[tm:tm + 8, :]


def _proj_rows(r0, pos0, x_ref, g1_ref, w_ref, wb_ref, qg_ref, kg_ref, cw_ref,
               qt_ref, kci_ref, vci_ref, ks_ref, vst_ref, kw_ref, vwt_ref, gnt_ref,
               sga_ref, ob_ref, u_scr):
    n = PROJ_SUB
    rows = slice(r0, r0 + n)
    xn = _rms(x_ref[rows, :], g1_ref[...]).astype(BF16)

    pa = jnp.dot(xn, w_ref[:, 0:PA_COLS], preferred_element_type=F32)

    qg = qg_ref[...]
    for pair in range(N_HEADS // 2):
        qp = pa[:, OFF_Q + pair * LANES:OFF_Q + (pair + 1) * LANES].T
        for half in range(2):
            qh = qp[half * HEAD_DIM:(half + 1) * HEAD_DIM, :]
            ms = jnp.mean(qh * qh, axis=0, keepdims=True)
            qt_ref[2 * pair + half, :, rows] = (qh * lax.rsqrt(ms + EPS) * qg * SCALE).astype(BF16)

    kci_ref[rows, :] = pa[:, OFF_KC:OFF_KC + KVW]
    vci_ref[rows, :] = pa[:, OFF_VC:OFF_VC + KVW]

    pos = pos0 + lax.broadcasted_iota(jnp.int32, (n, 1), 0)
    key_lanes = _alibi_key_lanes(pos)
    low = lax.broadcasted_iota(jnp.int32, (n, LANES), 1) < HEAD_DIM
    ks2 = pa[:, OFF_KS:OFF_KS + KVW]
    kw2 = pa[:, OFF_KW:OFF_KW + KVW]
    vs_t = pa[:, OFF_VS:OFF_VS + KVW].T.astype(BF16)
    vw_t = pa[:, OFF_VW:OFF_VW + KVW].T.astype(BF16)
    gates_t = jax.nn.sigmoid(pa[:, OFF_GN:OFF_GN + LANES]).T
    for g in range(N_KV):
        ks = jnp.where(low, ks2 if g == 0 else pltpu.roll(ks2, HEAD_DIM, axis=1), 0.0)
        kw = jnp.where(low, kw2 if g == 0 else pltpu.roll(kw2, HEAD_DIM, axis=1), 0.0)
        ks_ref[g, rows, :] = (_rms_padded(ks, kg_ref[1:2, :]) + key_lanes).astype(BF16)
        kw_ref[g, rows, :] = (_rms_padded(kw, kg_ref[2:3, :]) + key_lanes).astype(BF16)
        for c in range(n // TQ):
            ct = r0 // TQ + c
            vst_ref[g, ct] = vs_t[g * HEAD_DIM:(g + 1) * HEAD_DIM, c * TQ:(c + 1) * TQ]
            vwt_ref[g, ct] = vw_t[g * HEAD_DIM:(g + 1) * HEAD_DIM, c * TQ:(c + 1) * TQ]
        gnt_ref[g, :, rows] = gates_t[g * HEAD_DIM:g * HEAD_DIM + GROWS, :]

    def col(o):
        return jnp.dot(xn, w_ref[:, o:o + D_MODEL], preferred_element_type=F32)

    u = col(OFF_CONV_C) * col(OFF_CONV_X)
    u_scr[8 + r0:8 + r0 + n, :] = u
    cw = cw_ref[...]
    y = (cw[2:3, :] * u + cw[1:2, :] * u_scr[7 + r0:7 + r0 + n, :]
         + cw[0:1, :] * u_scr[6 + r0:6 + r0 + n, :])
    z = col(OFF_CONV_B) * y
    zb = jnp.dot(z.astype(BF16), wb_ref[...], preferred_element_type=F32)
    ob_ref[rows, :] = jax.nn.sigmoid(col(OFF_GATE_B)) * zb
    sga_ref[rows, :] = jax.nn.sigmoid(col(OFF_GATE_A))


def _proj_call(x2, g1, w_in_p, w_b, qg, kg, cw, B, S, tm):
    T = B * S
    nt = S // tm
    row = lambda b, i: (b * nt + i, 0)
    k_shape = jax.ShapeDtypeStruct((B, N_KV, S, LANES), BF16)
    k_spec = pl.BlockSpec((None, N_KV, tm, LANES), lambda b, i: (b, 0, i, 0))
    vt_shape = jax.ShapeDtypeStruct((B, N_KV, S // TQ, HEAD_DIM, TQ), BF16)
    vt_spec = pl.BlockSpec((None, N_KV, tm // TQ, HEAD_DIM, TQ), lambda b, i: (b, 0, i, 0, 0))
    return pl.pallas_call(
        _proj_kernel,
        grid=(B, nt),
        in_specs=[
            pl.BlockSpec((tm, D_MODEL), row),
            _const_spec((1, D_MODEL)),
            _const_spec((D_MODEL, W_IN_COLS)),
            _const_spec((D_MODEL, D_MODEL)),
            _const_spec((HEAD_DIM, 1)),
            _const_spec((3, LANES)),
            _const_spec((CONV_K, D_MODEL)),
        ],
        out_specs=[
            pl.BlockSpec((None, N_HEADS, HEAD_DIM, tm), lambda b, i: (b, 0, 0, i)),
            pl.BlockSpec((tm, KVW), row),
            pl.BlockSpec((tm, KVW), row),
            k_spec, vt_spec, k_spec, vt_spec,
            pl.BlockSpec((None, N_KV, GROWS, tm), lambda b, i: (b, 0, 0, i)),
            pl.BlockSpec((tm, D_MODEL), row),
            pl.BlockSpec((tm, D_MODEL), row),
        ],
        out_shape=[
            jax.ShapeDtypeStruct((B, N_HEADS, HEAD_DIM, S), BF16),
            jax.ShapeDtypeStruct((T, KVW), F32),
            jax.ShapeDtypeStruct((T, KVW), F32),
            k_shape, vt_shape, k_shape, vt_shape,
            jax.ShapeDtypeStruct((B, N_KV, GROWS, S), F32),
            jax.ShapeDtypeStruct((T, D_MODEL), F32),
            jax.ShapeDtypeStruct((T, D_MODEL), F32),
        ],
        scratch_shapes=[pltpu.VMEM((tm + 8, D_MODEL), F32)],
        compiler_params=pltpu.CompilerParams(
            dimension_semantics=("arbitrary", "arbitrary"),
            vmem_limit_bytes=VMEM_LIMIT),
        name="proj",
    )(x2, g1, w_in_p, w_b, qg, kg, cw)


def _cmp_kernel(kci_ref, vci_ref, pos_ref, w_ref, kg_ref, kc_ref, vct_ref, b_scr):
    ncp = kc_ref.shape[1]
    acc_a = jnp.zeros((ncp, CMP_OUT), F32)
    acc_b = jnp.zeros((ncp, CMP_OUT), F32)
    for l in range(CMP_STRIDE):
        rows = jnp.concatenate([kci_ref[pl.ds(l, ncp, stride=CMP_STRIDE), :],
                                vci_ref[pl.ds(l, ncp, stride=CMP_STRIDE), :]], axis=1)
        xa = (rows + pos_ref[l:l + 1, :]).astype(BF16)
        xb = (rows + pos_ref[CMP_STRIDE + l:CMP_STRIDE + l + 1, :]).astype(BF16)
        acc_a = acc_a + jnp.dot(xa, w_ref[l], preferred_element_type=F32)
        acc_b = acc_b + jnp.dot(xb, w_ref[CMP_STRIDE + l], preferred_element_type=F32)
    b_scr[0:ncp, :] = acc_b
    b_scr[ncp:ncp + 8, :] = jnp.zeros((8, CMP_OUT), F32)
    kcv = acc_a + b_scr[1:ncp + 1, :]
    c_end = lax.broadcasted_iota(jnp.int32, (ncp, 1), 0) * CMP_STRIDE + (CMP_BLOCK - 1)
    key_lanes = _alibi_key_lanes(c_end)
    vc_t = kcv[:, 2 * LANES:CMP_OUT].T.astype(BF16)
    for g in range(N_KV):
        kc = _rms_padded(kcv[:, g * LANES:(g + 1) * LANES], kg_ref[0:1, :])
        kc_ref[g] = (kc + key_lanes).astype(BF16)
        vct_ref[g] = vc_t[g * HEAD_DIM:(g + 1) * HEAD_DIM, :]


def _cmp_call(kci, vci, pos4, w_bd, kg, B, S):
    ncp = S // CMP_STRIDE
    return pl.pallas_call(
        _cmp_kernel,
        grid=(B,),
        in_specs=[
            pl.BlockSpec((S, KVW), lambda b: (b, 0)),
            pl.BlockSpec((S, KVW), lambda b: (b, 0)),
            _const_spec((CMP_BLOCK, 2 * KVW)),
            _const_spec((CMP_BLOCK, 2 * KVW, CMP_OUT)),
            _const_spec((3, LANES)),
        ],
        out_specs=[
            pl.BlockSpec((None, N_KV, ncp, LANES), lambda b: (b, 0, 0, 0)),
            pl.BlockSpec((None, N_KV, HEAD_DIM, ncp), lambda b: (b, 0, 0, 0)),
        ],
        out_shape=[
            jax.ShapeDtypeStruct((B, N_KV, ncp, LANES), BF16),
            jax.ShapeDtypeStruct((B, N_KV, HEAD_DIM, ncp), BF16),
        ],
        scratch_shapes=[pltpu.VMEM((ncp + 8, CMP_OUT), F32)],
        compiler_params=pltpu.CompilerParams(
            dimension_semantics=("arbitrary",), vmem_limit_bytes=VMEM_LIMIT),
        name="compress",
    )(kci, vci, pos4, w_bd, kg)


def _attn_kernel(qt_ref, kc_ref, vct_ref, ks_ref, vst_ref, kw_ref, vwt_ref, gnt_ref,
                 ovl_ref, oh_ref, cband_ref, eye_ref, o_ref, score_scr, cnt_scr,
                 num_s, den_s, num_w, den_w):
    acc_s = (num_s, den_s)
    acc_w = (num_w, den_w)
    g = pl.program_id(1)
    i = pl.program_id(2)
    t0 = i * QT
    kt_last = TILES_Q * (i + 1) - 1
    ns = ovl_ref.shape[0]
    ncp = ovl_ref.shape[1]

    lane = lax.broadcasted_iota(jnp.int32, (1, NL), 1)
    tq = t0 + (lane & (QT - 1))
    head = g * GQA + (lane >> QT_SHIFT) + 1
    slope = lax.bitcast_convert_type((127 - head) << 23, F32)
    a_t = (tq >> 6).astype(F32)
    b_t = (tq & 63).astype(F32)
    sub = lax.broadcasted_iota(jnp.int32, (HEAD_DIM, NL), 0)
    pos_col = lax.broadcasted_iota(jnp.int32, (TQ, 1), 0)
    qt = jnp.concatenate([qt_ref[r] for r in range(GQA)], axis=1)

    def aug_query(ref):
        rows = jnp.where(sub == 0, -(slope * 64.0) * a_t,
                         jnp.where(sub == 1, -slope * b_t,
                                   jnp.where(sub == 2, slope * 64.0,
                                             jnp.where(sub == 3, slope,
                                                       jnp.where(sub == 4, -ref, 0.0)))))
        return jnp.concatenate([qt, rows.astype(BF16)], axis=0)

    def tile(k_ref, kt):
        return k_ref[pl.ds(pl.multiple_of(kt * TQ, TQ), TQ), :]

    def causal(kt):
        return tq >= (kt * TQ + pos_col)

    def self_score(k_ref):
        k_t = jnp.concatenate([tile(k_ref, TILES_Q * i + c).astype(F32).T[0:HEAD_DIM, :]
                               for c in range(TILES_Q)], axis=1)
        return jnp.sum(qt.astype(F32) * jnp.concatenate([k_t] * GQA, axis=1),
                       axis=0, keepdims=True)

    def fold8(p):
        out = p[0:8, :]
        for k in range(1, TQ // 8):
            out = out + p[8 * k:8 * k + 8, :]
        return out

    def plain_tiles(score_fn, vt_ref, state, kts, masks, start, hook=None, lookahead=LOOKAHEAD):
        acc_ref, den_ref = state
        ahead = [score_fn(kt) for kt in kts[:lookahead]]
        if hook is not None:
            hook()
        total = None if start else acc_ref[...]
        den = None if start else den_ref[...]
        for n, (kt, mk) in enumerate(zip(kts, masks)):
            sc = ahead.pop(0)
            if n + lookahead < len(kts):
                ahead.append(score_fn(kts[n + lookahead]))
            if mk is not None:
                sc = jnp.where(mk, sc, NEG_INF)
            pr = jnp.exp(sc)
            pv = jnp.dot(vt_ref[kt], pr.astype(BF16), preferred_element_type=F32)
            total = pv if total is None else total + pv
            den = fold8(pr) if den is None else den + fold8(pr)
        acc_ref[...] = total
        den_ref[...] = den

    def online_tile(score_fn, vt_ref, state, kt, mask, m_run):
        acc_ref, den_ref = state
        sc = jnp.where(mask, score_fn(kt), NEG_INF)
        mx = jnp.max(sc, axis=0, keepdims=True)
        m_new = mx if m_run is None else jnp.maximum(m_run, mx)
        pr = jnp.exp(sc - m_new)
        pv = jnp.dot(vt_ref[kt], pr.astype(BF16), preferred_element_type=F32)
        if m_run is None:
            acc_ref[...] = pv
            den_ref[...] = fold8(pr)
        else:
            alpha = jnp.exp(m_run - m_new)
            acc_ref[...] = alpha * acc_ref[...] + pv
            den_ref[...] = alpha * den_ref[...] + fold8(pr)
        return m_new

    def finish(state):
        acc_ref, den_ref = state
        den = jnp.sum(den_ref[...], axis=0, keepdims=True)
        return acc_ref[...] / jnp.maximum(den, 1e-30)

    band = cband_ref[pl.ds(pl.multiple_of(ncp - (QT // CMP_STRIDE) * i, 8), ncp), :]
    lhs_c = jnp.concatenate([kc_ref[...], band.astype(BF16)], axis=1)
    rhs_c = jnp.concatenate([aug_query(0.0), eye_ref[...]], axis=0)
    has_cmp = tq >= CMP_BLOCK - 1
    qa_win = aug_query(self_score(kw_ref))

    def win_scores(kt):
        return jnp.dot(tile(kw_ref, kt), qa_win, preferred_element_type=F32)

    def cmp_and_window(win_kts, win_masks):
        s = jnp.dot(lhs_c, rhs_c, preferred_element_type=F32)
        out = []

        def cmp_rest():
            e = jnp.exp(s - jnp.max(s, axis=0, keepdims=True))
            den = jnp.maximum(jnp.sum(e, axis=0, keepdims=True), 1e-30)
            p = (e * jnp.where(has_cmp, 1.0 / den, 0.0)).astype(BF16)
            out.append(jnp.dot(vct_ref[...], p, preferred_element_type=F32))
            out.append(jnp.dot(ovl_ref[...], p, preferred_element_type=F32))

        plain_tiles(win_scores, vwt_ref, acc_w, win_kts, win_masks, True, hook=cmp_rest)
        return out[0], out[1]

    n_win = WINDOW // TQ
    n_wt = n_win + TILES_Q

    def win_steady():
        kts = [kt_last - d for d in range(n_wt)]
        masks = [causal(kt) for kt in kts[:TILES_Q]] + [None] * (n_win - TILES_Q)
        masks += [(tq - (kt * TQ + pos_col)) < WINDOW for kt in kts[n_win:]]
        return cmp_and_window(kts, masks)

    def win_start():
        kts = list(range(n_wt - 1))
        return cmp_and_window(kts, [causal(kt) for kt in kts])

    o_cmp, imp4 = lax.cond(kt_last >= n_wt - 1, win_steady, win_start)
    o_win = finish(acc_w)

    imp = imp4[:, 0:QT]
    for r in range(1, GQA):
        imp = imp + imp4[:, r * QT:(r + 1) * QT]
    j = lax.broadcasted_iota(jnp.int32, (ns, QT), 0)
    cur = (t0 + lax.broadcasted_iota(jnp.int32, (ns, QT), 1)) >> 6
    forced = (j == 0) | (j == cur) | (j == cur - 1)
    score = jnp.where(forced, FORCE_SCORE, jnp.where(j <= cur, imp, -1.0))
    score_scr[...] = score
    cnt_scr[...] = jnp.zeros((ns, QT), jnp.int32)

    row8 = lax.broadcasted_iota(jnp.int32, (8, QT), 0)
    for kg in range(ns // 8):
        @pl.when(8 * kg <= (TQ // SLC_BLOCK) * (kt_last + 1) - 1)
        def _(kg=kg):
            cnt = [cnt_scr[8 * v:8 * v + 8, :] for v in range(ns // 8)]
            for b in range(8 * kg, 8 * kg + 8):
                sb = score_scr[b:b + 1, :]
                for v in range(ns // 8):
                    sc_v = score[8 * v:8 * v + 8, :]
                    if v < kg:
                        ahead = sb > sc_v
                    elif v > kg:
                        ahead = sb >= sc_v
                    else:
                        ahead = (sb > sc_v) | ((sb == sc_v) & (row8 > b - 8 * kg))
                    cnt[v] = cnt[v] + jnp.where(ahead, 1, 0)
            for v in range(ns // 8):
                cnt_scr[8 * v:8 * v + 8, :] = cnt[v]

    sel = (cnt_scr[...] < N_SLC) & (j <= cur)
    selb = jnp.where(sel, 0.0, NEG_INF).astype(BF16)
    selb = jnp.concatenate([selb, jnp.zeros((LANES - ns, QT), BF16)], axis=0)

    sel_rows = jnp.concatenate([selb] * GQA, axis=1)
    qa_sel = jnp.concatenate([aug_query(self_score(ks_ref)), sel_rows], axis=0)

    def sel_scores(kt):
        lhs = jnp.concatenate([tile(ks_ref, kt), tile(oh_ref, kt)], axis=1)
        return jnp.dot(lhs, qa_sel, preferred_element_type=F32)

    def sel_group(base, n, masked, start):
        kts = [base + u for u in range(n)]
        masks = [causal(kt) if masked else None for kt in kts]
        plain_tiles(sel_scores, vst_ref, acc_s, kts, masks, start)

    n_loop = jnp.maximum(kt_last // UNROLL - 1, 0)
    base = n_loop * UNROLL
    for v in range(2 * UNROLL // TILES_Q):
        @pl.when((kt_last - base) // TILES_Q == v)
        def _(v=v):
            kts = [base + u for u in range(TILES_Q * (v + 1))]
            masks = [None] * (TILES_Q * v) + [causal(kt) for kt in kts[TILES_Q * v:]]
            plain_tiles(sel_scores, vst_ref, acc_s, kts, masks, True)

    def sel_body(grp, carry):
        sel_group(grp * UNROLL, UNROLL, False, False)
        return carry

    lax.fori_loop(0, n_loop, sel_body, 0)
    o_slc = finish(acc_s)

    def gate(branch):
        return jnp.concatenate(
            [gnt_ref[branch * GQA + r:branch * GQA + r + 1, :] for r in range(GQA)], axis=1)

    def emit(o_slc, o_win):
        o = gate(0) * o_cmp + gate(1) * o_slc + gate(2) * o_win
        for h in range(GQA // 2):
            pair = jnp.concatenate([o[:, (2 * h) * QT:(2 * h + 1) * QT],
                                    o[:, (2 * h + 1) * QT:(2 * h + 2) * QT]], axis=0)
            for c in range(TILES_Q):
                o_ref[c * TQ:(c + 1) * TQ, h * LANES:(h + 1) * LANES] = (
                    pair[:, c * TQ:(c + 1) * TQ].T.astype(BF16))
        return o

    bad = jnp.max(jnp.where(jnp.isfinite(emit(o_slc, o_win)), 0.0, 1.0))

    @pl.when(bad > 0.0)
    def _():
        m_run = online_tile(sel_scores, vst_ref, acc_s, kt_last, causal(kt_last), None)
        lax.fori_loop(0, kt_last, lambda kt, m_in: online_tile(sel_scores, vst_ref, acc_s, kt,
                                                               causal(kt), m_in), m_run)
        m_run = None
        for d in list(range(TILES_Q - 1, n_wt)) + list(range(TILES_Q - 1)):
            pos = (kt_last - d) * TQ + pos_col
            inside = (tq >= pos) & (tq - pos < WINDOW) & (pos >= 0)
            m_run = online_tile(win_scores, vwt_ref, acc_w, jnp.maximum(kt_last - d, 0), inside,
                                m_run)
        emit(finish(acc_s), finish(acc_w))


def _attn_call(qt, kc, vct, ks, vst, kw, vwt, gnt, ovl, onehot, cband, eye, B, S):
    T = B * S
    nq = S // QT
    ncp = S // CMP_STRIDE
    ns = S // SLC_BLOCK
    bg4 = lambda b, g, i: (b, g, 0, 0)
    bg5 = lambda b, g, i: (b, g, 0, 0, 0)
    k_spec = pl.BlockSpec((None, None, S, LANES), bg4)
    vt_spec = pl.BlockSpec((None, None, S // TQ, HEAD_DIM, TQ), bg5)
    return pl.pallas_call(
        _attn_kernel,
        grid=(B, N_KV, nq),
        in_specs=[
            pl.BlockSpec((None, GQA, HEAD_DIM, QT), lambda b, g, i: (b, g, 0, i)),
            pl.BlockSpec((None, None, ncp, LANES), bg4),
            pl.BlockSpec((None, None, HEAD_DIM, ncp), bg4),
            k_spec, vt_spec, k_spec, vt_spec,
            pl.BlockSpec((None, None, GROWS, QT), lambda b, g, i: (b, g, 0, i)),
            _const_spec((ns, ncp)),
            _const_spec((S, LANES)),
            _const_spec((2 * ncp, QT)),
            _const_spec((QT, NL)),
        ],
        out_specs=pl.BlockSpec((QT, GQA * HEAD_DIM), lambda b, g, i: (b * nq + i, g)),
        out_shape=jax.ShapeDtypeStruct((T, QW), BF16),
        scratch_shapes=[pltpu.VMEM((ns, QT), F32), pltpu.VMEM((ns, QT), jnp.int32),
                        pltpu.VMEM((HEAD_DIM, NL), F32), pltpu.VMEM((8, NL), F32),
                        pltpu.VMEM((HEAD_DIM, NL), F32), pltpu.VMEM((8, NL), F32)],
        compiler_params=pltpu.CompilerParams(
            dimension_semantics=("arbitrary", "arbitrary", "arbitrary"),
            vmem_limit_bytes=VMEM_LIMIT),
        name="attn",
    )(qt, kc, vct, ks, vst, kw, vwt, gnt, ovl, onehot, cband, eye)


FF_CHUNK = 512


def _mlp_kernel(x_ref, o_ref, sga_ref, ob_ref, wa_ref, wo_ref, g2_ref, wu_ref, wd_ref,
                out_ref, acc_scr):
    a = jnp.dot(o_ref[...], wa_ref[...], preferred_element_type=F32)
    mixed = sga_ref[...] * a + ob_ref[...]
    x1 = x_ref[...] + jnp.dot(mixed.astype(BF16), wo_ref[...], preferred_element_type=F32)
    h = _rms(x1, g2_ref[...]).astype(BF16)
    acc_scr[...] = x1
    for c in range(D_FF // FF_CHUNK):
        lo, hi = c * FF_CHUNK, (c + 1) * FF_CHUNK
        up = jnp.dot(h, wu_ref[:, lo:hi], preferred_element_type=F32)
        act = jnp.square(jnp.maximum(up, 0.0)).astype(BF16)
        acc_scr[...] += jnp.dot(act, wd_ref[lo:hi, :], preferred_element_type=F32)
    out_ref[...] = acc_scr[...]


def _mlp_call(x2, o_nsa, sga, ob, w_a, w_o, g2, w_up, w_down, tm):
    T = x2.shape[0]
    row = lambda i: (i, 0)
    return pl.pallas_call(
        _mlp_kernel,
        grid=(T // tm,),
        in_specs=[
            pl.BlockSpec((tm, D_MODEL), row),
            pl.BlockSpec((tm, QW), row),
            pl.BlockSpec((tm, D_MODEL), row),
            pl.BlockSpec((tm, D_MODEL), row),
            _const_spec((QW, D_MODEL)),
            _const_spec((D_MODEL, D_MODEL)),
            _const_spec((1, D_MODEL)),
            _const_spec((D_MODEL, D_FF)),
            _const_spec((D_FF, D_MODEL)),
        ],
        out_specs=pl.BlockSpec((tm, D_MODEL), row),
        out_shape=jax.ShapeDtypeStruct((T, D_MODEL), F32),
        scratch_shapes=[pltpu.VMEM((tm, D_MODEL), F32)],
        compiler_params=pltpu.CompilerParams(
            dimension_semantics=("arbitrary",), vmem_limit_bytes=VMEM_LIMIT),
        name="mlp",
    )(x2, o_nsa, sga, ob, w_a, w_o, g2, w_up, w_down)


def _pack_w_in(w_in):
    w_in = w_in.astype(BF16)
    o = OFF_GN
    gates = w_in[:, o:o + 3 * N_HEADS]
    gates = gates.reshape(D_MODEL, 3, N_KV, GQA).transpose(0, 2, 1, 3)
    gates = gates.reshape(D_MODEL, N_KV, 3 * GQA)
    gates = jnp.pad(gates, ((0, 0), (0, 0), (0, HEAD_DIM - 3 * GQA))).reshape(D_MODEL, LANES)
    return jnp.concatenate([w_in[:, :o], gates, w_in[:, o + 3 * N_HEADS:]], axis=1)


def _pack_cmp(w_cmp_k, w_cmp_v, cmp_pos_k, cmp_pos_v):
    wk = w_cmp_k.reshape(CMP_BLOCK, HEAD_DIM, HEAD_DIM)
    wv = w_cmp_v.reshape(CMP_BLOCK, HEAD_DIM, HEAD_DIM)
    slabs = [jnp.pad(blk.astype(BF16), ((0, 0), (0, 0), (col, CMP_OUT - col - HEAD_DIM)))
             for blk, col in ((wk, 0), (wk, LANES), (wv, 2 * LANES), (wv, 2 * LANES + HEAD_DIM))]
    pos4 = jnp.concatenate([cmp_pos_k, cmp_pos_k, cmp_pos_v, cmp_pos_v], axis=1)
    return jnp.concatenate(slabs, axis=1), pos4


def _pad_gain(g):
    return jnp.pad(g, ((0, 0), (0, LANES - HEAD_DIM)))


def _overlap(S):
    ncp = S // CMP_STRIDE
    ns = S // SLC_BLOCK
    c_start = np.arange(ncp) * CMP_STRIDE
    s_start = np.arange(ns) * SLC_BLOCK
    ov = np.clip(np.minimum(c_start[None, :] + CMP_BLOCK, s_start[:, None] + SLC_BLOCK)
                 - np.maximum(c_start[None, :], s_start[:, None]), 0, None)
    return jnp.asarray(ov.astype(np.float32) / CMP_BLOCK, dtype=BF16)


def _block_onehot(S):
    pos = np.arange(S)
    oh = np.zeros((S, LANES), np.float32)
    oh[pos, pos // SLC_BLOCK] = 1.0
    return jnp.asarray(oh, dtype=BF16)


def _cmp_band(S):
    ncp = S // CMP_STRIDE
    c_rel = np.arange(2 * ncp)[:, None] - ncp
    visible = c_rel * CMP_STRIDE + (CMP_BLOCK - 1) <= np.arange(QT)[None, :]
    return jnp.asarray(np.where(visible, 0.0, NEG_INF), dtype=F32)


def _tiled_eye():
    return jnp.asarray(np.tile(np.eye(QT, dtype=np.float32), (1, GQA)), dtype=BF16)


def _layer(x, norm1_g, w_in, q_norm_g, k_norm_g, cmp_pos_k, cmp_pos_v, w_cmp_k, w_cmp_v,
           conv_w, w_branch_a, w_branch_b, w_out, norm2_g, w_up, w_down):
    B, S, _ = x.shape
    assert S % (CMP_STRIDE * LANES) == 0 and S // SLC_BLOCK <= LANES
    assert (S // TQ) % UNROLL == 0 and S >= WINDOW
    x2 = x.reshape(B * S, D_MODEL)
    w_bd, pos4 = _pack_cmp(w_cmp_k, w_cmp_v, cmp_pos_k, cmp_pos_v)
    kg = _pad_gain(k_norm_g)
    qt, kci, vci, ks, vst, kw, vwt, gnt, sga, ob = _proj_call(
        x2, norm1_g[None, :], _pack_w_in(w_in), w_branch_b.astype(BF16),
        q_norm_g[:, None], kg, conv_w, B, S, tm=512)
    kc, vct = _cmp_call(kci, vci, pos4, w_bd, kg, B, S)
    o_nsa = _attn_call(qt, kc, vct, ks, vst, kw, vwt, gnt, _overlap(S), _block_onehot(S),
                       _cmp_band(S), _tiled_eye(), B, S)
    out = _mlp_call(x2, o_nsa, sga, ob, w_branch_a.astype(BF16), w_out.astype(BF16),
                    norm2_g[None, :], w_up.astype(BF16), w_down.astype(BF16), tm=512)
    return out.reshape(B, S, D_MODEL)


@jax.jit
def kernel(x, norm1_g, w_in, q_norm_g, k_norm_g, cmp_pos_k, cmp_pos_v, w_cmp_k, w_cmp_v,
           conv_w, w_branch_a, w_branch_b, w_out, norm2_g, w_up, w_down):
    for l in range(norm1_g.shape[0]):
        x = _layer(x, norm1_g[l], w_in[l], q_norm_g[l], k_norm_g[l], cmp_pos_k[l],
                   cmp_pos_v[l], w_cmp_k[l], w_cmp_v[l], conv_w[l], w_branch_a[l],
                   w_branch_b[l], w_out[l], norm2_g[l], w_up[l], w_down[l])
    return x
```

```python
import numpy as np
import jax
import jax.numpy as jnp
from jax import lax
from jax.experimental import pallas as pl
from jax.experimental.pallas import tpu as pltpu

D_MODEL = 1024
N_HEADS = 8
HEAD_DIM = 64
N_KV = 2
GQA = N_HEADS // N_KV
CMP_BLOCK = 32
CMP_STRIDE = 16
SLC_BLOCK = 64
N_SLC = 16
WINDOW = 512
FORCE_SCORE = 1e4
SCALE = 0.125
CONV_K = 3
D_FF = 4 * D_MODEL
EPS = 1e-6
NEG_INF = -1e30

LANES = 128
QW = N_HEADS * HEAD_DIM
KVW = N_KV * HEAD_DIM

OFF_Q = 0
OFF_KC = OFF_Q + QW
OFF_VC = OFF_KC + KVW
OFF_KS = OFF_VC + KVW
OFF_VS = OFF_KS + KVW
OFF_KW = OFF_VS + KVW
OFF_VW = OFF_KW + KVW
OFF_GN = OFF_VW + KVW
PA_COLS = OFF_GN + LANES
OFF_CONV_B = PA_COLS
OFF_CONV_C = OFF_CONV_B + D_MODEL
OFF_CONV_X = OFF_CONV_C + D_MODEL
OFF_GATE_A = OFF_CONV_X + D_MODEL
OFF_GATE_B = OFF_GATE_A + D_MODEL
W_IN_COLS = OFF_GATE_B + D_MODEL

PROJ_SUB = 256
TQ = 256
QT = 256
QT_SHIFT = QT.bit_length() - 1
TILES_Q = QT // TQ
NL = GQA * QT
UNROLL = 4
LOOKAHEAD = 1
GROWS = 16
CMP_OUT = 2 * LANES + KVW
VMEM_LIMIT = 56 * 1024 * 1024

F32 = jnp.float32
BF16 = jnp.bfloat16


def _rms(x, g):
    return x * lax.rsqrt(jnp.mean(x * x, axis=-1, keepdims=True) + EPS) * g


def _rms_padded(x, g):
    ms = jnp.sum(x * x, axis=-1, keepdims=True) * (1.0 / HEAD_DIM)
    return x * lax.rsqrt(ms + EPS) * g


def _alibi_key_lanes(pos):
    lane = lax.broadcasted_iota(jnp.int32, (pos.shape[0], LANES), 1)
    hi = (pos >> 6).astype(F32)
    lo = (pos & 63).astype(F32)
    return jnp.where((lane == HEAD_DIM) | (lane == HEAD_DIM + 1) | (lane == HEAD_DIM + 4), 1.0,
                     jnp.where(lane == HEAD_DIM + 2, hi,
                               jnp.where(lane == HEAD_DIM + 3, lo, 0.0)))


def _const_spec(shape):
    zeros = (0,) * len(shape)
    return pl.BlockSpec(shape, lambda *_: zeros, pipeline_mode=pl.Buffered(1))


def _proj_kernel(x_ref, g1_ref, w_ref, wb_ref, qg_ref, kg_ref, cw_ref,
                 qt_ref, kci_ref, vci_ref, ks_ref, vst_ref, kw_ref, vwt_ref, gnt_ref,
                 sga_ref, ob_ref, u_scr):
    i = pl.program_id(1)
    tm = x_ref.shape[0]

    @pl.when(i == 0)
    def _():
        u_scr[0:8, :] = jnp.zeros((8, D_MODEL), F32)

    for r0 in range(0, tm, PROJ_SUB):
        _proj_rows(r0, i * tm + r0, x_ref, g1_ref, w_ref, wb_ref, qg_ref, kg_ref, cw_ref,
                   qt_ref, kci_ref, vci_ref, ks_ref, vst_ref, kw_ref, vwt_ref, gnt_ref,
                   sga_ref, ob_ref, u_scr)
    u_scr[0:8, :] = u_scr[tm:tm + 8, :]


def _proj_rows(r0, pos0, x_ref, g1_ref, w_ref, wb_ref, qg_ref, kg_ref, cw_ref,
               qt_ref, kci_ref, vci_ref, ks_ref, vst_ref, kw_ref, vwt_ref, gnt_ref,
               sga_ref, ob_ref, u_scr):
    n = PROJ_SUB
    rows = slice(r0, r0 + n)
    xn = _rms(x_ref[rows, :], g1_ref[...]).astype(BF16)

    pa = jnp.dot(xn, w_ref[:, 0:PA_COLS], preferred_element_type=F32)

    qg = qg_ref[...]
    for pair in range(N_HEADS // 2):
        qp = pa[:, OFF_Q + pair * LANES:OFF_Q + (pair + 1) * LANES].T
        for half in range(2):
            qh = qp[half * HEAD_DIM:(half + 1) * HEAD_DIM, :]
            ms = jnp.mean(qh * qh, axis=0, keepdims=True)
            qt_ref[2 * pair + half, :, rows] = (qh * lax.rsqrt(ms + EPS) * qg * SCALE).astype(BF16)

    kci_ref[rows, :] = pa[:, OFF_KC:OFF_KC + KVW]
    vci_ref[rows, :] = pa[:, OFF_VC:OFF_VC + KVW]

    pos = pos0 + lax.broadcasted_iota(jnp.int32, (n, 1), 0)
    key_lanes = _alibi_key_lanes(pos)
    low = lax.broadcasted_iota(jnp.int32, (n, LANES), 1) < HEAD_DIM
    ks2 = pa[:, OFF_KS:OFF_KS + KVW]
    kw2 = pa[:, OFF_KW:OFF_KW + KVW]
    vs_t = pa[:, OFF_VS:OFF_VS + KVW].T.astype(BF16)
    vw_t = pa[:, OFF_VW:OFF_VW + KVW].T.astype(BF16)
    gates_t = jax.nn.sigmoid(pa[:, OFF_GN:OFF_GN + LANES]).T
    for g in range(N_KV):
        ks = jnp.where(low, ks2 if g == 0 else pltpu.roll(ks2, HEAD_DIM, axis=1), 0.0)
        kw = jnp.where(low, kw2 if g == 0 else pltpu.roll(kw2, HEAD_DIM, axis=1), 0.0)
        ks_ref[g, rows, :] = (_rms_padded(ks, kg_ref[1:2, :]) + key_lanes).astype(BF16)
        kw_ref[g, rows, :] = (_rms_padded(kw, kg_ref[2:3, :]) + key_lanes).astype(BF16)
        for c in range(n // TQ):
            ct = r0 // TQ + c
            vst_ref[g, ct] = vs_t[g * HEAD_DIM:(g + 1) * HEAD_DIM, c * TQ:(c + 1) * TQ]
            vwt_ref[g, ct] = vw_t[g * HEAD_DIM:(g + 1) * HEAD_DIM, c * TQ:(c + 1) * TQ]
        gnt_ref[g, :, rows] = gates_t[g * HEAD_DIM:g * HEAD_DIM + GROWS, :]

    def col(o):
        return jnp.dot(xn, w_ref[:, o:o + D_MODEL], preferred_element_type=F32)

    u = col(OFF_CONV_C) * col(OFF_CONV_X)
    u_scr[8 + r0:8 + r0 + n, :] = u
    cw = cw_ref[...]
    y = (cw[2:3, :] * u + cw[1:2, :] * u_scr[7 + r0:7 + r0 + n, :]
         + cw[0:1, :] * u_scr[6 + r0:6 + r0 + n, :])
    z = col(OFF_CONV_B) * y
    zb = jnp.dot(z.astype(BF16), wb_ref[...], preferred_element_type=F32)
    ob_ref[rows, :] = jax.nn.sigmoid(col(OFF_GATE_B)) * zb
    sga_ref[rows, :] = jax.nn.sigmoid(col(OFF_GATE_A))


def _proj_call(x2, g1, w_in_p, w_b, qg, kg, cw, B, S, tm):
    T = B * S
    nt = S // tm
    row = lambda b, i: (b * nt + i, 0)
    k_shape = jax.ShapeDtypeStruct((B, N_KV, S, LANES), BF16)
    k_spec = pl.BlockSpec((None, N_KV, tm, LANES), lambda b, i: (b, 0, i, 0))
    vt_shape = jax.ShapeDtypeStruct((B, N_KV, S // TQ, HEAD_DIM, TQ), BF16)
    vt_spec = pl.BlockSpec((None, N_KV, tm // TQ, HEAD_DIM, TQ), lambda b, i: (b, 0, i, 0, 0))
    return pl.pallas_call(
        _proj_kernel,
        grid=(B, nt),
        in_specs=[
            pl.BlockSpec((tm, D_MODEL), row),
            _const_spec((1, D_MODEL)),
            _const_spec((D_MODEL, W_IN_COLS)),
            _const_spec((D_MODEL, D_MODEL)),
            _const_spec((HEAD_DIM, 1)),
            _const_spec((3, LANES)),
            _const_spec((CONV_K, D_MODEL)),
        ],
        out_specs=[
            pl.BlockSpec((None, N_HEADS, HEAD_DIM, tm), lambda b, i: (b, 0, 0, i)),
            pl.BlockSpec((tm, KVW), row),
            pl.BlockSpec((tm, KVW), row),
            k_spec, vt_spec, k_spec, vt_spec,
            pl.BlockSpec((None, N_KV, GROWS, tm), lambda b, i: (b, 0, 0, i)),
            pl.BlockSpec((tm, D_MODEL), row),
            pl.BlockSpec((tm, D_MODEL), row),
        ],
        out_shape=[
            jax.ShapeDtypeStruct((B, N_HEADS, HEAD_DIM, S), BF16),
            jax.ShapeDtypeStruct((T, KVW), F32),
            jax.ShapeDtypeStruct((T, KVW), F32),
            k_shape, vt_shape, k_shape, vt_shape,
            jax.ShapeDtypeStruct((B, N_KV, GROWS, S), F32),
            jax.ShapeDtypeStruct((T, D_MODEL), F32),
            jax.ShapeDtypeStruct((T, D_MODEL), F32),
        ],
        scratch_shapes=[pltpu.VMEM((tm + 8, D_MODEL), F32)],
        compiler_params=pltpu.CompilerParams(
            dimension_semantics=("arbitrary", "arbitrary"),
            vmem_limit_bytes=VMEM_LIMIT),
        name="proj",
    )(x2, g1, w_in_p, w_b, qg, kg, cw)


def _cmp_kernel(kci_ref, vci_ref, pos_ref, w_ref, kg_ref, kc_ref, vct_ref, b_scr):
    ncp = kc_ref.shape[1]
    acc_a = jnp.zeros((ncp, CMP_OUT), F32)
    acc_b = jnp.zeros((ncp, CMP_OUT), F32)
    for l in range(CMP_STRIDE):
        rows = jnp.concatenate([kci_ref[pl.ds(l, ncp, stride=CMP_STRIDE), :],
                                vci_ref[pl.ds(l, ncp, stride=CMP_STRIDE), :]], axis=1)
        xa = (rows + pos_ref[l:l + 1, :]).astype(BF16)
        xb = (rows + pos_ref[CMP_STRIDE + l:CMP_STRIDE + l + 1, :]).astype(BF16)
        acc_a = acc_a + jnp.dot(xa, w_ref[l], preferred_element_type=F32)
        acc_b = acc_b + jnp.dot(xb, w_ref[CMP_STRIDE + l], preferred_element_type=F32)
    b_scr[0:ncp, :] = acc_b
    b_scr[ncp:ncp + 8, :] = jnp.zeros((8, CMP_OUT), F32)
    kcv = acc_a + b_scr[1:ncp + 1, :]
    c_end = lax.broadcasted_iota(jnp.int32, (ncp, 1), 0) * CMP_STRIDE + (CMP_BLOCK - 1)
    key_lanes = _alibi_key_lanes(c_end)
    vc_t = kcv[:, 2 * LANES:CMP_OUT].T.astype(BF16)
    for g in range(N_KV):
        kc = _rms_padded(kcv[:, g * LANES:(g + 1) * LANES], kg_ref[0:1, :])
        kc_ref[g] = (kc + key_lanes).astype(BF16)
        vct_ref[g] = vc_t[g * HEAD_DIM:(g + 1) * HEAD_DIM, :]


def _cmp_call(kci, vci, pos4, w_bd, kg, B, S):
    ncp = S // CMP_STRIDE
    return pl.pallas_call(
        _cmp_kernel,
        grid=(B,),
        in_specs=[
            pl.BlockSpec((S, KVW), lambda b: (b, 0)),
            pl.BlockSpec((S, KVW), lambda b: (b, 0)),
            _const_spec((CMP_BLOCK, 2 * KVW)),
            _const_spec((CMP_BLOCK, 2 * KVW, CMP_OUT)),
            _const_spec((3, LANES)),
        ],
        out_specs=[
            pl.BlockSpec((None, N_KV, ncp, LANES), lambda b: (b, 0, 0, 0)),
            pl.BlockSpec((None, N_KV, HEAD_DIM, ncp), lambda b: (b, 0, 0, 0)),
        ],
        out_shape=[
            jax.ShapeDtypeStruct((B, N_KV, ncp, LANES), BF16),
            jax.ShapeDtypeStruct((B, N_KV, HEAD_DIM, ncp), BF16),
        ],
        scratch_shapes=[pltpu.VMEM((ncp + 8, CMP_OUT), F32)],
        compiler_params=pltpu.CompilerParams(
            dimension_semantics=("arbitrary",), vmem_limit_bytes=VMEM_LIMIT),
        name="compress",
    )(kci, vci, pos4, w_bd, kg)


def _attn_kernel(qt_ref, kc_ref, vct_ref, ks_ref, vst_ref, kw_ref, vwt_ref, gnt_ref,
                 ovl_ref, oh_ref, cband_ref, eye_ref, o_ref, score_scr, cnt_scr,
                 num_s, den_s, num_w, den_w, tiles_smem):
    acc_s = (num_s, den_s)
    acc_w = (num_w, den_w)
    g = pl.program_id(1)
    i = pl.program_id(2)
    t0 = i * QT
    kt_last = TILES_Q * (i + 1) - 1
    ns = ovl_ref.shape[0]
    ncp = ovl_ref.shape[1]

    lane = lax.broadcasted_iota(jnp.int32, (1, NL), 1)
    tq = t0 + (lane & (QT - 1))
    head = g * GQA + (lane >> QT_SHIFT) + 1
    slope = lax.bitcast_convert_type((127 - head) << 23, F32)
    a_t = (tq >> 6).astype(F32)
    b_t = (tq & 63).astype(F32)
    sub = lax.broadcasted_iota(jnp.int32, (HEAD_DIM, NL), 0)
    pos_col = lax.broadcasted_iota(jnp.int32, (TQ, 1), 0)
    qt = jnp.concatenate([qt_ref[r] for r in range(GQA)], axis=1)

    def aug_query(ref):
        rows = jnp.where(sub == 0, -(slope * 64.0) * a_t,
                         jnp.where(sub == 1, -slope * b_t,
                                   jnp.where(sub == 2, slope * 64.0,
                                             jnp.where(sub == 3, slope,
                                                       jnp.where(sub == 4, -ref, 0.0)))))
        return jnp.concatenate([qt, rows.astype(BF16)], axis=0)

    def tile(k_ref, kt):
        return k_ref[pl.ds(pl.multiple_of(kt * TQ, TQ), TQ), :]

    def causal(kt):
        return tq >= (kt * TQ + pos_col)

    def self_score(k_ref):
        k_t = jnp.concatenate([tile(k_ref, TILES_Q * i + c).astype(F32).T[0:HEAD_DIM, :]
                               for c in range(TILES_Q)], axis=1)
        return jnp.sum(qt.astype(F32) * jnp.concatenate([k_t] * GQA, axis=1),
                       axis=0, keepdims=True)

    def fold8(p):
        out = p[0:8, :]
        for k in range(1, TQ // 8):
            out = out + p[8 * k:8 * k + 8, :]
        return out

    def plain_tiles(score_fn, vt_ref, state, kts, masks, start, hook=None, lookahead=LOOKAHEAD):
        acc_ref, den_ref = state
        ahead = [score_fn(kt) for kt in kts[:lookahead]]
        if hook is not None:
            hook()
        total = None if start else acc_ref[...]
        den = None if start else den_ref[...]
        for n, (kt, mk) in enumerate(zip(kts, masks)):
            sc = ahead.pop(0)
            if n + lookahead < len(kts):
                ahead.append(score_fn(kts[n + lookahead]))
            if mk is not None:
                sc = jnp.where(mk, sc, NEG_INF)
            pr = jnp.exp(sc)
            pv = jnp.dot(vt_ref[kt], pr.astype(BF16), preferred_element_type=F32)
            total = pv if total is None else total + pv
            den = fold8(pr) if den is None else den + fold8(pr)
        acc_ref[...] = total
        den_ref[...] = den

    def online_tile(score_fn, vt_ref, state, kt, mask, m_run):
        acc_ref, den_ref = state
        sc = jnp.where(mask, score_fn(kt), NEG_INF)
        mx = jnp.max(sc, axis=0, keepdims=True)
        m_new = mx if m_run is None else jnp.maximum(m_run, mx)
        pr = jnp.exp(sc - m_new)
        pv = jnp.dot(vt_ref[kt], pr.astype(BF16), preferred_element_type=F32)
        if m_run is None:
            acc_ref[...] = pv
            den_ref[...] = fold8(pr)
        else:
            alpha = jnp.exp(m_run - m_new)
            acc_ref[...] = alpha * acc_ref[...] + pv
            den_ref[...] = alpha * den_ref[...] + fold8(pr)
        return m_new

    def finish(state):
        acc_ref, den_ref = state
        den = jnp.sum(den_ref[...], axis=0, keepdims=True)
        return acc_ref[...] / jnp.maximum(den, 1e-30)

    band = cband_ref[pl.ds(pl.multiple_of(ncp - (QT // CMP_STRIDE) * i, 8), ncp), :]
    lhs_c = jnp.concatenate([kc_ref[...], band.astype(BF16)], axis=1)
    rhs_c = jnp.concatenate([aug_query(0.0), eye_ref[...]], axis=0)
    has_cmp = tq >= CMP_BLOCK - 1
    qa_win = aug_query(self_score(kw_ref))

    def win_scores(kt):
        return jnp.dot(tile(kw_ref, kt), qa_win, preferred_element_type=F32)

    def cmp_and_window(win_kts, win_masks):
        s = jnp.dot(lhs_c, rhs_c, preferred_element_type=F32)
        out = []

        def cmp_rest():
            e = jnp.exp(s - jnp.max(s, axis=0, keepdims=True))
            den = jnp.maximum(jnp.sum(e, axis=0, keepdims=True), 1e-30)
            p = (e * jnp.where(has_cmp, 1.0 / den, 0.0)).astype(BF16)
            out.append(jnp.dot(vct_ref[...], p, preferred_element_type=F32))
            out.append(jnp.dot(ovl_ref[...], p, preferred_element_type=F32))

        plain_tiles(win_scores, vwt_ref, acc_w, win_kts, win_masks, True, hook=cmp_rest)
        return out[0], out[1]

    n_win = WINDOW // TQ
    n_wt = n_win + TILES_Q

    def win_steady():
        kts = [kt_last - d for d in range(n_wt)]
        masks = [causal(kt) for kt in kts[:TILES_Q]] + [None] * (n_win - TILES_Q)
        masks += [(tq - (kt * TQ + pos_col)) < WINDOW for kt in kts[n_win:]]
        return cmp_and_window(kts, masks)

    def win_start():
        kts = list(range(n_wt - 1))
        return cmp_and_window(kts, [causal(kt) for kt in kts])

    o_cmp, imp4 = lax.cond(kt_last >= n_wt - 1, win_steady, win_start)
    o_win = finish(acc_w)

    imp = imp4[:, 0:QT]
    for r in range(1, GQA):
        imp = imp + imp4[:, r * QT:(r + 1) * QT]
    j = lax.broadcasted_iota(jnp.int32, (ns, QT), 0)
    cur = (t0 + lax.broadcasted_iota(jnp.int32, (ns, QT), 1)) >> 6
    forced = (j == 0) | (j == cur) | (j == cur - 1)
    score = jnp.where(forced, FORCE_SCORE, jnp.where(j <= cur, imp, -1.0))
    score_scr[...] = score
    cnt_scr[...] = jnp.zeros((ns, QT), jnp.int32)

    row8 = lax.broadcasted_iota(jnp.int32, (8, QT), 0)
    for kg in range(ns // 8):
        @pl.when(8 * kg <= (TQ // SLC_BLOCK) * (kt_last + 1) - 1)
        def _(kg=kg):
            cnt = [cnt_scr[8 * v:8 * v + 8, :] for v in range(ns // 8)]
            for b in range(8 * kg, 8 * kg + 8):
                sb = score_scr[b:b + 1, :]
                for v in range(ns // 8):
                    sc_v = score[8 * v:8 * v + 8, :]
                    if v < kg:
                        ahead = sb > sc_v
                    elif v > kg:
                        ahead = sb >= sc_v
                    else:
                        ahead = (sb > sc_v) | ((sb == sc_v) & (row8 > b - 8 * kg))
                    cnt[v] = cnt[v] + jnp.where(ahead, 1, 0)
            for v in range(ns // 8):
                cnt_scr[8 * v:8 * v + 8, :] = cnt[v]

    sel = (cnt_scr[...] < N_SLC) & (j <= cur)
    selb = jnp.where(sel, 0.0, NEG_INF).astype(BF16)
    selb = jnp.concatenate([selb, jnp.zeros((LANES - ns, QT), BF16)], axis=0)

    sel_rows = jnp.concatenate([selb] * GQA, axis=1)
    qa_sel = jnp.concatenate([aug_query(self_score(ks_ref)), sel_rows], axis=0)

    def sel_scores(kt):
        lhs = jnp.concatenate([tile(ks_ref, kt), tile(oh_ref, kt)], axis=1)
        return jnp.dot(lhs, qa_sel, preferred_element_type=F32)

    used = jnp.max(jnp.where(sel, 1.0, 0.0), axis=1, keepdims=True)
    row = lax.broadcasted_iota(jnp.int32, (ns, 1), 0)
    bits = jnp.where(used > 0.0, jnp.left_shift(1, row & 31), 0)
    words = [jnp.sum(bits[32 * w:32 * (w + 1), :]) for w in range(ns // 32)]
    blocks_per_tile = TQ // SLC_BLOCK
    n_act = jnp.int32(0)
    for kt in range(ns // blocks_per_tile - 1):
        field = (words[(kt * blocks_per_tile) // 32] >> ((kt * blocks_per_tile) % 32))
        active = ((field & (2 ** blocks_per_tile - 1)) != 0) & (kt < kt_last)
        tiles_smem[n_act] = kt
        n_act = n_act + active.astype(jnp.int32)
    tiles_smem[n_act] = kt_last

    n_loop = jnp.maximum(n_act // UNROLL - 1, 0)
    base = n_loop * UNROLL
    for v in range(2 * UNROLL):
        @pl.when(n_act - base == v)
        def _(v=v):
            kts = [tiles_smem[base + u] for u in range(v + 1)]
            plain_tiles(sel_scores, vst_ref, acc_s, kts, [None] * v + [causal(kts[v])], True)

    def sel_body(grp, carry):
        kts = [tiles_smem[grp * UNROLL + u] for u in range(UNROLL)]
        plain_tiles(sel_scores, vst_ref, acc_s, kts, [None] * UNROLL, False)
        return carry

    lax.fori_loop(0, n_loop, sel_body, 0)
    o_slc = finish(acc_s)

    def gate(branch):
        return jnp.concatenate(
            [gnt_ref[branch * GQA + r:branch * GQA + r + 1, :] for r in range(GQA)], axis=1)

    def emit(o_slc, o_win):
        o = gate(0) * o_cmp + gate(1) * o_slc + gate(2) * o_win
        for h in range(GQA // 2):
            pair = jnp.concatenate([o[:, (2 * h) * QT:(2 * h + 1) * QT],
                                    o[:, (2 * h + 1) * QT:(2 * h + 2) * QT]], axis=0)
            for c in range(TILES_Q):
                o_ref[c * TQ:(c + 1) * TQ, h * LANES:(h + 1) * LANES] = (
                    pair[:, c * TQ:(c + 1) * TQ].T.astype(BF16))
        return o

    bad = jnp.max(jnp.where(jnp.isfinite(emit(o_slc, o_win)), 0.0, 1.0))

    @pl.when(bad > 0.0)
    def _():
        m_run = online_tile(sel_scores, vst_ref, acc_s, kt_last, causal(kt_last), None)
        lax.fori_loop(0, kt_last, lambda kt, m_in: online_tile(sel_scores, vst_ref, acc_s, kt,
                                                               causal(kt), m_in), m_run)
        m_run = None
        for d in list(range(TILES_Q - 1, n_wt)) + list(range(TILES_Q - 1)):
            pos = (kt_last - d) * TQ + pos_col
            inside = (tq >= pos) & (tq - pos < WINDOW) & (pos >= 0)
            m_run = online_tile(win_scores, vwt_ref, acc_w, jnp.maximum(kt_last - d, 0), inside,
                                m_run)
        emit(finish(acc_s), finish(acc_w))


def _attn_call(qt, kc, vct, ks, vst, kw, vwt, gnt, ovl, onehot, cband, eye, B, S):
    T = B * S
    nq = S // QT
    ncp = S // CMP_STRIDE
    ns = S // SLC_BLOCK
    bg4 = lambda b, g, i: (b, g, 0, 0)
    bg5 = lambda b, g, i: (b, g, 0, 0, 0)
    k_spec = pl.BlockSpec((None, None, S, LANES), bg4)
    vt_spec = pl.BlockSpec((None, None, S // TQ, HEAD_DIM, TQ), bg5)
    return pl.pallas_call(
        _attn_kernel,
        grid=(B, N_KV, nq),
        in_specs=[
            pl.BlockSpec((None, GQA, HEAD_DIM, QT), lambda b, g, i: (b, g, 0, i)),
            pl.BlockSpec((None, None, ncp, LANES), bg4),
            pl.BlockSpec((None, None, HEAD_DIM, ncp), bg4),
            k_spec, vt_spec, k_spec, vt_spec,
            pl.BlockSpec((None, None, GROWS, QT), lambda b, g, i: (b, g, 0, i)),
            _const_spec((ns, ncp)),
            _const_spec((S, LANES)),
            _const_spec((2 * ncp, QT)),
            _const_spec((QT, NL)),
        ],
        out_specs=pl.BlockSpec((QT, GQA * HEAD_DIM), lambda b, g, i: (b * nq + i, g)),
        out_shape=jax.ShapeDtypeStruct((T, QW), BF16),
        scratch_shapes=[pltpu.VMEM((ns, QT), F32), pltpu.VMEM((ns, QT), jnp.int32),
                        pltpu.VMEM((HEAD_DIM, NL), F32), pltpu.VMEM((8, NL), F32),
                        pltpu.VMEM((HEAD_DIM, NL), F32), pltpu.VMEM((8, NL), F32),
                        pltpu.SMEM((S // TQ + 8,), jnp.int32)],
        compiler_params=pltpu.CompilerParams(
            dimension_semantics=("arbitrary", "arbitrary", "arbitrary"),
            vmem_limit_bytes=VMEM_LIMIT),
        name="attn",
    )(qt, kc, vct, ks, vst, kw, vwt, gnt, ovl, onehot, cband, eye)


FF_CHUNK = 512


def _mlp_kernel(x_ref, o_ref, sga_ref, ob_ref, wa_ref, wo_ref, g2_ref, wu_ref, wd_ref,
                out_ref, acc_scr):
    a = jnp.dot(o_ref[...], wa_ref[...], preferred_element_type=F32)
    mixed = sga_ref[...] * a + ob_ref[...]
    x1 = x_ref[...] + jnp.dot(mixed.astype(BF16), wo_ref[...], preferred_element_type=F32)
    h = _rms(x1, g2_ref[...]).astype(BF16)
    acc_scr[...] = x1
    for c in range(D_FF // FF_CHUNK):
        lo, hi = c * FF_CHUNK, (c + 1) * FF_CHUNK
        up = jnp.dot(h, wu_ref[:, lo:hi], preferred_element_type=F32)
        act = jnp.square(jnp.maximum(up, 0.0)).astype(BF16)
        acc_scr[...] += jnp.dot(act, wd_ref[lo:hi, :], preferred_element_type=F32)
    out_ref[...] = acc_scr[...]


def _mlp_call(x2, o_nsa, sga, ob, w_a, w_o, g2, w_up, w_down, tm):
    T = x2.shape[0]
    row = lambda i: (i, 0)
    return pl.pallas_call(
        _mlp_kernel,
        grid=(T // tm,),
        in_specs=[
            pl.BlockSpec((tm, D_MODEL), row),
            pl.BlockSpec((tm, QW), row),
            pl.BlockSpec((tm, D_MODEL), row),
            pl.BlockSpec((tm, D_MODEL), row),
            _const_spec((QW, D_MODEL)),
            _const_spec((D_MODEL, D_MODEL)),
            _const_spec((1, D_MODEL)),
            _const_spec((D_MODEL, D_FF)),
            _const_spec((D_FF, D_MODEL)),
        ],
        out_specs=pl.BlockSpec((tm, D_MODEL), row),
        out_shape=jax.ShapeDtypeStruct((T, D_MODEL), F32),
        scratch_shapes=[pltpu.VMEM((tm, D_MODEL), F32)],
        compiler_params=pltpu.CompilerParams(
            dimension_semantics=("arbitrary",), vmem_limit_bytes=VMEM_LIMIT),
        name="mlp",
    )(x2, o_nsa, sga, ob, w_a, w_o, g2, w_up, w_down)


def _pack_w_in(w_in):
    w_in = w_in.astype(BF16)
    o = OFF_GN
    gates = w_in[:, o:o + 3 * N_HEADS]
    gates = gates.reshape(D_MODEL, 3, N_KV, GQA).transpose(0, 2, 1, 3)
    gates = gates.reshape(D_MODEL, N_KV, 3 * GQA)
    gates = jnp.pad(gates, ((0, 0), (0, 0), (0, HEAD_DIM - 3 * GQA))).reshape(D_MODEL, LANES)
    return jnp.concatenate([w_in[:, :o], gates, w_in[:, o + 3 * N_HEADS:]], axis=1)


def _pack_cmp(w_cmp_k, w_cmp_v, cmp_pos_k, cmp_pos_v):
    wk = w_cmp_k.reshape(CMP_BLOCK, HEAD_DIM, HEAD_DIM)
    wv = w_cmp_v.reshape(CMP_BLOCK, HEAD_DIM, HEAD_DIM)
    slabs = [jnp.pad(blk.astype(BF16), ((0, 0), (0, 0), (col, CMP_OUT - col - HEAD_DIM)))
             for blk, col in ((wk, 0), (wk, LANES), (wv, 2 * LANES), (wv, 2 * LANES + HEAD_DIM))]
    pos4 = jnp.concatenate([cmp_pos_k, cmp_pos_k, cmp_pos_v, cmp_pos_v], axis=1)
    return jnp.concatenate(slabs, axis=1), pos4


def _pad_gain(g):
    return jnp.pad(g, ((0, 0), (0, LANES - HEAD_DIM)))


def _overlap(S):
    ncp = S // CMP_STRIDE
    ns = S // SLC_BLOCK
    c_start = np.arange(ncp) * CMP_STRIDE
    s_start = np.arange(ns) * SLC_BLOCK
    ov = np.clip(np.minimum(c_start[None, :] + CMP_BLOCK, s_start[:, None] + SLC_BLOCK)
                 - np.maximum(c_start[None, :], s_start[:, None]), 0, None)
    return jnp.asarray(ov.astype(np.float32) / CMP_BLOCK, dtype=BF16)


def _block_onehot(S):
    pos = np.arange(S)
    oh = np.zeros((S, LANES), np.float32)
    oh[pos, pos // SLC_BLOCK] = 1.0
    return jnp.asarray(oh, dtype=BF16)


def _cmp_band(S):
    ncp = S // CMP_STRIDE
    c_rel = np.arange(2 * ncp)[:, None] - ncp
    visible = c_rel * CMP_STRIDE + (CMP_BLOCK - 1) <= np.arange(QT)[None, :]
    return jnp.asarray(np.where(visible, 0.0, NEG_INF), dtype=F32)


def _tiled_eye():
    return jnp.asarray(np.tile(np.eye(QT, dtype=np.float32), (1, GQA)), dtype=BF16)


def _layer(x, norm1_g, w_in, q_norm_g, k_norm_g, cmp_pos_k, cmp_pos_v, w_cmp_k, w_cmp_v,
           conv_w, w_branch_a, w_branch_b, w_out, norm2_g, w_up, w_down):
    B, S, _ = x.shape
    assert S % (CMP_STRIDE * LANES) == 0 and S // SLC_BLOCK <= LANES
    assert (S // TQ) % UNROLL == 0 and S >= WINDOW
    assert TILES_Q == 1 and (S // SLC_BLOCK) % 32 == 0
    x2 = x.reshape(B * S, D_MODEL)
    w_bd, pos4 = _pack_cmp(w_cmp_k, w_cmp_v, cmp_pos_k, cmp_pos_v)
    kg = _pad_gain(k_norm_g)
    qt, kci, vci, ks, vst, kw, vwt, gnt, sga, ob = _proj_call(
        x2, norm1_g[None, :], _pack_w_in(w_in), w_branch_b.astype(BF16),
        q_norm_g[:, None], kg, conv_w, B, S, tm=512)
    kc, vct = _cmp_call(kci, vci, pos4, w_bd, kg, B, S)
    o_nsa = _attn_call(qt, kc, vct, ks, vst, kw, vwt, gnt, _overlap(S), _block_onehot(S),
                       _cmp_band(S), _tiled_eye(), B, S)
    out = _mlp_call(x2, o_nsa, sga, ob, w_branch_a.astype(BF16), w_out.astype(BF16),
                    norm2_g[None, :], w_up.astype(BF16), w_down.astype(BF16), tm=512)
    return out.reshape(B, S, D_MODEL)


@jax.jit
def kernel(x, norm1_g, w_in, q_norm_g, k_norm_g, cmp_pos_k, cmp_pos_v, w_cmp_k, w_cmp_v,
           conv_w, w_branch_a, w_branch_b, w_out, norm2_g, w_up, w_down):
    for l in range(norm1_g.shape[0]):
        x = _layer(x, norm1_g[l], w_in[l], q_norm_g[l], k_norm_g[l], cmp_pos_k[l],
                   cmp_pos_v[l], w_cmp_k[l], w_cmp_v[l], conv_w[l], w_branch_a[l],
                   w_branch_b[l], w_out[l], norm2_g[l], w_up[l], w_down[l])
    return x
```

```python
import numpy as np
import jax
import jax.numpy as jnp
from jax import lax
from jax.experimental import pallas as pl
from jax.experimental.pallas import tpu as pltpu

D_MODEL = 1024
N_HEADS = 8
HEAD_DIM = 64
N_KV = 2
GQA = N_HEADS // N_KV
CMP_BLOCK = 32
CMP_STRIDE = 16
SLC_BLOCK = 64
N_SLC = 16
WINDOW = 512
FORCE_SCORE = 1e4
SCALE = 0.125
CONV_K = 3
D_FF = 4 * D_MODEL
EPS = 1e-6
NEG_INF = -1e30

LANES = 128
QW = N_HEADS * HEAD_DIM
KVW = N_KV * HEAD_DIM

OFF_Q = 0
OFF_KC = OFF_Q + QW
OFF_VC = OFF_KC + KVW
OFF_KS = OFF_VC + KVW
OFF_VS = OFF_KS + KVW
OFF_KW = OFF_VS + KVW
OFF_VW = OFF_KW + KVW
OFF_GN = OFF_VW + KVW
PA_COLS = OFF_GN + LANES
OFF_CONV_B = PA_COLS
OFF_CONV_C = OFF_CONV_B + D_MODEL
OFF_CONV_X = OFF_CONV_C + D_MODEL
OFF_GATE_A = OFF_CONV_X + D_MODEL
OFF_GATE_B = OFF_GATE_A + D_MODEL
W_IN_COLS = OFF_GATE_B + D_MODEL

PROJ_SUB = 256
TQ = 256
QT = 256
QT_SHIFT = QT.bit_length() - 1
TILES_Q = QT // TQ
NL = GQA * QT
UNROLL = 4
LOOKAHEAD = 1
GROWS = 16
CMP_OUT = 2 * LANES + KVW
VMEM_LIMIT = 56 * 1024 * 1024

F32 = jnp.float32
BF16 = jnp.bfloat16


def _rms(x, g):
    return x * lax.rsqrt(jnp.mean(x * x, axis=-1, keepdims=True) + EPS) * g


def _rms_padded(x, g):
    ms = jnp.sum(x * x, axis=-1, keepdims=True) * (1.0 / HEAD_DIM)
    return x * lax.rsqrt(ms + EPS) * g


def _alibi_key_lanes(pos):
    lane = lax.broadcasted_iota(jnp.int32, (pos.shape[0], LANES), 1)
    hi = (pos >> 6).astype(F32)
    lo = (pos & 63).astype(F32)
    return jnp.where((lane == HEAD_DIM) | (lane == HEAD_DIM + 1) | (lane == HEAD_DIM + 4), 1.0,
                     jnp.where(lane == HEAD_DIM + 2, hi,
                               jnp.where(lane == HEAD_DIM + 3, lo, 0.0)))


def _const_spec(shape):
    zeros = (0,) * len(shape)
    return pl.BlockSpec(shape, lambda *_: zeros, pipeline_mode=pl.Buffered(1))


def _proj_kernel(x_ref, g1_ref, wh_ref, wt_ref, wb_ref, qg_ref, kg_ref, cw_ref,
                 qt_ref, kci_ref, vci_ref, ks_ref, vst_ref, kw_ref, vwt_ref, gnt_ref,
                 sga_ref, ob_ref, u_scr):
    i = pl.program_id(1)
    tm = x_ref.shape[0]

    @pl.when(i == 0)
    def _():
        u_scr[0:8, :] = jnp.zeros((8, D_MODEL), F32)

    for r0 in range(0, tm, PROJ_SUB):
        _proj_rows(r0, i * tm + r0, x_ref, g1_ref, wh_ref, wt_ref, wb_ref, qg_ref, kg_ref, cw_ref,
                   qt_ref, kci_ref, vci_ref, ks_ref, vst_ref, kw_ref, vwt_ref, gnt_ref,
                   sga_ref, ob_ref, u_scr)
    u_scr[0:8, :] = u_scr[tm:tm + 8, :]


def _proj_rows(r0, pos0, x_ref, g1_ref, wh_ref, wt_ref, wb_ref, qg_ref, kg_ref, cw_ref,
               qt_ref, kci_ref, vci_ref, ks_ref, vst_ref, kw_ref, vwt_ref, gnt_ref,
               sga_ref, ob_ref, u_scr):
    n = PROJ_SUB
    rows = slice(r0, r0 + n)
    xn = _rms(x_ref[rows, :], g1_ref[...]).astype(BF16)

    pa = jnp.dot(xn, wh_ref[...], preferred_element_type=F32)

    qg = qg_ref[...]
    for pair in range(N_HEADS // 2):
        qp = pa[:, OFF_Q + pair * LANES:OFF_Q + (pair + 1) * LANES].T
        for half in range(2):
            qh = qp[half * HEAD_DIM:(half + 1) * HEAD_DIM, :]
            ms = jnp.mean(qh * qh, axis=0, keepdims=True)
            qt_ref[2 * pair + half, :, rows] = (qh * lax.rsqrt(ms + EPS) * qg * SCALE).astype(BF16)

    kci_ref[rows, :] = pa[:, OFF_KC:OFF_KC + KVW]
    vci_ref[rows, :] = pa[:, OFF_VC:OFF_VC + KVW]

    pos = pos0 + lax.broadcasted_iota(jnp.int32, (n, 1), 0)
    key_lanes = _alibi_key_lanes(pos)
    low = lax.broadcasted_iota(jnp.int32, (n, LANES), 1) < HEAD_DIM
    ks2 = pa[:, OFF_KS:OFF_KS + KVW]
    kw2 = pa[:, OFF_KW:OFF_KW + KVW]
    vs_t = pa[:, OFF_VS:OFF_VS + KVW].T.astype(BF16)
    vw_t = pa[:, OFF_VW:OFF_VW + KVW].T.astype(BF16)
    gates_t = jax.nn.sigmoid(pa[:, OFF_GN:OFF_GN + LANES]).T
    for g in range(N_KV):
        ks = jnp.where(low, ks2 if g == 0 else pltpu.roll(ks2, HEAD_DIM, axis=1), 0.0)
        kw = jnp.where(low, kw2 if g == 0 else pltpu.roll(kw2, HEAD_DIM, axis=1), 0.0)
        ks_ref[g, rows, :] = (_rms_padded(ks, kg_ref[1:2, :]) + key_lanes).astype(BF16)
        kw_ref[g, rows, :] = (_rms_padded(kw, kg_ref[2:3, :]) + key_lanes).astype(BF16)
        for c in range(n // TQ):
            ct = r0 // TQ + c
            vst_ref[g, ct] = vs_t[g * HEAD_DIM:(g + 1) * HEAD_DIM, c * TQ:(c + 1) * TQ]
            vwt_ref[g, ct] = vw_t[g * HEAD_DIM:(g + 1) * HEAD_DIM, c * TQ:(c + 1) * TQ]
        gnt_ref[g, :, rows] = gates_t[g * HEAD_DIM:g * HEAD_DIM + GROWS, :]

    def col(o):
        return jnp.dot(xn, wt_ref[:, o - PA_COLS:o - PA_COLS + D_MODEL],
                       preferred_element_type=F32)

    u = col(OFF_CONV_C) * col(OFF_CONV_X)
    u_scr[8 + r0:8 + r0 + n, :] = u
    cw = cw_ref[...]
    y = (cw[2:3, :] * u + cw[1:2, :] * u_scr[7 + r0:7 + r0 + n, :]
         + cw[0:1, :] * u_scr[6 + r0:6 + r0 + n, :])
    z = col(OFF_CONV_B) * y
    zb = jnp.dot(z.astype(BF16), wb_ref[...], preferred_element_type=F32)
    ob_ref[rows, :] = jax.nn.sigmoid(col(OFF_GATE_B)) * zb
    sga_ref[rows, :] = jax.nn.sigmoid(col(OFF_GATE_A))


def _proj_call(x2, g1, w_head, w_tail, w_b, qg, kg, cw, B, S, tm):
    T = B * S
    nt = S // tm
    row = lambda b, i: (b * nt + i, 0)
    k_shape = jax.ShapeDtypeStruct((B, N_KV, S, LANES), BF16)
    k_spec = pl.BlockSpec((None, N_KV, tm, LANES), lambda b, i: (b, 0, i, 0))
    vt_shape = jax.ShapeDtypeStruct((B, N_KV, S // TQ, HEAD_DIM, TQ), BF16)
    vt_spec = pl.BlockSpec((None, N_KV, tm // TQ, HEAD_DIM, TQ), lambda b, i: (b, 0, i, 0, 0))
    return pl.pallas_call(
        _proj_kernel,
        grid=(B, nt),
        in_specs=[
            pl.BlockSpec((tm, D_MODEL), row),
            _const_spec((1, D_MODEL)),
            _const_spec((D_MODEL, PA_COLS)),
            _const_spec((D_MODEL, W_IN_COLS - PA_COLS)),
            _const_spec((D_MODEL, D_MODEL)),
            _const_spec((HEAD_DIM, 1)),
            _const_spec((3, LANES)),
            _const_spec((CONV_K, D_MODEL)),
        ],
        out_specs=[
            pl.BlockSpec((None, N_HEADS, HEAD_DIM, tm), lambda b, i: (b, 0, 0, i)),
            pl.BlockSpec((tm, KVW), row),
            pl.BlockSpec((tm, KVW), row),
            k_spec, vt_spec, k_spec, vt_spec,
            pl.BlockSpec((None, N_KV, GROWS, tm), lambda b, i: (b, 0, 0, i)),
            pl.BlockSpec((tm, D_MODEL), row),
            pl.BlockSpec((tm, D_MODEL), row),
        ],
        out_shape=[
            jax.ShapeDtypeStruct((B, N_HEADS, HEAD_DIM, S), BF16),
            jax.ShapeDtypeStruct((T, KVW), F32),
            jax.ShapeDtypeStruct((T, KVW), F32),
            k_shape, vt_shape, k_shape, vt_shape,
            jax.ShapeDtypeStruct((B, N_KV, GROWS, S), F32),
            jax.ShapeDtypeStruct((T, D_MODEL), F32),
            jax.ShapeDtypeStruct((T, D_MODEL), F32),
        ],
        scratch_shapes=[pltpu.VMEM((tm + 8, D_MODEL), F32)],
        compiler_params=pltpu.CompilerParams(
            dimension_semantics=("arbitrary", "arbitrary"),
            vmem_limit_bytes=VMEM_LIMIT),
        name="proj",
    )(x2, g1, w_head, w_tail, w_b, qg, kg, cw)


def _cmp_kernel(kci_ref, vci_ref, pos_ref, w_ref, kg_ref, kc_ref, vct_ref, b_scr):
    ncp = kc_ref.shape[1]
    acc_a = jnp.zeros((ncp, CMP_OUT), F32)
    acc_b = jnp.zeros((ncp, CMP_OUT), F32)
    for l in range(CMP_STRIDE):
        rows = jnp.concatenate([kci_ref[pl.ds(l, ncp, stride=CMP_STRIDE), :],
                                vci_ref[pl.ds(l, ncp, stride=CMP_STRIDE), :]], axis=1)
        xa = (rows + pos_ref[l:l + 1, :]).astype(BF16)
        xb = (rows + pos_ref[CMP_STRIDE + l:CMP_STRIDE + l + 1, :]).astype(BF16)
        acc_a = acc_a + jnp.dot(xa, w_ref[l], preferred_element_type=F32)
        acc_b = acc_b + jnp.dot(xb, w_ref[CMP_STRIDE + l], preferred_element_type=F32)
    b_scr[0:ncp, :] = acc_b
    b_scr[ncp:ncp + 8, :] = jnp.zeros((8, CMP_OUT), F32)
    kcv = acc_a + b_scr[1:ncp + 1, :]
    c_end = lax.broadcasted_iota(jnp.int32, (ncp, 1), 0) * CMP_STRIDE + (CMP_BLOCK - 1)
    key_lanes = _alibi_key_lanes(c_end)
    vc_t = kcv[:, 2 * LANES:CMP_OUT].T.astype(BF16)
    for g in range(N_KV):
        kc = _rms_padded(kcv[:, g * LANES:(g + 1) * LANES], kg_ref[0:1, :])
        kc_ref[g] = (kc + key_lanes).astype(BF16)
        vct_ref[g] = vc_t[g * HEAD_DIM:(g + 1) * HEAD_DIM, :]


def _cmp_call(kci, vci, pos4, w_bd, kg, B, S):
    ncp = S // CMP_STRIDE
    return pl.pallas_call(
        _cmp_kernel,
        grid=(B,),
        in_specs=[
            pl.BlockSpec((S, KVW), lambda b: (b, 0)),
            pl.BlockSpec((S, KVW), lambda b: (b, 0)),
            _const_spec((CMP_BLOCK, 2 * KVW)),
            _const_spec((CMP_BLOCK, 2 * KVW, CMP_OUT)),
            _const_spec((3, LANES)),
        ],
        out_specs=[
            pl.BlockSpec((None, N_KV, ncp, LANES), lambda b: (b, 0, 0, 0)),
            pl.BlockSpec((None, N_KV, HEAD_DIM, ncp), lambda b: (b, 0, 0, 0)),
        ],
        out_shape=[
            jax.ShapeDtypeStruct((B, N_KV, ncp, LANES), BF16),
            jax.ShapeDtypeStruct((B, N_KV, HEAD_DIM, ncp), BF16),
        ],
        scratch_shapes=[pltpu.VMEM((ncp + 8, CMP_OUT), F32)],
        compiler_params=pltpu.CompilerParams(
            dimension_semantics=("arbitrary",), vmem_limit_bytes=VMEM_LIMIT),
        name="compress",
    )(kci, vci, pos4, w_bd, kg)


def _attn_kernel(qt_ref, kc_ref, vct_ref, ks_ref, vst_ref, kw_ref, vwt_ref, gnt_ref,
                 ovl_ref, oh_ref, cband_ref, eye_ref, o_ref, score_scr, cnt_scr,
                 num_s, den_s, num_w, den_w, tiles_smem):
    acc_s = (num_s, den_s)
    acc_w = (num_w, den_w)
    g = pl.program_id(1)
    i = pl.program_id(2)
    t0 = i * QT
    kt_last = TILES_Q * (i + 1) - 1
    ns = ovl_ref.shape[0]
    ncp = ovl_ref.shape[1]

    lane = lax.broadcasted_iota(jnp.int32, (1, NL), 1)
    tq = t0 + (lane & (QT - 1))
    head = g * GQA + (lane >> QT_SHIFT) + 1
    slope = lax.bitcast_convert_type((127 - head) << 23, F32)
    a_t = (tq >> 6).astype(F32)
    b_t = (tq & 63).astype(F32)
    sub = lax.broadcasted_iota(jnp.int32, (HEAD_DIM, NL), 0)
    pos_col = lax.broadcasted_iota(jnp.int32, (TQ, 1), 0)
    qt = jnp.concatenate([qt_ref[r] for r in range(GQA)], axis=1)
    alibi_rows = jnp.where(sub == 0, -(slope * 64.0) * a_t,
                           jnp.where(sub == 1, -slope * b_t,
                                     jnp.where(sub == 2, slope * 64.0,
                                               jnp.where(sub == 3, slope, 0.0))))

    def aug_query(ref):
        rows = jnp.where(sub == 4, -ref, alibi_rows)
        return jnp.concatenate([qt, rows.astype(BF16)], axis=0)

    def tile(k_ref, kt):
        return k_ref[pl.ds(pl.multiple_of(kt * TQ, TQ), TQ), :]

    def causal(kt):
        return tq >= (kt * TQ + pos_col)

    def self_score(k_ref):
        k_t = jnp.concatenate([tile(k_ref, TILES_Q * i + c).astype(F32).T[0:HEAD_DIM, :]
                               for c in range(TILES_Q)], axis=1)
        return jnp.sum(qt.astype(F32) * jnp.concatenate([k_t] * GQA, axis=1),
                       axis=0, keepdims=True)

    def fold8(p):
        out = p[0:8, :]
        for k in range(1, TQ // 8):
            out = out + p[8 * k:8 * k + 8, :]
        return out

    def plain_tiles(score_fn, vt_ref, state, kts, masks, start, hook=None, lookahead=LOOKAHEAD):
        acc_ref, den_ref = state
        ahead = [score_fn(kt) for kt in kts[:lookahead]]
        if hook is not None:
            hook()
        total = None if start else acc_ref[...]
        den = None if start else den_ref[...]
        for n, (kt, mk) in enumerate(zip(kts, masks)):
            sc = ahead.pop(0)
            if n + lookahead < len(kts):
                ahead.append(score_fn(kts[n + lookahead]))
            if mk is not None:
                sc = jnp.where(mk, sc, NEG_INF)
            pr = jnp.exp(sc)
            pv = jnp.dot(vt_ref[kt], pr.astype(BF16), preferred_element_type=F32)
            total = pv if total is None else total + pv
            den = fold8(pr) if den is None else den + fold8(pr)
        acc_ref[...] = total
        den_ref[...] = den

    def online_tile(score_fn, vt_ref, state, kt, mask, m_run):
        acc_ref, den_ref = state
        sc = jnp.where(mask, score_fn(kt), NEG_INF)
        mx = jnp.max(sc, axis=0, keepdims=True)
        m_new = mx if m_run is None else jnp.maximum(m_run, mx)
        pr = jnp.exp(sc - m_new)
        pv = jnp.dot(vt_ref[kt], pr.astype(BF16), preferred_element_type=F32)
        if m_run is None:
            acc_ref[...] = pv
            den_ref[...] = fold8(pr)
        else:
            alpha = jnp.exp(m_run - m_new)
            acc_ref[...] = alpha * acc_ref[...] + pv
            den_ref[...] = alpha * den_ref[...] + fold8(pr)
        return m_new

    def finish(state):
        acc_ref, den_ref = state
        den = jnp.sum(den_ref[...], axis=0, keepdims=True)
        return acc_ref[...] / jnp.maximum(den, 1e-30)

    band = cband_ref[pl.ds(pl.multiple_of(ncp - (QT // CMP_STRIDE) * i, 8), ncp), :]
    lhs_c = jnp.concatenate([kc_ref[...], band.astype(BF16)], axis=1)
    rhs_c = jnp.concatenate([aug_query(0.0), eye_ref[...]], axis=0)
    has_cmp = tq >= CMP_BLOCK - 1
    qa_win = aug_query(self_score(kw_ref))

    def win_scores(kt):
        return jnp.dot(tile(kw_ref, kt), qa_win, preferred_element_type=F32)

    def cmp_and_window(win_kts, win_masks):
        s = jnp.dot(lhs_c, rhs_c, preferred_element_type=F32)
        out = []

        def cmp_rest():
            e = jnp.exp(s - jnp.max(s, axis=0, keepdims=True))
            den = jnp.maximum(jnp.sum(e, axis=0, keepdims=True), 1e-30)
            p = (e * jnp.where(has_cmp, 1.0 / den, 0.0)).astype(BF16)
            out.append(jnp.dot(vct_ref[...], p, preferred_element_type=F32))
            out.append(jnp.dot(ovl_ref[...], p, preferred_element_type=F32))

        plain_tiles(win_scores, vwt_ref, acc_w, win_kts, win_masks, True, hook=cmp_rest)
        return out[0], out[1]

    n_win = WINDOW // TQ
    n_wt = n_win + TILES_Q

    def win_steady():
        kts = [kt_last - d for d in range(n_wt)]
        masks = [causal(kt) for kt in kts[:TILES_Q]] + [None] * (n_win - TILES_Q)
        masks += [(tq - (kt * TQ + pos_col)) < WINDOW for kt in kts[n_win:]]
        return cmp_and_window(kts, masks)

    def win_start():
        kts = list(range(n_wt - 1))
        return cmp_and_window(kts, [causal(kt) for kt in kts])

    o_cmp, imp4 = lax.cond(kt_last >= n_wt - 1, win_steady, win_start)
    o_win = finish(acc_w)

    imp = imp4[:, 0:QT]
    for r in range(1, GQA):
        imp = imp + imp4[:, r * QT:(r + 1) * QT]
    j = lax.broadcasted_iota(jnp.int32, (ns, QT), 0)
    cur = (t0 + lax.broadcasted_iota(jnp.int32, (ns, QT), 1)) >> 6
    forced = (j == 0) | (j == cur) | (j == cur - 1)
    score = jnp.where(forced, FORCE_SCORE, jnp.where(j <= cur, imp, -1.0))
    score_scr[...] = score
    cnt_scr[...] = jnp.zeros((ns, QT), jnp.int32)

    row8 = lax.broadcasted_iota(jnp.int32, (8, QT), 0)
    for kg in range(ns // 8):
        @pl.when(8 * kg <= (TQ // SLC_BLOCK) * (kt_last + 1) - 1)
        def _(kg=kg):
            cnt = [cnt_scr[8 * v:8 * v + 8, :] for v in range(ns // 8)]
            for b in range(8 * kg, 8 * kg + 8):
                sb = score_scr[b:b + 1, :]
                for v in range(ns // 8):
                    sc_v = score[8 * v:8 * v + 8, :]
                    if v < kg:
                        ahead = sb > sc_v
                    elif v > kg:
                        ahead = sb >= sc_v
                    else:
                        ahead = (sb > sc_v) | ((sb == sc_v) & (row8 > b - 8 * kg))
                    cnt[v] = cnt[v] + jnp.where(ahead, 1, 0)
            for v in range(ns // 8):
                cnt_scr[8 * v:8 * v + 8, :] = cnt[v]

    sel = (cnt_scr[...] < N_SLC) & (j <= cur)
    selb = jnp.where(sel, 0.0, NEG_INF).astype(BF16)
    selb = jnp.concatenate([selb, jnp.zeros((LANES - ns, QT), BF16)], axis=0)

    sel_rows = jnp.concatenate([selb] * GQA, axis=1)
    qa_sel = jnp.concatenate([aug_query(self_score(ks_ref)), sel_rows], axis=0)

    def sel_scores(kt):
        lhs = jnp.concatenate([tile(ks_ref, kt), tile(oh_ref, kt)], axis=1)
        return jnp.dot(lhs, qa_sel, preferred_element_type=F32)

    used = jnp.max(jnp.where(sel, 1.0, 0.0), axis=1, keepdims=True)
    row = lax.broadcasted_iota(jnp.int32, (ns, 1), 0)
    bits = jnp.where(used > 0.0, jnp.left_shift(1, row & 31), 0)
    words = [jnp.sum(bits[32 * w:32 * (w + 1), :]) for w in range(ns // 32)]
    blocks_per_tile = TQ // SLC_BLOCK
    n_act = jnp.int32(0)
    for kt in range(ns // blocks_per_tile - 1):
        field = (words[(kt * blocks_per_tile) // 32] >> ((kt * blocks_per_tile) % 32))
        active = ((field & (2 ** blocks_per_tile - 1)) != 0) & (kt < kt_last)
        tiles_smem[n_act] = kt
        n_act = n_act + active.astype(jnp.int32)
    tiles_smem[n_act] = kt_last

    n_loop = jnp.maximum(n_act // UNROLL - 1, 0)
    base = n_loop * UNROLL
    for v in range(2 * UNROLL):
        @pl.when(n_act - base == v)
        def _(v=v):
            kts = [tiles_smem[base + u] for u in range(v + 1)]
            plain_tiles(sel_scores, vst_ref, acc_s, kts, [None] * v + [causal(kts[v])], True)

    def sel_body(grp, carry):
        kts = [tiles_smem[grp * UNROLL + u] for u in range(UNROLL)]
        plain_tiles(sel_scores, vst_ref, acc_s, kts, [None] * UNROLL, False)
        return carry

    lax.fori_loop(0, n_loop, sel_body, 0)
    o_slc = finish(acc_s)

    def gate(branch):
        return jnp.concatenate(
            [gnt_ref[branch * GQA + r:branch * GQA + r + 1, :] for r in range(GQA)], axis=1)

    def emit(o_slc, o_win):
        o = gate(0) * o_cmp + gate(1) * o_slc + gate(2) * o_win
        for h in range(GQA // 2):
            pair = jnp.concatenate([o[:, (2 * h) * QT:(2 * h + 1) * QT],
                                    o[:, (2 * h + 1) * QT:(2 * h + 2) * QT]], axis=0)
            for c in range(TILES_Q):
                o_ref[c * TQ:(c + 1) * TQ, h * LANES:(h + 1) * LANES] = (
                    pair[:, c * TQ:(c + 1) * TQ].T.astype(BF16))
        return o

    bad = jnp.max(jnp.where(jnp.isfinite(emit(o_slc, o_win)), 0.0, 1.0))

    @pl.when(bad > 0.0)
    def _():
        m_run = online_tile(sel_scores, vst_ref, acc_s, kt_last, causal(kt_last), None)
        lax.fori_loop(0, kt_last, lambda kt, m_in: online_tile(sel_scores, vst_ref, acc_s, kt,
                                                               causal(kt), m_in), m_run)
        m_run = None
        for d in list(range(TILES_Q - 1, n_wt)) + list(range(TILES_Q - 1)):
            pos = (kt_last - d) * TQ + pos_col
            inside = (tq >= pos) & (tq - pos < WINDOW) & (pos >= 0)
            m_run = online_tile(win_scores, vwt_ref, acc_w, jnp.maximum(kt_last - d, 0), inside,
                                m_run)
        emit(finish(acc_s), finish(acc_w))


def _attn_call(qt, kc, vct, ks, vst, kw, vwt, gnt, ovl, onehot, cband, eye, B, S):
    T = B * S
    nq = S // QT
    ncp = S // CMP_STRIDE
    ns = S // SLC_BLOCK
    bg4 = lambda b, g, i: (b, g, 0, 0)
    bg5 = lambda b, g, i: (b, g, 0, 0, 0)
    k_spec = pl.BlockSpec((None, None, S, LANES), bg4)
    vt_spec = pl.BlockSpec((None, None, S // TQ, HEAD_DIM, TQ), bg5)
    return pl.pallas_call(
        _attn_kernel,
        grid=(B, N_KV, nq),
        in_specs=[
            pl.BlockSpec((None, GQA, HEAD_DIM, QT), lambda b, g, i: (b, g, 0, i)),
            pl.BlockSpec((None, None, ncp, LANES), bg4),
            pl.BlockSpec((None, None, HEAD_DIM, ncp), bg4),
            k_spec, vt_spec, k_spec, vt_spec,
            pl.BlockSpec((None, None, GROWS, QT), lambda b, g, i: (b, g, 0, i)),
            _const_spec((ns, ncp)),
            _const_spec((S, LANES)),
            _const_spec((2 * ncp, QT)),
            _const_spec((QT, NL)),
        ],
        out_specs=pl.BlockSpec((QT, GQA * HEAD_DIM), lambda b, g, i: (b * nq + i, g)),
        out_shape=jax.ShapeDtypeStruct((T, QW), BF16),
        scratch_shapes=[pltpu.VMEM((ns, QT), F32), pltpu.VMEM((ns, QT), jnp.int32),
                        pltpu.VMEM((HEAD_DIM, NL), F32), pltpu.VMEM((8, NL), F32),
                        pltpu.VMEM((HEAD_DIM, NL), F32), pltpu.VMEM((8, NL), F32),
                        pltpu.SMEM((S // TQ + 8,), jnp.int32)],
        compiler_params=pltpu.CompilerParams(
            dimension_semantics=("arbitrary", "arbitrary", "arbitrary"),
            vmem_limit_bytes=VMEM_LIMIT),
        name="attn",
    )(qt, kc, vct, ks, vst, kw, vwt, gnt, ovl, onehot, cband, eye)


FF_CHUNK = 512


def _mlp_kernel(x_ref, o_ref, sga_ref, ob_ref, wa_ref, wo_ref, g2_ref, wu_ref, wd_ref,
                out_ref, acc_scr):
    a = jnp.dot(o_ref[...], wa_ref[...], preferred_element_type=F32)
    mixed = sga_ref[...] * a + ob_ref[...]
    x1 = x_ref[...] + jnp.dot(mixed.astype(BF16), wo_ref[...], preferred_element_type=F32)
    h = _rms(x1, g2_ref[...]).astype(BF16)
    acc_scr[...] = x1
    for c in range(D_FF // FF_CHUNK):
        lo, hi = c * FF_CHUNK, (c + 1) * FF_CHUNK
        up = jnp.dot(h, wu_ref[:, lo:hi], preferred_element_type=F32)
        act = jnp.square(jnp.maximum(up, 0.0)).astype(BF16)
        acc_scr[...] += jnp.dot(act, wd_ref[lo:hi, :], preferred_element_type=F32)
    out_ref[...] = acc_scr[...]


def _mlp_call(x2, o_nsa, sga, ob, w_a, w_o, g2, w_up, w_down, tm):
    T = x2.shape[0]
    row = lambda i: (i, 0)
    return pl.pallas_call(
        _mlp_kernel,
        grid=(T // tm,),
        in_specs=[
            pl.BlockSpec((tm, D_MODEL), row),
            pl.BlockSpec((tm, QW), row),
            pl.BlockSpec((tm, D_MODEL), row),
            pl.BlockSpec((tm, D_MODEL), row),
            _const_spec((QW, D_MODEL)),
            _const_spec((D_MODEL, D_MODEL)),
            _const_spec((1, D_MODEL)),
            _const_spec((D_MODEL, D_FF)),
            _const_spec((D_FF, D_MODEL)),
        ],
        out_specs=pl.BlockSpec((tm, D_MODEL), row),
        out_shape=jax.ShapeDtypeStruct((T, D_MODEL), F32),
        scratch_shapes=[pltpu.VMEM((tm, D_MODEL), F32)],
        compiler_params=pltpu.CompilerParams(
            dimension_semantics=("arbitrary",), vmem_limit_bytes=VMEM_LIMIT),
        name="mlp",
    )(x2, o_nsa, sga, ob, w_a, w_o, g2, w_up, w_down)


def _pack_w_in(w_in):
    o = OFF_GN
    gates = w_in[:, o:o + 3 * N_HEADS]
    gates = gates.reshape(D_MODEL, 3, N_KV, GQA).transpose(0, 2, 1, 3)
    gates = gates.reshape(D_MODEL, N_KV, 3 * GQA)
    gates = jnp.pad(gates, ((0, 0), (0, 0), (0, HEAD_DIM - 3 * GQA))).reshape(D_MODEL, LANES)
    head = jnp.concatenate([w_in[:, :o], gates], axis=1).astype(BF16)
    return head, w_in[:, o + 3 * N_HEADS:].astype(BF16)


def _pack_cmp(w_cmp_k, w_cmp_v, cmp_pos_k, cmp_pos_v):
    wk = w_cmp_k.reshape(CMP_BLOCK, HEAD_DIM, HEAD_DIM)
    wv = w_cmp_v.reshape(CMP_BLOCK, HEAD_DIM, HEAD_DIM)
    slabs = [jnp.pad(blk.astype(BF16), ((0, 0), (0, 0), (col, CMP_OUT - col - HEAD_DIM)))
             for blk, col in ((wk, 0), (wk, LANES), (wv, 2 * LANES), (wv, 2 * LANES + HEAD_DIM))]
    pos4 = jnp.concatenate([cmp_pos_k, cmp_pos_k, cmp_pos_v, cmp_pos_v], axis=1)
    return jnp.concatenate(slabs, axis=1), pos4


def _pad_gain(g):
    return jnp.pad(g, ((0, 0), (0, LANES - HEAD_DIM)))


def _overlap(S):
    ncp = S // CMP_STRIDE
    ns = S // SLC_BLOCK
    c_start = np.arange(ncp) * CMP_STRIDE
    s_start = np.arange(ns) * SLC_BLOCK
    ov = np.clip(np.minimum(c_start[None, :] + CMP_BLOCK, s_start[:, None] + SLC_BLOCK)
                 - np.maximum(c_start[None, :], s_start[:, None]), 0, None)
    return jnp.asarray(ov.astype(np.float32) / CMP_BLOCK, dtype=BF16)


def _block_onehot(S):
    pos = np.arange(S)
    oh = np.zeros((S, LANES), np.float32)
    oh[pos, pos // SLC_BLOCK] = 1.0
    return jnp.asarray(oh, dtype=BF16)


def _cmp_band(S):
    ncp = S // CMP_STRIDE
    c_rel = np.arange(2 * ncp)[:, None] - ncp
    visible = c_rel * CMP_STRIDE + (CMP_BLOCK - 1) <= np.arange(QT)[None, :]
    return jnp.asarray(np.where(visible, 0.0, NEG_INF), dtype=F32)


def _tiled_eye():
    return jnp.asarray(np.tile(np.eye(QT, dtype=np.float32), (1, GQA)), dtype=BF16)


def _layer(x, norm1_g, w_in, q_norm_g, k_norm_g, cmp_pos_k, cmp_pos_v, w_cmp_k, w_cmp_v,
           conv_w, w_branch_a, w_branch_b, w_out, norm2_g, w_up, w_down):
    B, S, _ = x.shape
    assert S % (CMP_STRIDE * LANES) == 0 and S // SLC_BLOCK <= LANES
    assert (S // TQ) % UNROLL == 0 and S >= WINDOW
    assert TILES_Q == 1 and (S // SLC_BLOCK) % 32 == 0
    x2 = x.reshape(B * S, D_MODEL)
    w_bd, pos4 = _pack_cmp(w_cmp_k, w_cmp_v, cmp_pos_k, cmp_pos_v)
    kg = _pad_gain(k_norm_g)
    qt, kci, vci, ks, vst, kw, vwt, gnt, sga, ob = _proj_call(
        x2, norm1_g[None, :], *_pack_w_in(w_in), w_branch_b.astype(BF16),
        q_norm_g[:, None], kg, conv_w, B, S, tm=512)
    kc, vct = _cmp_call(kci, vci, pos4, w_bd, kg, B, S)
    o_nsa = _attn_call(qt, kc, vct, ks, vst, kw, vwt, gnt, _overlap(S), _block_onehot(S),
                       _cmp_band(S), _tiled_eye(), B, S)
    out = _mlp_call(x2, o_nsa, sga, ob, w_branch_a.astype(BF16), w_out.astype(BF16),
                    norm2_g[None, :], w_up.astype(BF16), w_down.astype(BF16), tm=512)
    return out.reshape(B, S, D_MODEL)


@jax.jit
def kernel(x, norm1_g, w_in, q_norm_g, k_norm_g, cmp_pos_k, cmp_pos_v, w_cmp_k, w_cmp_v,
           conv_w, w_branch_a, w_branch_b, w_out, norm2_g, w_up, w_down):
    for l in range(norm1_g.shape[0]):
        x = _layer(x, norm1_g[l], w_in[l], q_norm_g[l], k_norm_g[l], cmp_pos_k[l],
                   cmp_pos_v[l], w_cmp_k[l], w_cmp_v[l], conv_w[l], w_branch_a[l],
                   w_branch_b[l], w_out[l], norm2_g[l], w_up[l], w_down[l])
    return x
```

```python
import numpy as np
import jax
import jax.numpy as jnp
from jax import lax
from jax.experimental import pallas as pl
from jax.experimental.pallas import tpu as pltpu

D_MODEL = 1024
N_HEADS = 8
HEAD_DIM = 64
N_KV = 2
GQA = N_HEADS // N_KV
CMP_BLOCK = 32
CMP_STRIDE = 16
SLC_BLOCK = 64
N_SLC = 16
WINDOW = 512
FORCE_SCORE = 1e4
SCALE = 0.125
CONV_K = 3
D_FF = 4 * D_MODEL
EPS = 1e-6
NEG_INF = -1e30

LANES = 128
QW = N_HEADS * HEAD_DIM
KVW = N_KV * HEAD_DIM

OFF_Q = 0
OFF_KC = OFF_Q + QW
OFF_VC = OFF_KC + KVW
OFF_KS = OFF_VC + KVW
OFF_VS = OFF_KS + KVW
OFF_KW = OFF_VS + KVW
OFF_VW = OFF_KW + KVW
OFF_GN = OFF_VW + KVW
PA_COLS = OFF_GN + LANES
OFF_CONV_B = PA_COLS
OFF_CONV_C = OFF_CONV_B + D_MODEL
OFF_CONV_X = OFF_CONV_C + D_MODEL
OFF_GATE_A = OFF_CONV_X + D_MODEL
OFF_GATE_B = OFF_GATE_A + D_MODEL
W_IN_COLS = OFF_GATE_B + D_MODEL

PROJ_SUB = 256
TQ = 256
QT = 256
QT_SHIFT = QT.bit_length() - 1
TILES_Q = QT // TQ
NL = GQA * QT
UNROLL = 4
LOOKAHEAD = 1
GROWS = 16
CMP_OUT = 2 * LANES + KVW
VMEM_LIMIT = 56 * 1024 * 1024

F32 = jnp.float32
BF16 = jnp.bfloat16


def _rms(x, g):
    return x * lax.rsqrt(jnp.mean(x * x, axis=-1, keepdims=True) + EPS) * g


def _rms_padded(x, g):
    ms = jnp.sum(x * x, axis=-1, keepdims=True) * (1.0 / HEAD_DIM)
    return x * lax.rsqrt(ms + EPS) * g


def _alibi_key_lanes(pos):
    lane = lax.broadcasted_iota(jnp.int32, (pos.shape[0], LANES), 1)
    hi = (pos >> 6).astype(F32)
    lo = (pos & 63).astype(F32)
    return jnp.where((lane == HEAD_DIM) | (lane == HEAD_DIM + 1) | (lane == HEAD_DIM + 4), 1.0,
                     jnp.where(lane == HEAD_DIM + 2, hi,
                               jnp.where(lane == HEAD_DIM + 3, lo, 0.0)))


def _const_spec(shape):
    zeros = (0,) * len(shape)
    return pl.BlockSpec(shape, lambda *_: zeros, pipeline_mode=pl.Buffered(1))


def _proj_kernel(x_ref, g1_ref, wh_ref, wt_ref, wb_ref, qg_ref, kg_ref, cw_ref,
                 qt_ref, kci_ref, vci_ref, ks_ref, vst_ref, kw_ref, vwt_ref, gnt_ref,
                 sga_ref, ob_ref, u_scr):
    i = pl.program_id(1)
    tm = x_ref.shape[0]

    @pl.when(i == 0)
    def _():
        u_scr[0:8, :] = jnp.zeros((8, D_MODEL), F32)

    for r0 in range(0, tm, PROJ_SUB):
        _proj_rows(r0, i * tm + r0, x_ref, g1_ref, wh_ref, wt_ref, wb_ref, qg_ref, kg_ref, cw_ref,
                   qt_ref, kci_ref, vci_ref, ks_ref, vst_ref, kw_ref, vwt_ref, gnt_ref,
                   sga_ref, ob_ref, u_scr)
    u_scr[0:8, :] = u_scr[tm:tm + 8, :]


def _proj_rows(r0, pos0, x_ref, g1_ref, wh_ref, wt_ref, wb_ref, qg_ref, kg_ref, cw_ref,
               qt_ref, kci_ref, vci_ref, ks_ref, vst_ref, kw_ref, vwt_ref, gnt_ref,
               sga_ref, ob_ref, u_scr):
    n = PROJ_SUB
    rows = slice(r0, r0 + n)
    xn = _rms(x_ref[rows, :], g1_ref[...]).astype(BF16)

    pa = jnp.dot(xn, wh_ref[...], preferred_element_type=F32)

    qg = qg_ref[...]
    for pair in range(N_HEADS // 2):
        qp = pa[:, OFF_Q + pair * LANES:OFF_Q + (pair + 1) * LANES].T
        for half in range(2):
            qh = qp[half * HEAD_DIM:(half + 1) * HEAD_DIM, :]
            ms = jnp.mean(qh * qh, axis=0, keepdims=True)
            qt_ref[2 * pair + half, :, rows] = (qh * lax.rsqrt(ms + EPS) * qg * SCALE).astype(BF16)

    kci_ref[rows, :] = pa[:, OFF_KC:OFF_KC + KVW]
    vci_ref[rows, :] = pa[:, OFF_VC:OFF_VC + KVW]

    pos = pos0 + lax.broadcasted_iota(jnp.int32, (n, 1), 0)
    key_lanes = _alibi_key_lanes(pos)
    low = lax.broadcasted_iota(jnp.int32, (n, LANES), 1) < HEAD_DIM
    ks2 = pa[:, OFF_KS:OFF_KS + KVW]
    kw2 = pa[:, OFF_KW:OFF_KW + KVW]
    vs_t = pa[:, OFF_VS:OFF_VS + KVW].T.astype(BF16)
    vw_t = pa[:, OFF_VW:OFF_VW + KVW].T.astype(BF16)
    gates_t = jax.nn.sigmoid(pa[:, OFF_GN:OFF_GN + LANES]).T
    for g in range(N_KV):
        ks = jnp.where(low, ks2 if g == 0 else pltpu.roll(ks2, HEAD_DIM, axis=1), 0.0)
        kw = jnp.where(low, kw2 if g == 0 else pltpu.roll(kw2, HEAD_DIM, axis=1), 0.0)
        ks_ref[g, rows, :] = (_rms_padded(ks, kg_ref[1:2, :]) + key_lanes).astype(BF16)
        kw_ref[g, rows, :] = (_rms_padded(kw, kg_ref[2:3, :]) + key_lanes).astype(BF16)
        for c in range(n // TQ):
            ct = r0 // TQ + c
            vst_ref[g, ct] = vs_t[g * HEAD_DIM:(g + 1) * HEAD_DIM, c * TQ:(c + 1) * TQ]
            vwt_ref[g, ct] = vw_t[g * HEAD_DIM:(g + 1) * HEAD_DIM, c * TQ:(c + 1) * TQ]
        gnt_ref[g, :, rows] = gates_t[g * HEAD_DIM:g * HEAD_DIM + GROWS, :]

    def col(o):
        return jnp.dot(xn, wt_ref[:, o - PA_COLS:o - PA_COLS + D_MODEL],
                       preferred_element_type=F32)

    u = col(OFF_CONV_C) * col(OFF_CONV_X)
    u_scr[8 + r0:8 + r0 + n, :] = u
    cw = cw_ref[...]
    y = (cw[2:3, :] * u + cw[1:2, :] * u_scr[7 + r0:7 + r0 + n, :]
         + cw[0:1, :] * u_scr[6 + r0:6 + r0 + n, :])
    z = col(OFF_CONV_B) * y
    zb = jnp.dot(z.astype(BF16), wb_ref[...], preferred_element_type=F32)
    ob_ref[rows, :] = jax.nn.sigmoid(col(OFF_GATE_B)) * zb
    sga_ref[rows, :] = jax.nn.sigmoid(col(OFF_GATE_A))


def _proj_call(x2, g1, w_head, w_tail, w_b, qg, kg, cw, B, S, tm):
    T = B * S
    nt = S // tm
    row = lambda b, i: (b * nt + i, 0)
    k_shape = jax.ShapeDtypeStruct((B, N_KV, S, LANES), BF16)
    k_spec = pl.BlockSpec((None, N_KV, tm, LANES), lambda b, i: (b, 0, i, 0))
    vt_shape = jax.ShapeDtypeStruct((B, N_KV, S // TQ, HEAD_DIM, TQ), BF16)
    vt_spec = pl.BlockSpec((None, N_KV, tm // TQ, HEAD_DIM, TQ), lambda b, i: (b, 0, i, 0, 0))
    return pl.pallas_call(
        _proj_kernel,
        grid=(B, nt),
        in_specs=[
            pl.BlockSpec((tm, D_MODEL), row),
            _const_spec((1, D_MODEL)),
            _const_spec((D_MODEL, PA_COLS)),
            _const_spec((D_MODEL, W_IN_COLS - PA_COLS)),
            _const_spec((D_MODEL, D_MODEL)),
            _const_spec((HEAD_DIM, 1)),
            _const_spec((3, LANES)),
            _const_spec((CONV_K, D_MODEL)),
        ],
        out_specs=[
            pl.BlockSpec((None, N_HEADS, HEAD_DIM, tm), lambda b, i: (b, 0, 0, i)),
            pl.BlockSpec((tm, KVW), row),
            pl.BlockSpec((tm, KVW), row),
            k_spec, vt_spec, k_spec, vt_spec,
            pl.BlockSpec((None, N_KV, GROWS, tm), lambda b, i: (b, 0, 0, i)),
            pl.BlockSpec((tm, D_MODEL), row),
            pl.BlockSpec((tm, D_MODEL), row),
        ],
        out_shape=[
            jax.ShapeDtypeStruct((B, N_HEADS, HEAD_DIM, S), BF16),
            jax.ShapeDtypeStruct((T, KVW), F32),
            jax.ShapeDtypeStruct((T, KVW), F32),
            k_shape, vt_shape, k_shape, vt_shape,
            jax.ShapeDtypeStruct((B, N_KV, GROWS, S), F32),
            jax.ShapeDtypeStruct((T, D_MODEL), F32),
            jax.ShapeDtypeStruct((T, D_MODEL), F32),
        ],
        scratch_shapes=[pltpu.VMEM((tm + 8, D_MODEL), F32)],
        compiler_params=pltpu.CompilerParams(
            dimension_semantics=("arbitrary", "arbitrary"),
            vmem_limit_bytes=VMEM_LIMIT),
        name="proj",
    )(x2, g1, w_head, w_tail, w_b, qg, kg, cw)


def _cmp_kernel(kci_ref, vci_ref, pos_ref, w_ref, kg_ref, kc_ref, vct_ref, b_scr):
    ncp = kc_ref.shape[1]
    acc_a = jnp.zeros((ncp, CMP_OUT), F32)
    acc_b = jnp.zeros((ncp, CMP_OUT), F32)
    for l in range(CMP_STRIDE):
        rows = jnp.concatenate([kci_ref[pl.ds(l, ncp, stride=CMP_STRIDE), :],
                                vci_ref[pl.ds(l, ncp, stride=CMP_STRIDE), :]], axis=1)
        xa = (rows + pos_ref[l:l + 1, :]).astype(BF16)
        xb = (rows + pos_ref[CMP_STRIDE + l:CMP_STRIDE + l + 1, :]).astype(BF16)
        acc_a = acc_a + jnp.dot(xa, w_ref[l], preferred_element_type=F32)
        acc_b = acc_b + jnp.dot(xb, w_ref[CMP_STRIDE + l], preferred_element_type=F32)
    b_scr[0:ncp, :] = acc_b
    b_scr[ncp:ncp + 8, :] = jnp.zeros((8, CMP_OUT), F32)
    kcv = acc_a + b_scr[1:ncp + 1, :]
    c_end = lax.broadcasted_iota(jnp.int32, (ncp, 1), 0) * CMP_STRIDE + (CMP_BLOCK - 1)
    key_lanes = _alibi_key_lanes(c_end)
    vc_t = kcv[:, 2 * LANES:CMP_OUT].T.astype(BF16)
    for g in range(N_KV):
        kc = _rms_padded(kcv[:, g * LANES:(g + 1) * LANES], kg_ref[0:1, :])
        kc_ref[g] = (kc + key_lanes).astype(BF16)
        vct_ref[g] = vc_t[g * HEAD_DIM:(g + 1) * HEAD_DIM, :]


def _cmp_call(kci, vci, pos4, w_bd, kg, B, S):
    ncp = S // CMP_STRIDE
    return pl.pallas_call(
        _cmp_kernel,
        grid=(B,),
        in_specs=[
            pl.BlockSpec((S, KVW), lambda b: (b, 0)),
            pl.BlockSpec((S, KVW), lambda b: (b, 0)),
            _const_spec((CMP_BLOCK, 2 * KVW)),
            _const_spec((CMP_BLOCK, 2 * KVW, CMP_OUT)),
            _const_spec((3, LANES)),
        ],
        out_specs=[
            pl.BlockSpec((None, N_KV, ncp, LANES), lambda b: (b, 0, 0, 0)),
            pl.BlockSpec((None, N_KV, HEAD_DIM, ncp), lambda b: (b, 0, 0, 0)),
        ],
        out_shape=[
            jax.ShapeDtypeStruct((B, N_KV, ncp, LANES), BF16),
            jax.ShapeDtypeStruct((B, N_KV, HEAD_DIM, ncp), BF16),
        ],
        scratch_shapes=[pltpu.VMEM((ncp + 8, CMP_OUT), F32)],
        compiler_params=pltpu.CompilerParams(
            dimension_semantics=("arbitrary",), vmem_limit_bytes=VMEM_LIMIT),
        name="compress",
    )(kci, vci, pos4, w_bd, kg)


def _attn_kernel(qt_ref, kc_ref, vct_ref, ks_ref, vst_ref, kw_ref, vwt_ref, gnt_ref,
                 ovl_ref, oh_ref, cband_ref, eye_ref, o_ref, score_scr, cnt_scr,
                 num_s, den_s, num_w, den_w, tiles_smem):
    acc_s = (num_s, den_s)
    acc_w = (num_w, den_w)
    g = pl.program_id(1)
    i = pl.program_id(2)
    t0 = i * QT
    kt_last = TILES_Q * (i + 1) - 1
    ns = ovl_ref.shape[0]
    ncp = ovl_ref.shape[1]

    lane = lax.broadcasted_iota(jnp.int32, (1, NL), 1)
    tq = t0 + (lane & (QT - 1))
    head = g * GQA + (lane >> QT_SHIFT) + 1
    slope = lax.bitcast_convert_type((127 - head) << 23, F32)
    a_t = (tq >> 6).astype(F32)
    b_t = (tq & 63).astype(F32)
    sub = lax.broadcasted_iota(jnp.int32, (HEAD_DIM, NL), 0)
    pos_col = lax.broadcasted_iota(jnp.int32, (TQ, 1), 0)
    qt = jnp.concatenate([qt_ref[r] for r in range(GQA)], axis=1)
    alibi_rows = jnp.where(sub == 0, -(slope * 64.0) * a_t,
                           jnp.where(sub == 1, -slope * b_t,
                                     jnp.where(sub == 2, slope * 64.0,
                                               jnp.where(sub == 3, slope, 0.0))))

    def aug_query(ref):
        rows = jnp.where(sub == 4, -ref, alibi_rows)
        return jnp.concatenate([qt, rows.astype(BF16)], axis=0)

    def tile(k_ref, kt):
        return k_ref[pl.ds(pl.multiple_of(kt * TQ, TQ), TQ), :]

    def causal(kt):
        return tq >= (kt * TQ + pos_col)

    def self_score(k_ref):
        k_t = jnp.concatenate([tile(k_ref, TILES_Q * i + c).astype(F32).T[0:HEAD_DIM, :]
                               for c in range(TILES_Q)], axis=1)
        return jnp.sum(qt.astype(F32) * jnp.concatenate([k_t] * GQA, axis=1),
                       axis=0, keepdims=True)

    def fold8(p):
        out = p[0:8, :]
        for k in range(1, TQ // 8):
            out = out + p[8 * k:8 * k + 8, :]
        return out

    def plain_tiles(score_fn, vt_ref, state, kts, masks, start, hook=None, lookahead=LOOKAHEAD):
        acc_ref, den_ref = state
        ahead = [score_fn(kt) for kt in kts[:lookahead]]
        if hook is not None:
            hook()
        total = None if start else acc_ref[...]
        den = None if start else den_ref[...]
        for n, (kt, mk) in enumerate(zip(kts, masks)):
            sc = ahead.pop(0)
            if n + lookahead < len(kts):
                ahead.append(score_fn(kts[n + lookahead]))
            if mk is not None:
                sc = jnp.where(mk, sc, NEG_INF)
            pr = jnp.exp(sc)
            pv = jnp.dot(vt_ref[kt], pr.astype(BF16), preferred_element_type=F32)
            total = pv if total is None else total + pv
            den = fold8(pr) if den is None else den + fold8(pr)
        acc_ref[...] = total
        den_ref[...] = den

    def online_tile(score_fn, vt_ref, state, kt, mask, m_run):
        acc_ref, den_ref = state
        sc = jnp.where(mask, score_fn(kt), NEG_INF)
        mx = jnp.max(sc, axis=0, keepdims=True)
        m_new = mx if m_run is None else jnp.maximum(m_run, mx)
        pr = jnp.exp(sc - m_new)
        pv = jnp.dot(vt_ref[kt], pr.astype(BF16), preferred_element_type=F32)
        if m_run is None:
            acc_ref[...] = pv
            den_ref[...] = fold8(pr)
        else:
            alpha = jnp.exp(m_run - m_new)
            acc_ref[...] = alpha * acc_ref[...] + pv
            den_ref[...] = alpha * den_ref[...] + fold8(pr)
        return m_new

    def finish(state):
        acc_ref, den_ref = state
        den = jnp.sum(den_ref[...], axis=0, keepdims=True)
        return acc_ref[...] / jnp.maximum(den, 1e-30)

    band = cband_ref[pl.ds(pl.multiple_of(ncp - (QT // CMP_STRIDE) * i, 8), ncp), :]
    lhs_c = jnp.concatenate([kc_ref[...], band.astype(BF16)], axis=1)
    rhs_c = jnp.concatenate([aug_query(0.0), eye_ref[...]], axis=0)
    has_cmp = tq >= CMP_BLOCK - 1
    qa_win = aug_query(self_score(kw_ref))

    def win_scores(kt):
        return jnp.dot(tile(kw_ref, kt), qa_win, preferred_element_type=F32)

    def cmp_and_window(win_kts, win_masks):
        s = jnp.dot(lhs_c, rhs_c, preferred_element_type=F32)
        out = []

        def cmp_rest():
            e = jnp.exp(s - jnp.max(s, axis=0, keepdims=True))
            den = jnp.maximum(jnp.sum(e, axis=0, keepdims=True), 1e-30)
            p = (e * jnp.where(has_cmp, 1.0 / den, 0.0)).astype(BF16)
            out.append(jnp.dot(vct_ref[...], p, preferred_element_type=F32))
            out.append(jnp.dot(ovl_ref[...], p, preferred_element_type=F32))

        plain_tiles(win_scores, vwt_ref, acc_w, win_kts, win_masks, True, hook=cmp_rest)
        return out[0], out[1]

    n_win = WINDOW // TQ
    n_wt = n_win + TILES_Q

    def win_steady():
        kts = [kt_last - d for d in range(n_wt)]
        masks = [causal(kt) for kt in kts[:TILES_Q]] + [None] * (n_win - TILES_Q)
        masks += [(tq - (kt * TQ + pos_col)) < WINDOW for kt in kts[n_win:]]
        return cmp_and_window(kts, masks)

    def win_start():
        kts = list(range(n_wt - 1))
        return cmp_and_window(kts, [causal(kt) for kt in kts])

    o_cmp, imp4 = lax.cond(kt_last >= n_wt - 1, win_steady, win_start)
    o_win = finish(acc_w)

    imp = imp4[:, 0:QT]
    for r in range(1, GQA):
        imp = imp + imp4[:, r * QT:(r + 1) * QT]
    j = lax.broadcasted_iota(jnp.int32, (ns, QT), 0)
    cur = (t0 + lax.broadcasted_iota(jnp.int32, (ns, QT), 1)) >> 6
    forced = (j == 0) | (j == cur) | (j == cur - 1)
    score = jnp.where(forced, FORCE_SCORE, jnp.where(j <= cur, imp, -1.0))
    score_scr[...] = score
    cnt_scr[...] = jnp.zeros((ns, QT), jnp.int32)

    row8 = lax.broadcasted_iota(jnp.int32, (8, QT), 0)
    for kg in range(ns // 8):
        @pl.when(8 * kg <= (TQ // SLC_BLOCK) * (kt_last + 1) - 1)
        def _(kg=kg):
            cnt = [cnt_scr[8 * v:8 * v + 8, :] for v in range(ns // 8)]
            for b in range(8 * kg, 8 * kg + 8):
                sb = score_scr[b:b + 1, :]
                for v in range(ns // 8):
                    sc_v = score[8 * v:8 * v + 8, :]
                    if v < kg:
                        ahead = sb > sc_v
                    elif v > kg:
                        ahead = sb >= sc_v
                    else:
                        ahead = (sb > sc_v) | ((sb == sc_v) & (row8 > b - 8 * kg))
                    cnt[v] = cnt[v] + jnp.where(ahead, 1, 0)
            for v in range(ns // 8):
                cnt_scr[8 * v:8 * v + 8, :] = cnt[v]

    sel = (cnt_scr[...] < N_SLC) & (j <= cur)
    selb = jnp.where(sel, 0.0, NEG_INF).astype(BF16)
    selb = jnp.concatenate([selb, jnp.zeros((LANES - ns, QT), BF16)], axis=0)

    sel_rows = jnp.concatenate([selb] * GQA, axis=1)
    qa_sel = jnp.concatenate([aug_query(self_score(ks_ref)), sel_rows], axis=0)

    def sel_scores(kt):
        lhs = jnp.concatenate([tile(ks_ref, kt), tile(oh_ref, kt)], axis=1)
        return jnp.dot(lhs, qa_sel, preferred_element_type=F32)

    used = jnp.max(jnp.where(sel, 1.0, 0.0), axis=1, keepdims=True)
    row = lax.broadcasted_iota(jnp.int32, (ns, 1), 0)
    bits = jnp.where(used > 0.0, jnp.left_shift(1, row & 31), 0)
    words = [jnp.sum(bits[32 * w:32 * (w + 1), :]) for w in range(ns // 32)]
    blocks_per_tile = TQ // SLC_BLOCK
    n_act = jnp.int32(0)
    for kt in range(ns // blocks_per_tile - 1):
        field = (words[(kt * blocks_per_tile) // 32] >> ((kt * blocks_per_tile) % 32))
        active = ((field & (2 ** blocks_per_tile - 1)) != 0) & (kt < kt_last)
        tiles_smem[n_act] = kt
        n_act = n_act + active.astype(jnp.int32)
    tiles_smem[n_act] = kt_last

    n_loop = jnp.maximum(n_act // UNROLL - 1, 0)
    base = n_loop * UNROLL
    def first_block(v):
        kts = [tiles_smem[base + u] for u in range(v + 1)]
        plain_tiles(sel_scores, vst_ref, acc_s, kts, [None] * v + [causal(kts[v])], True)
        return 0

    lax.switch(n_act - base, [lambda v=v: first_block(v) for v in range(2 * UNROLL)])

    def sel_body(grp, carry):
        kts = [tiles_smem[grp * UNROLL + u] for u in range(UNROLL)]
        plain_tiles(sel_scores, vst_ref, acc_s, kts, [None] * UNROLL, False)
        return carry

    lax.fori_loop(0, n_loop, sel_body, 0)
    o_slc = finish(acc_s)

    def gate(branch):
        return jnp.concatenate(
            [gnt_ref[branch * GQA + r:branch * GQA + r + 1, :] for r in range(GQA)], axis=1)

    def emit(o_slc, o_win):
        o = gate(0) * o_cmp + gate(1) * o_slc + gate(2) * o_win
        for h in range(GQA // 2):
            pair = jnp.concatenate([o[:, (2 * h) * QT:(2 * h + 1) * QT],
                                    o[:, (2 * h + 1) * QT:(2 * h + 2) * QT]], axis=0)
            for c in range(TILES_Q):
                o_ref[c * TQ:(c + 1) * TQ, h * LANES:(h + 1) * LANES] = (
                    pair[:, c * TQ:(c + 1) * TQ].T.astype(BF16))
        return o

    bad = jnp.max(jnp.where(jnp.isfinite(emit(o_slc, o_win)), 0.0, 1.0))

    @pl.when(bad > 0.0)
    def _():
        m_run = online_tile(sel_scores, vst_ref, acc_s, kt_last, causal(kt_last), None)
        lax.fori_loop(0, kt_last, lambda kt, m_in: online_tile(sel_scores, vst_ref, acc_s, kt,
                                                               causal(kt), m_in), m_run)
        m_run = None
        for d in list(range(TILES_Q - 1, n_wt)) + list(range(TILES_Q - 1)):
            pos = (kt_last - d) * TQ + pos_col
            inside = (tq >= pos) & (tq - pos < WINDOW) & (pos >= 0)
            m_run = online_tile(win_scores, vwt_ref, acc_w, jnp.maximum(kt_last - d, 0), inside,
                                m_run)
        emit(finish(acc_s), finish(acc_w))


def _attn_call(qt, kc, vct, ks, vst, kw, vwt, gnt, ovl, onehot, cband, eye, B, S):
    T = B * S
    nq = S // QT
    ncp = S // CMP_STRIDE
    ns = S // SLC_BLOCK
    bg4 = lambda b, g, i: (b, g, 0, 0)
    bg5 = lambda b, g, i: (b, g, 0, 0, 0)
    k_spec = pl.BlockSpec((None, None, S, LANES), bg4)
    vt_spec = pl.BlockSpec((None, None, S // TQ, HEAD_DIM, TQ), bg5)
    return pl.pallas_call(
        _attn_kernel,
        grid=(B, N_KV, nq),
        in_specs=[
            pl.BlockSpec((None, GQA, HEAD_DIM, QT), lambda b, g, i: (b, g, 0, i)),
            pl.BlockSpec((None, None, ncp, LANES), bg4),
            pl.BlockSpec((None, None, HEAD_DIM, ncp), bg4),
            k_spec, vt_spec, k_spec, vt_spec,
            pl.BlockSpec((None, None, GROWS, QT), lambda b, g, i: (b, g, 0, i)),
            _const_spec((ns, ncp)),
            _const_spec((S, LANES)),
            _const_spec((2 * ncp, QT)),
            _const_spec((QT, NL)),
        ],
        out_specs=pl.BlockSpec((QT, GQA * HEAD_DIM), lambda b, g, i: (b * nq + i, g)),
        out_shape=jax.ShapeDtypeStruct((T, QW), BF16),
        scratch_shapes=[pltpu.VMEM((ns, QT), F32), pltpu.VMEM((ns, QT), jnp.int32),
                        pltpu.VMEM((HEAD_DIM, NL), F32), pltpu.VMEM((8, NL), F32),
                        pltpu.VMEM((HEAD_DIM, NL), F32), pltpu.VMEM((8, NL), F32),
                        pltpu.SMEM((S // TQ + 8,), jnp.int32)],
        compiler_params=pltpu.CompilerParams(
            dimension_semantics=("arbitrary", "arbitrary", "arbitrary"),
            vmem_limit_bytes=VMEM_LIMIT),
        name="attn",
    )(qt, kc, vct, ks, vst, kw, vwt, gnt, ovl, onehot, cband, eye)


FF_CHUNK = 512


def _mlp_kernel(x_ref, o_ref, sga_ref, ob_ref, wa_ref, wo_ref, g2_ref, wu_ref, wd_ref,
                out_ref, acc_scr):
    a = jnp.dot(o_ref[...], wa_ref[...], preferred_element_type=F32)
    mixed = sga_ref[...] * a + ob_ref[...]
    x1 = x_ref[...] + jnp.dot(mixed.astype(BF16), wo_ref[...], preferred_element_type=F32)
    h = _rms(x1, g2_ref[...]).astype(BF16)
    acc_scr[...] = x1
    for c in range(D_FF // FF_CHUNK):
        lo, hi = c * FF_CHUNK, (c + 1) * FF_CHUNK
        up = jnp.dot(h, wu_ref[:, lo:hi], preferred_element_type=F32)
        act = jnp.square(jnp.maximum(up, 0.0)).astype(BF16)
        acc_scr[...] += jnp.dot(act, wd_ref[lo:hi, :], preferred_element_type=F32)
    out_ref[...] = acc_scr[...]


def _mlp_call(x2, o_nsa, sga, ob, w_a, w_o, g2, w_up, w_down, tm):
    T = x2.shape[0]
    row = lambda i: (i, 0)
    return pl.pallas_call(
        _mlp_kernel,
        grid=(T // tm,),
        in_specs=[
            pl.BlockSpec((tm, D_MODEL), row),
            pl.BlockSpec((tm, QW), row),
            pl.BlockSpec((tm, D_MODEL), row),
            pl.BlockSpec((tm, D_MODEL), row),
            _const_spec((QW, D_MODEL)),
            _const_spec((D_MODEL, D_MODEL)),
            _const_spec((1, D_MODEL)),
            _const_spec((D_MODEL, D_FF)),
            _const_spec((D_FF, D_MODEL)),
        ],
        out_specs=pl.BlockSpec((tm, D_MODEL), row),
        out_shape=jax.ShapeDtypeStruct((T, D_MODEL), F32),
        scratch_shapes=[pltpu.VMEM((tm, D_MODEL), F32)],
        compiler_params=pltpu.CompilerParams(
            dimension_semantics=("arbitrary",), vmem_limit_bytes=VMEM_LIMIT),
        name="mlp",
    )(x2, o_nsa, sga, ob, w_a, w_o, g2, w_up, w_down)


def _pack_w_in(w_in):
    o = OFF_GN
    gates = w_in[:, o:o + 3 * N_HEADS]
    gates = gates.reshape(D_MODEL, 3, N_KV, GQA).transpose(0, 2, 1, 3)
    gates = gates.reshape(D_MODEL, N_KV, 3 * GQA)
    gates = jnp.pad(gates, ((0, 0), (0, 0), (0, HEAD_DIM - 3 * GQA))).reshape(D_MODEL, LANES)
    head = jnp.concatenate([w_in[:, :o], gates], axis=1).astype(BF16)
    return head, w_in[:, o + 3 * N_HEADS:].astype(BF16)


def _pack_cmp(w_cmp_k, w_cmp_v, cmp_pos_k, cmp_pos_v):
    wk = w_cmp_k.reshape(CMP_BLOCK, HEAD_DIM, HEAD_DIM)
    wv = w_cmp_v.reshape(CMP_BLOCK, HEAD_DIM, HEAD_DIM)
    slabs = [jnp.pad(blk.astype(BF16), ((0, 0), (0, 0), (col, CMP_OUT - col - HEAD_DIM)))
             for blk, col in ((wk, 0), (wk, LANES), (wv, 2 * LANES), (wv, 2 * LANES + HEAD_DIM))]
    pos4 = jnp.concatenate([cmp_pos_k, cmp_pos_k, cmp_pos_v, cmp_pos_v], axis=1)
    return jnp.concatenate(slabs, axis=1), pos4


def _pad_gain(g):
    return jnp.pad(g, ((0, 0), (0, LANES - HEAD_DIM)))


def _overlap(S):
    ncp = S // CMP_STRIDE
    ns = S // SLC_BLOCK
    c_start = np.arange(ncp) * CMP_STRIDE
    s_start = np.arange(ns) * SLC_BLOCK
    ov = np.clip(np.minimum(c_start[None, :] + CMP_BLOCK, s_start[:, None] + SLC_BLOCK)
                 - np.maximum(c_start[None, :], s_start[:, None]), 0, None)
    return jnp.asarray(ov.astype(np.float32) / CMP_BLOCK, dtype=BF16)


def _block_onehot(S):
    pos = np.arange(S)
    oh = np.zeros((S, LANES), np.float32)
    oh[pos, pos // SLC_BLOCK] = 1.0
    return jnp.asarray(oh, dtype=BF16)


def _cmp_band(S):
    ncp = S // CMP_STRIDE
    c_rel = np.arange(2 * ncp)[:, None] - ncp
    visible = c_rel * CMP_STRIDE + (CMP_BLOCK - 1) <= np.arange(QT)[None, :]
    return jnp.asarray(np.where(visible, 0.0, NEG_INF), dtype=F32)


def _tiled_eye():
    return jnp.asarray(np.tile(np.eye(QT, dtype=np.float32), (1, GQA)), dtype=BF16)


def _layer(x, norm1_g, w_in, q_norm_g, k_norm_g, cmp_pos_k, cmp_pos_v, w_cmp_k, w_cmp_v,
           conv_w, w_branch_a, w_branch_b, w_out, norm2_g, w_up, w_down):
    B, S, _ = x.shape
    assert S % (CMP_STRIDE * LANES) == 0 and S // SLC_BLOCK <= LANES
    assert (S // TQ) % UNROLL == 0 and S >= WINDOW
    assert TILES_Q == 1 and (S // SLC_BLOCK) % 32 == 0
    x2 = x.reshape(B * S, D_MODEL)
    w_bd, pos4 = _pack_cmp(w_cmp_k, w_cmp_v, cmp_pos_k, cmp_pos_v)
    kg = _pad_gain(k_norm_g)
    qt, kci, vci, ks, vst, kw, vwt, gnt, sga, ob = _proj_call(
        x2, norm1_g[None, :], *_pack_w_in(w_in), w_branch_b.astype(BF16),
        q_norm_g[:, None], kg, conv_w, B, S, tm=512)
    kc, vct = _cmp_call(kci, vci, pos4, w_bd, kg, B, S)
    o_nsa = _attn_call(qt, kc, vct, ks, vst, kw, vwt, gnt, _overlap(S), _block_onehot(S),
                       _cmp_band(S), _tiled_eye(), B, S)
    out = _mlp_call(x2, o_nsa, sga, ob, w_branch_a.astype(BF16), w_out.astype(BF16),
                    norm2_g[None, :], w_up.astype(BF16), w_down.astype(BF16), tm=512)
    return out.reshape(B, S, D_MODEL)


@jax.jit
def kernel(x, norm1_g, w_in, q_norm_g, k_norm_g, cmp_pos_k, cmp_pos_v, w_cmp_k, w_cmp_v,
           conv_w, w_branch_a, w_branch_b, w_out, norm2_g, w_up, w_down):
    for l in range(norm1_g.shape[0]):
        x = _layer(x, norm1_g[l], w_in[l], q_norm_g[l], k_norm_g[l], cmp_pos_k[l],
                   cmp_pos_v[l], w_cmp_k[l], w_cmp_v[l], conv_w[l], w_branch_a[l],
                   w_branch_b[l], w_out[l], norm2_g[l], w_up[l], w_down[l])
    return x
```

```python
import numpy as np
import jax
import jax.numpy as jnp
from jax import lax
from jax.experimental import pallas as pl
from jax.experimental.pallas import tpu as pltpu

D_MODEL = 1024
N_HEADS = 8
HEAD_DIM = 64
N_KV = 2
GQA = N_HEADS // N_KV
CMP_BLOCK = 32
CMP_STRIDE = 16
SLC_BLOCK = 64
N_SLC = 16
WINDOW = 512
FORCE_SCORE = 1e4
SCALE = 0.125
CONV_K = 3
D_FF = 4 * D_MODEL
EPS = 1e-6
NEG_INF = -1e30

LANES = 128
QW = N_HEADS * HEAD_DIM
KVW = N_KV * HEAD_DIM

OFF_Q = 0
OFF_KC = OFF_Q + QW
OFF_VC = OFF_KC + KVW
OFF_KS = OFF_VC + KVW
OFF_VS = OFF_KS + KVW
OFF_KW = OFF_VS + KVW
OFF_VW = OFF_KW + KVW
OFF_GN = OFF_VW + KVW
PA_COLS = OFF_GN + LANES
OFF_CONV_B = PA_COLS
OFF_CONV_C = OFF_CONV_B + D_MODEL
OFF_CONV_X = OFF_CONV_C + D_MODEL
OFF_GATE_A = OFF_CONV_X + D_MODEL
OFF_GATE_B = OFF_GATE_A + D_MODEL
W_IN_COLS = OFF_GATE_B + D_MODEL

PROJ_SUB = 256
TQ = 256
QT = 256
QT_SHIFT = QT.bit_length() - 1
TILES_Q = QT // TQ
NL = GQA * QT
UNROLL = 4
LOOKAHEAD = 1
GROWS = 16
CMP_OUT = 2 * LANES + KVW
VMEM_LIMIT = 56 * 1024 * 1024

F32 = jnp.float32
BF16 = jnp.bfloat16


def _rms(x, g):
    return x * lax.rsqrt(jnp.mean(x * x, axis=-1, keepdims=True) + EPS) * g


def _rms_padded(x, g):
    ms = jnp.sum(x * x, axis=-1, keepdims=True) * (1.0 / HEAD_DIM)
    return x * lax.rsqrt(ms + EPS) * g


def _alibi_key_lanes(pos):
    lane = lax.broadcasted_iota(jnp.int32, (pos.shape[0], LANES), 1)
    hi = (pos >> 6).astype(F32)
    lo = (pos & 63).astype(F32)
    return jnp.where((lane == HEAD_DIM) | (lane == HEAD_DIM + 1) | (lane == HEAD_DIM + 4), 1.0,
                     jnp.where(lane == HEAD_DIM + 2, hi,
                               jnp.where(lane == HEAD_DIM + 3, lo, 0.0)))


def _const_spec(shape):
    zeros = (0,) * len(shape)
    return pl.BlockSpec(shape, lambda *_: zeros, pipeline_mode=pl.Buffered(1))


def _proj_kernel(x_ref, g1_ref, wh_ref, wt_ref, wb_ref, qg_ref, kg_ref, cw_ref,
                 qt_ref, kci_ref, vci_ref, ks_ref, vst_ref, kw_ref, vwt_ref, gnt_ref,
                 sga_ref, ob_ref, u_scr):
    i = pl.program_id(1)
    tm = x_ref.shape[0]

    @pl.when(i == 0)
    def _():
        u_scr[0:8, :] = jnp.zeros((8, D_MODEL), F32)

    for r0 in range(0, tm, PROJ_SUB):
        _proj_rows(r0, i * tm + r0, x_ref, g1_ref, wh_ref, wt_ref, wb_ref, qg_ref, kg_ref, cw_ref,
                   qt_ref, kci_ref, vci_ref, ks_ref, vst_ref, kw_ref, vwt_ref, gnt_ref,
                   sga_ref, ob_ref, u_scr)
    u_scr[0:8, :] = u_scr[tm:tm + 8, :]


def _proj_rows(r0, pos0, x_ref, g1_ref, wh_ref, wt_ref, wb_ref, qg_ref, kg_ref, cw_ref,
               qt_ref, kci_ref, vci_ref, ks_ref, vst_ref, kw_ref, vwt_ref, gnt_ref,
               sga_ref, ob_ref, u_scr):
    n = PROJ_SUB
    rows = slice(r0, r0 + n)
    xn = _rms(x_ref[rows, :], g1_ref[...]).astype(BF16)

    pa = jnp.dot(xn, wh_ref[...], preferred_element_type=F32)

    qg = qg_ref[...]
    for pair in range(N_HEADS // 2):
        qp = pa[:, OFF_Q + pair * LANES:OFF_Q + (pair + 1) * LANES].T
        for half in range(2):
            qh = qp[half * HEAD_DIM:(half + 1) * HEAD_DIM, :]
            ms = jnp.mean(qh * qh, axis=0, keepdims=True)
            qt_ref[2 * pair + half, :, rows] = (qh * lax.rsqrt(ms + EPS) * qg * SCALE).astype(BF16)

    kci_ref[rows, :] = pa[:, OFF_KC:OFF_KC + KVW]
    vci_ref[rows, :] = pa[:, OFF_VC:OFF_VC + KVW]

    pos = pos0 + lax.broadcasted_iota(jnp.int32, (n, 1), 0)
    key_lanes = _alibi_key_lanes(pos)
    low = lax.broadcasted_iota(jnp.int32, (n, LANES), 1) < HEAD_DIM
    ks2 = pa[:, OFF_KS:OFF_KS + KVW]
    kw2 = pa[:, OFF_KW:OFF_KW + KVW]
    vs_t = pa[:, OFF_VS:OFF_VS + KVW].T.astype(BF16)
    vw_t = pa[:, OFF_VW:OFF_VW + KVW].T.astype(BF16)
    gates_t = jax.nn.sigmoid(pa[:, OFF_GN:OFF_GN + LANES]).T
    for g in range(N_KV):
        ks = jnp.where(low, ks2 if g == 0 else pltpu.roll(ks2, HEAD_DIM, axis=1), 0.0)
        kw = jnp.where(low, kw2 if g == 0 else pltpu.roll(kw2, HEAD_DIM, axis=1), 0.0)
        ks_ref[g, rows, :] = (_rms_padded(ks, kg_ref[1:2, :]) + key_lanes).astype(BF16)
        kw_ref[g, rows, :] = (_rms_padded(kw, kg_ref[2:3, :]) + key_lanes).astype(BF16)
        for c in range(n // TQ):
            ct = r0 // TQ + c
            vst_ref[g, ct] = vs_t[g * HEAD_DIM:(g + 1) * HEAD_DIM, c * TQ:(c + 1) * TQ]
            vwt_ref[g, ct] = vw_t[g * HEAD_DIM:(g + 1) * HEAD_DIM, c * TQ:(c + 1) * TQ]
        gnt_ref[g, :, rows] = gates_t[g * HEAD_DIM:g * HEAD_DIM + GROWS, :]

    def col(o):
        return jnp.dot(xn, wt_ref[:, o - PA_COLS:o - PA_COLS + D_MODEL],
                       preferred_element_type=F32)

    u = col(OFF_CONV_C) * col(OFF_CONV_X)
    u_scr[8 + r0:8 + r0 + n, :] = u
    cw = cw_ref[...]
    y = (cw[2:3, :] * u + cw[1:2, :] * u_scr[7 + r0:7 + r0 + n, :]
         + cw[0:1, :] * u_scr[6 + r0:6 + r0 + n, :])
    z = col(OFF_CONV_B) * y
    zb = jnp.dot(z.astype(BF16), wb_ref[...], preferred_element_type=F32)
    ob_ref[rows, :] = jax.nn.sigmoid(col(OFF_GATE_B)) * zb
    sga_ref[rows, :] = jax.nn.sigmoid(col(OFF_GATE_A))


def _proj_call(x2, g1, w_head, w_tail, w_b, qg, kg, cw, B, S, tm):
    T = B * S
    nt = S // tm
    row = lambda b, i: (b * nt + i, 0)
    k_shape = jax.ShapeDtypeStruct((B, N_KV, S, LANES), BF16)
    k_spec = pl.BlockSpec((None, N_KV, tm, LANES), lambda b, i: (b, 0, i, 0))
    vt_shape = jax.ShapeDtypeStruct((B, N_KV, S // TQ, HEAD_DIM, TQ), BF16)
    vt_spec = pl.BlockSpec((None, N_KV, tm // TQ, HEAD_DIM, TQ), lambda b, i: (b, 0, i, 0, 0))
    return pl.pallas_call(
        _proj_kernel,
        grid=(B, nt),
        in_specs=[
            pl.BlockSpec((tm, D_MODEL), row),
            _const_spec((1, D_MODEL)),
            _const_spec((D_MODEL, PA_COLS)),
            _const_spec((D_MODEL, W_IN_COLS - PA_COLS)),
            _const_spec((D_MODEL, D_MODEL)),
            _const_spec((HEAD_DIM, 1)),
            _const_spec((3, LANES)),
            _const_spec((CONV_K, D_MODEL)),
        ],
        out_specs=[
            pl.BlockSpec((None, N_HEADS, HEAD_DIM, tm), lambda b, i: (b, 0, 0, i)),
            pl.BlockSpec((tm, KVW), row),
            pl.BlockSpec((tm, KVW), row),
            k_spec, vt_spec, k_spec, vt_spec,
            pl.BlockSpec((None, N_KV, GROWS, tm), lambda b, i: (b, 0, 0, i)),
            pl.BlockSpec((tm, D_MODEL), row),
            pl.BlockSpec((tm, D_MODEL), row),
        ],
        out_shape=[
            jax.ShapeDtypeStruct((B, N_HEADS, HEAD_DIM, S), BF16),
            jax.ShapeDtypeStruct((T, KVW), F32),
            jax.ShapeDtypeStruct((T, KVW), F32),
            k_shape, vt_shape, k_shape, vt_shape,
            jax.ShapeDtypeStruct((B, N_KV, GROWS, S), F32),
            jax.ShapeDtypeStruct((T, D_MODEL), F32),
            jax.ShapeDtypeStruct((T, D_MODEL), F32),
        ],
        scratch_shapes=[pltpu.VMEM((tm + 8, D_MODEL), F32)],
        compiler_params=pltpu.CompilerParams(
            dimension_semantics=("arbitrary", "arbitrary"),
            vmem_limit_bytes=VMEM_LIMIT),
        name="proj",
    )(x2, g1, w_head, w_tail, w_b, qg, kg, cw)


def _cmp_kernel(kci_ref, vci_ref, pos_ref, w_ref, kg_ref, kc_ref, vct_ref, b_scr):
    ncp = kc_ref.shape[1]
    acc_a = jnp.zeros((ncp, CMP_OUT), F32)
    acc_b = jnp.zeros((ncp, CMP_OUT), F32)
    for l in range(CMP_STRIDE):
        rows = jnp.concatenate([kci_ref[pl.ds(l, ncp, stride=CMP_STRIDE), :],
                                vci_ref[pl.ds(l, ncp, stride=CMP_STRIDE), :]], axis=1)
        xa = (rows + pos_ref[l:l + 1, :]).astype(BF16)
        xb = (rows + pos_ref[CMP_STRIDE + l:CMP_STRIDE + l + 1, :]).astype(BF16)
        acc_a = acc_a + jnp.dot(xa, w_ref[l], preferred_element_type=F32)
        acc_b = acc_b + jnp.dot(xb, w_ref[CMP_STRIDE + l], preferred_element_type=F32)
    b_scr[0:ncp, :] = acc_b
    b_scr[ncp:ncp + 8, :] = jnp.zeros((8, CMP_OUT), F32)
    kcv = acc_a + b_scr[1:ncp + 1, :]
    c_end = lax.broadcasted_iota(jnp.int32, (ncp, 1), 0) * CMP_STRIDE + (CMP_BLOCK - 1)
    key_lanes = _alibi_key_lanes(c_end)
    vc_t = kcv[:, 2 * LANES:CMP_OUT].T.astype(BF16)
    for g in range(N_KV):
        kc = _rms_padded(kcv[:, g * LANES:(g + 1) * LANES], kg_ref[0:1, :])
        kc_ref[g] = (kc + key_lanes).astype(BF16)
        vct_ref[g] = vc_t[g * HEAD_DIM:(g + 1) * HEAD_DIM, :]


def _cmp_call(kci, vci, pos4, w_bd, kg, B, S):
    ncp = S // CMP_STRIDE
    return pl.pallas_call(
        _cmp_kernel,
        grid=(B,),
        in_specs=[
            pl.BlockSpec((S, KVW), lambda b: (b, 0)),
            pl.BlockSpec((S, KVW), lambda b: (b, 0)),
            _const_spec((CMP_BLOCK, 2 * KVW)),
            _const_spec((CMP_BLOCK, 2 * KVW, CMP_OUT)),
            _const_spec((3, LANES)),
        ],
        out_specs=[
            pl.BlockSpec((None, N_KV, ncp, LANES), lambda b: (b, 0, 0, 0)),
            pl.BlockSpec((None, N_KV, HEAD_DIM, ncp), lambda b: (b, 0, 0, 0)),
        ],
        out_shape=[
            jax.ShapeDtypeStruct((B, N_KV, ncp, LANES), BF16),
            jax.ShapeDtypeStruct((B, N_KV, HEAD_DIM, ncp), BF16),
        ],
        scratch_shapes=[pltpu.VMEM((ncp + 8, CMP_OUT), F32)],
        compiler_params=pltpu.CompilerParams(
            dimension_semantics=("arbitrary",), vmem_limit_bytes=VMEM_LIMIT),
        name="compress",
    )(kci, vci, pos4, w_bd, kg)


def _attn_kernel(qt_ref, kc_ref, vct_ref, ks_ref, vst_ref, kw_ref, vwt_ref, gnt_ref,
                 ovl_ref, oh_ref, cband_ref, eye_ref, o_ref, score_scr, cnt_scr,
                 num_s, den_s, num_w, den_w, tiles_smem):
    acc_s = (num_s, den_s)
    acc_w = (num_w, den_w)
    g = pl.program_id(1)
    i = pl.program_id(2)
    t0 = i * QT
    kt_last = TILES_Q * (i + 1) - 1
    ns = ovl_ref.shape[0]
    ncp = ovl_ref.shape[1]

    lane = lax.broadcasted_iota(jnp.int32, (1, NL), 1)
    tq = t0 + (lane & (QT - 1))
    head = g * GQA + (lane >> QT_SHIFT) + 1
    slope = lax.bitcast_convert_type((127 - head) << 23, F32)
    a_t = (tq >> 6).astype(F32)
    b_t = (tq & 63).astype(F32)
    sub = lax.broadcasted_iota(jnp.int32, (HEAD_DIM, NL), 0)
    pos_col = lax.broadcasted_iota(jnp.int32, (TQ, 1), 0)
    qt = jnp.concatenate([qt_ref[r] for r in range(GQA)], axis=1)
    alibi_rows = jnp.where(sub == 0, -(slope * 64.0) * a_t,
                           jnp.where(sub == 1, -slope * b_t,
                                     jnp.where(sub == 2, slope * 64.0,
                                               jnp.where(sub == 3, slope, 0.0))))

    def aug_query(ref):
        rows = jnp.where(sub == 4, -ref, alibi_rows)
        return jnp.concatenate([qt, rows.astype(BF16)], axis=0)

    def tile(k_ref, kt):
        return k_ref[pl.ds(pl.multiple_of(kt * TQ, TQ), TQ), :]

    def causal(kt):
        return tq >= (kt * TQ + pos_col)

    def self_score(k_ref):
        k_t = jnp.concatenate([tile(k_ref, TILES_Q * i + c).astype(F32).T[0:HEAD_DIM, :]
                               for c in range(TILES_Q)], axis=1)
        return jnp.sum(qt.astype(F32) * jnp.concatenate([k_t] * GQA, axis=1),
                       axis=0, keepdims=True)

    def fold8(p):
        out = p[0:8, :]
        for k in range(1, TQ // 8):
            out = out + p[8 * k:8 * k + 8, :]
        return out

    def plain_tiles(score_fn, vt_ref, state, kts, masks, start, hook=None, lookahead=LOOKAHEAD):
        acc_ref, den_ref = state
        ahead = [score_fn(kt) for kt in kts[:lookahead]]
        if hook is not None:
            hook()
        total = None if start else acc_ref[...]
        den = None if start else den_ref[...]
        for n, (kt, mk) in enumerate(zip(kts, masks)):
            sc = ahead.pop(0)
            if n + lookahead < len(kts):
                ahead.append(score_fn(kts[n + lookahead]))
            if mk is not None:
                sc = jnp.where(mk, sc, NEG_INF)
            pr = jnp.exp(sc)
            pv = jnp.dot(vt_ref[kt], pr.astype(BF16), preferred_element_type=F32)
            total = pv if total is None else total + pv
            den = fold8(pr) if den is None else den + fold8(pr)
        acc_ref[...] = total
        den_ref[...] = den

    def online_tile(score_fn, vt_ref, state, kt, mask, m_run):
        acc_ref, den_ref = state
        sc = jnp.where(mask, score_fn(kt), NEG_INF)
        mx = jnp.max(sc, axis=0, keepdims=True)
        m_new = mx if m_run is None else jnp.maximum(m_run, mx)
        pr = jnp.exp(sc - m_new)
        pv = jnp.dot(vt_ref[kt], pr.astype(BF16), preferred_element_type=F32)
        if m_run is None:
            acc_ref[...] = pv
            den_ref[...] = fold8(pr)
        else:
            alpha = jnp.exp(m_run - m_new)
            acc_ref[...] = alpha * acc_ref[...] + pv
            den_ref[...] = alpha * den_ref[...] + fold8(pr)
        return m_new

    def finish(state):
        acc_ref, den_ref = state
        den = jnp.sum(den_ref[...], axis=0, keepdims=True)
        return acc_ref[...] / jnp.maximum(den, 1e-30)

    band = cband_ref[pl.ds(pl.multiple_of(ncp - (QT // CMP_STRIDE) * i, 8), ncp), :]
    lhs_c = jnp.concatenate([kc_ref[...], band.astype(BF16)], axis=1)
    rhs_c = jnp.concatenate([aug_query(0.0), eye_ref[...]], axis=0)
    has_cmp = tq >= CMP_BLOCK - 1
    qa_win = aug_query(self_score(kw_ref))

    def win_scores(kt):
        return jnp.dot(tile(kw_ref, kt), qa_win, preferred_element_type=F32)

    def cmp_and_window(win_kts, win_masks):
        s = jnp.dot(lhs_c, rhs_c, preferred_element_type=F32)
        out = []

        def cmp_rest():
            e = jnp.exp(s - jnp.max(s, axis=0, keepdims=True))
            den = jnp.maximum(jnp.sum(e, axis=0, keepdims=True), 1e-30)
            p = (e * jnp.where(has_cmp, 1.0 / den, 0.0)).astype(BF16)
            out.append(jnp.dot(vct_ref[...], p, preferred_element_type=F32))
            out.append(jnp.dot(ovl_ref[...], p, preferred_element_type=F32))

        plain_tiles(win_scores, vwt_ref, acc_w, win_kts, win_masks, True, hook=cmp_rest)
        return out[0], out[1]

    n_win = WINDOW // TQ
    n_wt = n_win + TILES_Q

    def win_steady():
        kts = [kt_last - d for d in range(n_wt)]
        masks = [causal(kt) for kt in kts[:TILES_Q]] + [None] * (n_win - TILES_Q)
        masks += [(tq - (kt * TQ + pos_col)) < WINDOW for kt in kts[n_win:]]
        return cmp_and_window(kts, masks)

    def win_start():
        kts = list(range(n_wt - 1))
        return cmp_and_window(kts, [causal(kt) for kt in kts])

    o_cmp, imp4 = lax.cond(kt_last >= n_wt - 1, win_steady, win_start)
    o_win = finish(acc_w)

    imp = imp4[:, 0:QT]
    for r in range(1, GQA):
        imp = imp + imp4[:, r * QT:(r + 1) * QT]
    j = lax.broadcasted_iota(jnp.int32, (ns, QT), 0)
    cur = (t0 + lax.broadcasted_iota(jnp.int32, (ns, QT), 1)) >> 6
    forced = (j == 0) | (j == cur) | (j == cur - 1)
    score = jnp.where(forced, FORCE_SCORE, jnp.where(j <= cur, imp, -1.0))
    score_scr[...] = score
    cnt_scr[...] = jnp.zeros((ns, QT), jnp.int32)

    row8 = lax.broadcasted_iota(jnp.int32, (8, QT), 0)
    for kg in range(ns // 8):
        @pl.when(8 * kg <= (TQ // SLC_BLOCK) * (kt_last + 1) - 1)
        def _(kg=kg):
            cnt = [cnt_scr[8 * v:8 * v + 8, :] for v in range(ns // 8)]
            for b in range(8 * kg, 8 * kg + 8):
                sb = score_scr[b:b + 1, :]
                for v in range(ns // 8):
                    sc_v = score[8 * v:8 * v + 8, :]
                    if v < kg:
                        ahead = sb > sc_v
                    elif v > kg:
                        ahead = sb >= sc_v
                    else:
                        ahead = (sb > sc_v) | ((sb == sc_v) & (row8 > b - 8 * kg))
                    cnt[v] = cnt[v] + jnp.where(ahead, 1, 0)
            for v in range(ns // 8):
                cnt_scr[8 * v:8 * v + 8, :] = cnt[v]

    sel = (cnt_scr[...] < N_SLC) & (j <= cur)
    selb = jnp.where(sel, 0.0, NEG_INF).astype(BF16)
    selb = jnp.concatenate([selb, jnp.zeros((LANES - ns, QT), BF16)], axis=0)

    sel_rows = jnp.concatenate([selb] * GQA, axis=1)
    qa_sel = jnp.concatenate([aug_query(self_score(ks_ref)), sel_rows], axis=0)

    def sel_scores(kt):
        lhs = jnp.concatenate([tile(ks_ref, kt), tile(oh_ref, kt)], axis=1)
        return jnp.dot(lhs, qa_sel, preferred_element_type=F32)

    used = jnp.max(jnp.where(sel, 1.0, 0.0), axis=1, keepdims=True)
    row = lax.broadcasted_iota(jnp.int32, (ns, 1), 0)
    bits = jnp.where(used > 0.0, jnp.left_shift(1, row & 31), 0)
    words = [jnp.sum(bits[32 * w:32 * (w + 1), :]) for w in range(ns // 32)]
    blocks_per_tile = TQ // SLC_BLOCK
    n_act = jnp.int32(0)
    for kt in range(ns // blocks_per_tile - 1):
        field = (words[(kt * blocks_per_tile) // 32] >> ((kt * blocks_per_tile) % 32))
        active = ((field & (2 ** blocks_per_tile - 1)) != 0) & (kt < kt_last)
        tiles_smem[n_act] = kt
        n_act = n_act + active.astype(jnp.int32)
    tiles_smem[n_act] = kt_last

    n_loop = jnp.maximum(n_act // UNROLL - 1, 0)
    base = n_loop * UNROLL
    for v in range(2 * UNROLL):
        @pl.when(n_act - base == v)
        def _(v=v):
            kts = [tiles_smem[base + u] for u in range(v + 1)]
            plain_tiles(sel_scores, vst_ref, acc_s, kts, [None] * v + [causal(kts[v])], True)

    def sel_body(grp, carry):
        kts = [tiles_smem[grp * UNROLL + u] for u in range(UNROLL)]
        plain_tiles(sel_scores, vst_ref, acc_s, kts, [None] * UNROLL, False)
        return carry

    lax.fori_loop(0, n_loop, sel_body, 0)
    o_slc = finish(acc_s)

    def gate(branch):
        return jnp.concatenate(
            [gnt_ref[branch * GQA + r:branch * GQA + r + 1, :] for r in range(GQA)], axis=1)

    def emit(o_slc, o_win):
        o = gate(0) * o_cmp + gate(1) * o_slc + gate(2) * o_win
        for h in range(GQA // 2):
            pair = jnp.concatenate([o[:, (2 * h) * QT:(2 * h + 1) * QT],
                                    o[:, (2 * h + 1) * QT:(2 * h + 2) * QT]], axis=0)
            for c in range(TILES_Q):
                o_ref[c * TQ:(c + 1) * TQ, h * LANES:(h + 1) * LANES] = (
                    pair[:, c * TQ:(c + 1) * TQ].T.astype(BF16))
        return o

    bad = jnp.max(jnp.where(jnp.isfinite(emit(o_slc, o_win)), 0.0, 1.0))

    @pl.when(bad > 0.0)
    def _():
        m_run = online_tile(sel_scores, vst_ref, acc_s, kt_last, causal(kt_last), None)
        lax.fori_loop(0, kt_last, lambda kt, m_in: online_tile(sel_scores, vst_ref, acc_s, kt,
                                                               causal(kt), m_in), m_run)
        m_run = None
        for d in list(range(TILES_Q - 1, n_wt)) + list(range(TILES_Q - 1)):
            pos = (kt_last - d) * TQ + pos_col
            inside = (tq >= pos) & (tq - pos < WINDOW) & (pos >= 0)
            m_run = online_tile(win_scores, vwt_ref, acc_w, jnp.maximum(kt_last - d, 0), inside,
                                m_run)
        emit(finish(acc_s), finish(acc_w))


def _attn_call(qt, kc, vct, ks, vst, kw, vwt, gnt, ovl, onehot, cband, eye, B, S):
    T = B * S
    nq = S // QT
    ncp = S // CMP_STRIDE
    ns = S // SLC_BLOCK
    bg4 = lambda b, g, i: (b, g, 0, 0)
    bg5 = lambda b, g, i: (b, g, 0, 0, 0)
    k_spec = pl.BlockSpec((None, None, S, LANES), bg4)
    vt_spec = pl.BlockSpec((None, None, S // TQ, HEAD_DIM, TQ), bg5)
    return pl.pallas_call(
        _attn_kernel,
        grid=(B, N_KV, nq),
        in_specs=[
            pl.BlockSpec((None, GQA, HEAD_DIM, QT), lambda b, g, i: (b, g, 0, i)),
            pl.BlockSpec((None, None, ncp, LANES), bg4),
            pl.BlockSpec((None, None, HEAD_DIM, ncp), bg4),
            k_spec, vt_spec, k_spec, vt_spec,
            pl.BlockSpec((None, None, GROWS, QT), lambda b, g, i: (b, g, 0, i)),
            _const_spec((ns, ncp)),
            _const_spec((S, LANES)),
            _const_spec((2 * ncp, QT)),
            _const_spec((QT, NL)),
        ],
        out_specs=pl.BlockSpec((QT, GQA * HEAD_DIM), lambda b, g, i: (b * nq + i, g)),
        out_shape=jax.ShapeDtypeStruct((T, QW), BF16),
        scratch_shapes=[pltpu.VMEM((ns, QT), F32), pltpu.VMEM((ns, QT), jnp.int32),
                        pltpu.VMEM((HEAD_DIM, NL), F32), pltpu.VMEM((8, NL), F32),
                        pltpu.VMEM((HEAD_DIM, NL), F32), pltpu.VMEM((8, NL), F32),
                        pltpu.SMEM((S // TQ + 8,), jnp.int32)],
        compiler_params=pltpu.CompilerParams(
            dimension_semantics=("arbitrary", "arbitrary", "arbitrary"),
            vmem_limit_bytes=VMEM_LIMIT),
        name="attn",
    )(qt, kc, vct, ks, vst, kw, vwt, gnt, ovl, onehot, cband, eye)


FF_CHUNK = 512


def _mlp_kernel(x_ref, o_ref, sga_ref, ob_ref, wa_ref, wo_ref, g2_ref, wu_ref, wd_ref,
                out_ref, acc_scr):
    a = jnp.dot(o_ref[...], wa_ref[...], preferred_element_type=F32)
    mixed = sga_ref[...] * a + ob_ref[...]
    x1 = x_ref[...] + jnp.dot(mixed.astype(BF16), wo_ref[...], preferred_element_type=F32)
    h = _rms(x1, g2_ref[...]).astype(BF16)
    acc_scr[...] = x1
    for c in range(D_FF // FF_CHUNK):
        lo, hi = c * FF_CHUNK, (c + 1) * FF_CHUNK
        up = jnp.dot(h, wu_ref[:, lo:hi], preferred_element_type=F32)
        act = jnp.square(jnp.maximum(up, 0.0)).astype(BF16)
        acc_scr[...] += jnp.dot(act, wd_ref[lo:hi, :], preferred_element_type=F32)
    out_ref[...] = acc_scr[...]


def _mlp_call(x2, o_nsa, sga, ob, w_a, w_o, g2, w_up, w_down, tm):
    T = x2.shape[0]
    row = lambda i: (i, 0)
    return pl.pallas_call(
        _mlp_kernel,
        grid=(T // tm,),
        in_specs=[
            pl.BlockSpec((tm, D_MODEL), row),
            pl.BlockSpec((tm, QW), row),
            pl.BlockSpec((tm, D_MODEL), row),
            pl.BlockSpec((tm, D_MODEL), row),
            _const_spec((QW, D_MODEL)),
            _const_spec((D_MODEL, D_MODEL)),
            _const_spec((1, D_MODEL)),
            _const_spec((D_MODEL, D_FF)),
            _const_spec((D_FF, D_MODEL)),
        ],
        out_specs=pl.BlockSpec((tm, D_MODEL), row),
        out_shape=jax.ShapeDtypeStruct((T, D_MODEL), F32),
        scratch_shapes=[pltpu.VMEM((tm, D_MODEL), F32)],
        compiler_params=pltpu.CompilerParams(
            dimension_semantics=("arbitrary",), vmem_limit_bytes=VMEM_LIMIT),
        name="mlp",
    )(x2, o_nsa, sga, ob, w_a, w_o, g2, w_up, w_down)


def _pack_w_in(w_in):
    o = OFF_GN
    gates = w_in[:, o:o + 3 * N_HEADS]
    gates = gates.reshape(D_MODEL, 3, N_KV, GQA).transpose(0, 2, 1, 3)
    gates = gates.reshape(D_MODEL, N_KV, 3 * GQA)
    gates = jnp.pad(gates, ((0, 0), (0, 0), (0, HEAD_DIM - 3 * GQA))).reshape(D_MODEL, LANES)
    head = jnp.concatenate([w_in[:, :o], gates], axis=1).astype(BF16)
    return head, w_in[:, o + 3 * N_HEADS:].astype(BF16)


def _pack_cmp(w_cmp_k, w_cmp_v, cmp_pos_k, cmp_pos_v):
    wk = w_cmp_k.reshape(CMP_BLOCK, HEAD_DIM, HEAD_DIM)
    wv = w_cmp_v.reshape(CMP_BLOCK, HEAD_DIM, HEAD_DIM)
    slabs = [jnp.pad(blk.astype(BF16), ((0, 0), (0, 0), (col, CMP_OUT - col - HEAD_DIM)))
             for blk, col in ((wk, 0), (wk, LANES), (wv, 2 * LANES), (wv, 2 * LANES + HEAD_DIM))]
    pos4 = jnp.concatenate([cmp_pos_k, cmp_pos_k, cmp_pos_v, cmp_pos_v], axis=1)
    return jnp.concatenate(slabs, axis=1), pos4


def _pad_gain(g):
    return jnp.pad(g, ((0, 0), (0, LANES - HEAD_DIM)))


def _overlap(S):
    ncp = S // CMP_STRIDE
    ns = S // SLC_BLOCK
    c_start = np.arange(ncp) * CMP_STRIDE
    s_start = np.arange(ns) * SLC_BLOCK
    ov = np.clip(np.minimum(c_start[None, :] + CMP_BLOCK, s_start[:, None] + SLC_BLOCK)
                 - np.maximum(c_start[None, :], s_start[:, None]), 0, None)
    return jnp.asarray(ov.astype(np.float32) / CMP_BLOCK, dtype=BF16)


def _block_onehot(S):
    pos = np.arange(S)
    oh = np.zeros((S, LANES), np.float32)
    oh[pos, pos // SLC_BLOCK] = 1.0
    return jnp.asarray(oh, dtype=BF16)


def _cmp_band(S):
    ncp = S // CMP_STRIDE
    c_rel = np.arange(2 * ncp)[:, None] - ncp
    visible = c_rel * CMP_STRIDE + (CMP_BLOCK - 1) <= np.arange(QT)[None, :]
    return jnp.asarray(np.where(visible, 0.0, NEG_INF), dtype=F32)


def _tiled_eye():
    return jnp.asarray(np.tile(np.eye(QT, dtype=np.float32), (1, GQA)), dtype=BF16)


def _layer(x, norm1_g, w_in, q_norm_g, k_norm_g, cmp_pos_k, cmp_pos_v, w_cmp_k, w_cmp_v,
           conv_w, w_branch_a, w_branch_b, w_out, norm2_g, w_up, w_down):
    B, S, _ = x.shape
    assert S % (CMP_STRIDE * LANES) == 0 and S // SLC_BLOCK <= LANES
    assert (S // TQ) % UNROLL == 0 and S >= WINDOW
    assert TILES_Q == 1 and (S // SLC_BLOCK) % 32 == 0
    x2 = x.reshape(B * S, D_MODEL)
    w_bd, pos4 = _pack_cmp(w_cmp_k, w_cmp_v, cmp_pos_k, cmp_pos_v)
    kg = _pad_gain(k_norm_g)
    qt, kci, vci, ks, vst, kw, vwt, gnt, sga, ob = _proj_call(
        x2, norm1_g[None, :], *_pack_w_in(w_in), w_branch_b.astype(BF16),
        q_norm_g[:, None], kg, conv_w, B, S, tm=1024)
    kc, vct = _cmp_call(kci, vci, pos4, w_bd, kg, B, S)
    o_nsa = _attn_call(qt, kc, vct, ks, vst, kw, vwt, gnt, _overlap(S), _block_onehot(S),
                       _cmp_band(S), _tiled_eye(), B, S)
    out = _mlp_call(x2, o_nsa, sga, ob, w_branch_a.astype(BF16), w_out.astype(BF16),
                    norm2_g[None, :], w_up.astype(BF16), w_down.astype(BF16), tm=512)
    return out.reshape(B, S, D_MODEL)


@jax.jit
def kernel(x, norm1_g, w_in, q_norm_g, k_norm_g, cmp_pos_k, cmp_pos_v, w_cmp_k, w_cmp_v,
           conv_w, w_branch_a, w_branch_b, w_out, norm2_g, w_up, w_down):
    for l in range(norm1_g.shape[0]):
        x = _layer(x, norm1_g[l], w_in[l], q_norm_g[l], k_norm_g[l], cmp_pos_k[l],
                   cmp_pos_v[l], w_cmp_k[l], w_cmp_v[l], conv_w[l], w_branch_a[l],
                   w_branch_b[l], w_out[l], norm2_g[l], w_up[l], w_down[l])
    return x
```

```python
import numpy as np
import jax
import jax.numpy as jnp
from jax import lax
from jax.experimental import pallas as pl
from jax.experimental.pallas import tpu as pltpu

D_MODEL = 1024
N_HEADS = 8
HEAD_DIM = 64
N_KV = 2
GQA = N_HEADS // N_KV
CMP_BLOCK = 32
CMP_STRIDE = 16
SLC_BLOCK = 64
N_SLC = 16
WINDOW = 512
FORCE_SCORE = 1e4
SCALE = 0.125
CONV_K = 3
D_FF = 4 * D_MODEL
EPS = 1e-6
NEG_INF = -1e30

LANES = 128
QW = N_HEADS * HEAD_DIM
KVW = N_KV * HEAD_DIM

OFF_Q = 0
OFF_KC = OFF_Q + QW
OFF_VC = OFF_KC + KVW
OFF_KS = OFF_VC + KVW
OFF_VS = OFF_KS + KVW
OFF_KW = OFF_VS + KVW
OFF_VW = OFF_KW + KVW
OFF_GN = OFF_VW + KVW
PA_COLS = OFF_GN + LANES
OFF_CONV_B = PA_COLS
OFF_CONV_C = OFF_CONV_B + D_MODEL
OFF_CONV_X = OFF_CONV_C + D_MODEL
OFF_GATE_A = OFF_CONV_X + D_MODEL
OFF_GATE_B = OFF_GATE_A + D_MODEL
W_IN_COLS = OFF_GATE_B + D_MODEL

PROJ_SUB = 256
TQ = 256
QT = 256
QT_SHIFT = QT.bit_length() - 1
TILES_Q = QT // TQ
NL = GQA * QT
UNROLL = 4
LOOKAHEAD = 1
GROWS = 16
CMP_OUT = 2 * LANES + KVW
VMEM_LIMIT = 56 * 1024 * 1024

F32 = jnp.float32
BF16 = jnp.bfloat16


def _rms(x, g):
    return x * lax.rsqrt(jnp.mean(x * x, axis=-1, keepdims=True) + EPS) * g


def _rms_padded(x, g):
    ms = jnp.sum(x * x, axis=-1, keepdims=True) * (1.0 / HEAD_DIM)
    return x * lax.rsqrt(ms + EPS) * g


def _alibi_key_lanes(pos):
    lane = lax.broadcasted_iota(jnp.int32, (pos.shape[0], LANES), 1)
    hi = (pos >> 6).astype(F32)
    lo = (pos & 63).astype(F32)
    return jnp.where((lane == HEAD_DIM) | (lane == HEAD_DIM + 1) | (lane == HEAD_DIM + 4), 1.0,
                     jnp.where(lane == HEAD_DIM + 2, hi,
                               jnp.where(lane == HEAD_DIM + 3, lo, 0.0)))


def _const_spec(shape):
    zeros = (0,) * len(shape)
    return pl.BlockSpec(shape, lambda *_: zeros, pipeline_mode=pl.Buffered(1))


def _proj_kernel(x_ref, g1_ref, wh_ref, wt_ref, wb_ref, qg_ref, kg_ref, cw_ref,
                 qt_ref, kci_ref, vci_ref, ks_ref, vst_ref, kw_ref, vwt_ref, gnt_ref,
                 sga_ref, ob_ref, u_scr):
    i = pl.program_id(1)
    tm = x_ref.shape[0]

    @pl.when(i == 0)
    def _():
        u_scr[0:8, :] = jnp.zeros((8, D_MODEL), F32)

    for r0 in range(0, tm, PROJ_SUB):
        _proj_rows(r0, i * tm + r0, x_ref, g1_ref, wh_ref, wt_ref, wb_ref, qg_ref, kg_ref, cw_ref,
                   qt_ref, kci_ref, vci_ref, ks_ref, vst_ref, kw_ref, vwt_ref, gnt_ref,
                   sga_ref, ob_ref, u_scr)
    u_scr[0:8, :] = u_scr[tm:tm + 8, :]


def _proj_rows(r0, pos0, x_ref, g1_ref, wh_ref, wt_ref, wb_ref, qg_ref, kg_ref, cw_ref,
               qt_ref, kci_ref, vci_ref, ks_ref, vst_ref, kw_ref, vwt_ref, gnt_ref,
               sga_ref, ob_ref, u_scr):
    n = PROJ_SUB
    rows = slice(r0, r0 + n)
    xn = _rms(x_ref[rows, :], g1_ref[...]).astype(BF16)

    pa = jnp.dot(xn, wh_ref[...], preferred_element_type=F32)

    qg = qg_ref[...]
    for pair in range(N_HEADS // 2):
        qp = pa[:, OFF_Q + pair * LANES:OFF_Q + (pair + 1) * LANES].T
        for half in range(2):
            qh = qp[half * HEAD_DIM:(half + 1) * HEAD_DIM, :]
            ms = jnp.mean(qh * qh, axis=0, keepdims=True)
            qt_ref[2 * pair + half, :, rows] = (qh * lax.rsqrt(ms + EPS) * qg * SCALE).astype(BF16)

    kci_ref[rows, :] = pa[:, OFF_KC:OFF_KC + KVW]
    vci_ref[rows, :] = pa[:, OFF_VC:OFF_VC + KVW]

    pos = pos0 + lax.broadcasted_iota(jnp.int32, (n, 1), 0)
    key_lanes = _alibi_key_lanes(pos)
    low = lax.broadcasted_iota(jnp.int32, (n, LANES), 1) < HEAD_DIM
    ks2 = pa[:, OFF_KS:OFF_KS + KVW]
    kw2 = pa[:, OFF_KW:OFF_KW + KVW]
    vs_t = pa[:, OFF_VS:OFF_VS + KVW].T.astype(BF16)
    vw_t = pa[:, OFF_VW:OFF_VW + KVW].T.astype(BF16)
    gates_t = jax.nn.sigmoid(pa[:, OFF_GN:OFF_GN + LANES]).T
    for g in range(N_KV):
        ks = jnp.where(low, ks2 if g == 0 else pltpu.roll(ks2, HEAD_DIM, axis=1), 0.0)
        kw = jnp.where(low, kw2 if g == 0 else pltpu.roll(kw2, HEAD_DIM, axis=1), 0.0)
        ks_ref[g, rows, :] = (_rms_padded(ks, kg_ref[1:2, :]) + key_lanes).astype(BF16)
        kw_ref[g, rows, :] = (_rms_padded(kw, kg_ref[2:3, :]) + key_lanes).astype(BF16)
        for c in range(n // TQ):
            ct = r0 // TQ + c
            vst_ref[g, ct] = vs_t[g * HEAD_DIM:(g + 1) * HEAD_DIM, c * TQ:(c + 1) * TQ]
            vwt_ref[g, ct] = vw_t[g * HEAD_DIM:(g + 1) * HEAD_DIM, c * TQ:(c + 1) * TQ]
        gnt_ref[g, :, rows] = gates_t[g * HEAD_DIM:g * HEAD_DIM + GROWS, :]

    def col(o):
        return jnp.dot(xn, wt_ref[:, o - PA_COLS:o - PA_COLS + D_MODEL],
                       preferred_element_type=F32)

    u = col(OFF_CONV_C) * col(OFF_CONV_X)
    u_scr[8 + r0:8 + r0 + n, :] = u
    cw = cw_ref[...]
    y = (cw[2:3, :] * u + cw[1:2, :] * u_scr[7 + r0:7 + r0 + n, :]
         + cw[0:1, :] * u_scr[6 + r0:6 + r0 + n, :])
    z = col(OFF_CONV_B) * y
    zb = jnp.dot(z.astype(BF16), wb_ref[...], preferred_element_type=F32)
    ob_ref[rows, :] = jax.nn.sigmoid(col(OFF_GATE_B)) * zb
    sga_ref[rows, :] = jax.nn.sigmoid(col(OFF_GATE_A))


def _proj_call(x2, g1, w_head, w_tail, w_b, qg, kg, cw, B, S, tm):
    T = B * S
    nt = S // tm
    row = lambda b, i: (b * nt + i, 0)
    k_shape = jax.ShapeDtypeStruct((B, N_KV, S, LANES), BF16)
    k_spec = pl.BlockSpec((None, N_KV, tm, LANES), lambda b, i: (b, 0, i, 0))
    vt_shape = jax.ShapeDtypeStruct((B, N_KV, S // TQ, HEAD_DIM, TQ), BF16)
    vt_spec = pl.BlockSpec((None, N_KV, tm // TQ, HEAD_DIM, TQ), lambda b, i: (b, 0, i, 0, 0))
    return pl.pallas_call(
        _proj_kernel,
        grid=(B, nt),
        in_specs=[
            pl.BlockSpec((tm, D_MODEL), row),
            _const_spec((1, D_MODEL)),
            _const_spec((D_MODEL, PA_COLS)),
            _const_spec((D_MODEL, W_IN_COLS - PA_COLS)),
            _const_spec((D_MODEL, D_MODEL)),
            _const_spec((HEAD_DIM, 1)),
            _const_spec((3, LANES)),
            _const_spec((CONV_K, D_MODEL)),
        ],
        out_specs=[
            pl.BlockSpec((None, N_HEADS, HEAD_DIM, tm), lambda b, i: (b, 0, 0, i)),
            pl.BlockSpec((tm, KVW), row),
            pl.BlockSpec((tm, KVW), row),
            k_spec, vt_spec, k_spec, vt_spec,
            pl.BlockSpec((None, N_KV, GROWS, tm), lambda b, i: (b, 0, 0, i)),
            pl.BlockSpec((tm, D_MODEL), row),
            pl.BlockSpec((tm, D_MODEL), row),
        ],
        out_shape=[
            jax.ShapeDtypeStruct((B, N_HEADS, HEAD_DIM, S), BF16),
            jax.ShapeDtypeStruct((T, KVW), F32),
            jax.ShapeDtypeStruct((T, KVW), F32),
            k_shape, vt_shape, k_shape, vt_shape,
            jax.ShapeDtypeStruct((B, N_KV, GROWS, S), F32),
            jax.ShapeDtypeStruct((T, D_MODEL), F32),
            jax.ShapeDtypeStruct((T, D_MODEL), F32),
        ],
        scratch_shapes=[pltpu.VMEM((tm + 8, D_MODEL), F32)],
        compiler_params=pltpu.CompilerParams(
            dimension_semantics=("arbitrary", "arbitrary"),
            vmem_limit_bytes=VMEM_LIMIT),
        name="proj",
    )(x2, g1, w_head, w_tail, w_b, qg, kg, cw)


def _cmp_kernel(kci_ref, vci_ref, pos_ref, w_ref, kg_ref, kc_ref, vct_ref, b_scr):
    ncp = kc_ref.shape[1]
    acc_a = jnp.zeros((ncp, CMP_OUT), F32)
    acc_b = jnp.zeros((ncp, CMP_OUT), F32)
    for l in range(CMP_STRIDE):
        rows = jnp.concatenate([kci_ref[pl.ds(l, ncp, stride=CMP_STRIDE), :],
                                vci_ref[pl.ds(l, ncp, stride=CMP_STRIDE), :]], axis=1)
        xa = (rows + pos_ref[l:l + 1, :]).astype(BF16)
        xb = (rows + pos_ref[CMP_STRIDE + l:CMP_STRIDE + l + 1, :]).astype(BF16)
        acc_a = acc_a + jnp.dot(xa, w_ref[l], preferred_element_type=F32)
        acc_b = acc_b + jnp.dot(xb, w_ref[CMP_STRIDE + l], preferred_element_type=F32)
    b_scr[0:ncp, :] = acc_b
    b_scr[ncp:ncp + 8, :] = jnp.zeros((8, CMP_OUT), F32)
    kcv = acc_a + b_scr[1:ncp + 1, :]
    c_end = lax.broadcasted_iota(jnp.int32, (ncp, 1), 0) * CMP_STRIDE + (CMP_BLOCK - 1)
    key_lanes = _alibi_key_lanes(c_end)
    vc_t = kcv[:, 2 * LANES:CMP_OUT].T.astype(BF16)
    for g in range(N_KV):
        kc = _rms_padded(kcv[:, g * LANES:(g + 1) * LANES], kg_ref[0:1, :])
        kc_ref[g] = (kc + key_lanes).astype(BF16)
        vct_ref[g] = vc_t[g * HEAD_DIM:(g + 1) * HEAD_DIM, :]


def _cmp_call(kci, vci, pos4, w_bd, kg, B, S):
    ncp = S // CMP_STRIDE
    return pl.pallas_call(
        _cmp_kernel,
        grid=(B,),
        in_specs=[
            pl.BlockSpec((S, KVW), lambda b: (b, 0)),
            pl.BlockSpec((S, KVW), lambda b: (b, 0)),
            _const_spec((CMP_BLOCK, 2 * KVW)),
            _const_spec((CMP_BLOCK, 2 * KVW, CMP_OUT)),
            _const_spec((3, LANES)),
        ],
        out_specs=[
            pl.BlockSpec((None, N_KV, ncp, LANES), lambda b: (b, 0, 0, 0)),
            pl.BlockSpec((None, N_KV, HEAD_DIM, ncp), lambda b: (b, 0, 0, 0)),
        ],
        out_shape=[
            jax.ShapeDtypeStruct((B, N_KV, ncp, LANES), BF16),
            jax.ShapeDtypeStruct((B, N_KV, HEAD_DIM, ncp), BF16),
        ],
        scratch_shapes=[pltpu.VMEM((ncp + 8, CMP_OUT), F32)],
        compiler_params=pltpu.CompilerParams(
            dimension_semantics=("arbitrary",), vmem_limit_bytes=VMEM_LIMIT),
        name="compress",
    )(kci, vci, pos4, w_bd, kg)


def _attn_kernel(qt_ref, kc_ref, vct_ref, ks_ref, vst_ref, kw_ref, vwt_ref, gnt_ref,
                 ovl_ref, oh_ref, cband_ref, eye_ref, o_ref, score_scr, cnt_scr,
                 num_s, den_s, num_w, den_w, tiles_smem):
    acc_s = (num_s, den_s)
    acc_w = (num_w, den_w)
    g = pl.program_id(1)
    i = pl.program_id(2)
    t0 = i * QT
    kt_last = TILES_Q * (i + 1) - 1
    ns = ovl_ref.shape[0]
    ncp = ovl_ref.shape[1]

    lane = lax.broadcasted_iota(jnp.int32, (1, NL), 1)
    tq = t0 + (lane & (QT - 1))
    head = g * GQA + (lane >> QT_SHIFT) + 1
    slope = lax.bitcast_convert_type((127 - head) << 23, F32)
    a_t = (tq >> 6).astype(F32)
    b_t = (tq & 63).astype(F32)
    sub = lax.broadcasted_iota(jnp.int32, (HEAD_DIM, NL), 0)
    pos_col = lax.broadcasted_iota(jnp.int32, (TQ, 1), 0)
    qt = jnp.concatenate([qt_ref[r] for r in range(GQA)], axis=1)
    alibi_rows = jnp.where(sub == 0, -(slope * 64.0) * a_t,
                           jnp.where(sub == 1, -slope * b_t,
                                     jnp.where(sub == 2, slope * 64.0,
                                               jnp.where(sub == 3, slope, 0.0))))

    def aug_query(ref):
        rows = jnp.where(sub == 4, -ref, alibi_rows)
        return jnp.concatenate([qt, rows.astype(BF16)], axis=0)

    def tile(k_ref, kt):
        return k_ref[pl.ds(pl.multiple_of(kt * TQ, TQ), TQ), :]

    def causal(kt):
        return tq >= (kt * TQ + pos_col)

    def self_score(k_ref):
        k_t = jnp.concatenate([tile(k_ref, TILES_Q * i + c).astype(F32).T[0:HEAD_DIM, :]
                               for c in range(TILES_Q)], axis=1)
        return jnp.sum(qt.astype(F32) * jnp.concatenate([k_t] * GQA, axis=1),
                       axis=0, keepdims=True)

    def fold8(p):
        out = p[0:8, :]
        for k in range(1, TQ // 8):
            out = out + p[8 * k:8 * k + 8, :]
        return out

    def plain_tiles(score_fn, vt_ref, state, kts, masks, start, hook=None, lookahead=LOOKAHEAD):
        acc_ref, den_ref = state
        ahead = [score_fn(kt) for kt in kts[:lookahead]]
        if hook is not None:
            hook()
        total = None if start else acc_ref[...]
        den = None if start else den_ref[...]
        for n, (kt, mk) in enumerate(zip(kts, masks)):
            sc = ahead.pop(0)
            if n + lookahead < len(kts):
                ahead.append(score_fn(kts[n + lookahead]))
            if mk is not None:
                sc = jnp.where(mk, sc, NEG_INF)
            pr = jnp.exp(sc)
            pv = jnp.dot(vt_ref[kt], pr.astype(BF16), preferred_element_type=F32)
            total = pv if total is None else total + pv
            den = fold8(pr) if den is None else den + fold8(pr)
        acc_ref[...] = total
        den_ref[...] = den

    def online_tile(score_fn, vt_ref, state, kt, mask, m_run):
        acc_ref, den_ref = state
        sc = jnp.where(mask, score_fn(kt), NEG_INF)
        mx = jnp.max(sc, axis=0, keepdims=True)
        m_new = mx if m_run is None else jnp.maximum(m_run, mx)
        pr = jnp.exp(sc - m_new)
        pv = jnp.dot(vt_ref[kt], pr.astype(BF16), preferred_element_type=F32)
        if m_run is None:
            acc_ref[...] = pv
            den_ref[...] = fold8(pr)
        else:
            alpha = jnp.exp(m_run - m_new)
            acc_ref[...] = alpha * acc_ref[...] + pv
            den_ref[...] = alpha * den_ref[...] + fold8(pr)
        return m_new

    def finish(state):
        acc_ref, den_ref = state
        den = jnp.sum(den_ref[...], axis=0, keepdims=True)
        return acc_ref[...] / jnp.maximum(den, 1e-30)

    band = cband_ref[pl.ds(pl.multiple_of(ncp - (QT // CMP_STRIDE) * i, 8), ncp), :]
    lhs_c = jnp.concatenate([kc_ref[...], band.astype(BF16)], axis=1)
    rhs_c = jnp.concatenate([aug_query(0.0), eye_ref[...]], axis=0)
    has_cmp = tq >= CMP_BLOCK - 1
    qa_win = aug_query(self_score(kw_ref))

    def win_scores(kt):
        return jnp.dot(tile(kw_ref, kt), qa_win, preferred_element_type=F32)

    def cmp_and_window(win_kts, win_masks):
        s = jnp.dot(lhs_c, rhs_c, preferred_element_type=F32)
        out = []

        def cmp_rest():
            e = jnp.exp(s - jnp.max(s, axis=0, keepdims=True))
            den = jnp.maximum(jnp.sum(e, axis=0, keepdims=True), 1e-30)
            p = (e * jnp.where(has_cmp, 1.0 / den, 0.0)).astype(BF16)
            out.append(jnp.dot(vct_ref[...], p, preferred_element_type=F32))
            out.append(jnp.dot(ovl_ref[...], p, preferred_element_type=F32))

        plain_tiles(win_scores, vwt_ref, acc_w, win_kts, win_masks, True, hook=cmp_rest)
        return out[0], out[1]

    n_win = WINDOW // TQ
    n_wt = n_win + TILES_Q

    def win_steady():
        kts = [kt_last - d for d in range(n_wt)]
        masks = [causal(kt) for kt in kts[:TILES_Q]] + [None] * (n_win - TILES_Q)
        masks += [(tq - (kt * TQ + pos_col)) < WINDOW for kt in kts[n_win:]]
        return cmp_and_window(kts, masks)

    def win_start():
        kts = list(range(n_wt - 1))
        return cmp_and_window(kts, [causal(kt) for kt in kts])

    o_cmp, imp4 = lax.cond(kt_last >= n_wt - 1, win_steady, win_start)
    o_win = finish(acc_w)

    imp = imp4[:, 0:QT]
    for r in range(1, GQA):
        imp = imp + imp4[:, r * QT:(r + 1) * QT]
    j = lax.broadcasted_iota(jnp.int32, (ns, QT), 0)
    cur = (t0 + lax.broadcasted_iota(jnp.int32, (ns, QT), 1)) >> 6
    forced = (j == 0) | (j == cur) | (j == cur - 1)
    score = jnp.where(forced, FORCE_SCORE, jnp.where(j <= cur, imp, -1.0))
    score_scr[...] = score
    cnt_scr[...] = jnp.zeros((ns, QT), jnp.int32)

    row8 = lax.broadcasted_iota(jnp.int32, (8, QT), 0)
    for kg in range(ns // 8):
        @pl.when(8 * kg <= (TQ // SLC_BLOCK) * (kt_last + 1) - 1)
        def _(kg=kg):
            cnt = [cnt_scr[8 * v:8 * v + 8, :] for v in range(ns // 8)]
            for b in range(8 * kg, 8 * kg + 8):
                sb = score_scr[b:b + 1, :]
                for v in range(ns // 8):
                    sc_v = score[8 * v:8 * v + 8, :]
                    if v < kg:
                        ahead = sb > sc_v
                    elif v > kg:
                        ahead = sb >= sc_v
                    else:
                        ahead = (sb > sc_v) | ((sb == sc_v) & (row8 > b - 8 * kg))
                    cnt[v] = cnt[v] + jnp.where(ahead, 1, 0)
            for v in range(ns // 8):
                cnt_scr[8 * v:8 * v + 8, :] = cnt[v]

    sel = (cnt_scr[...] < N_SLC) & (j <= cur)
    selb = jnp.where(sel, 0.0, NEG_INF).astype(BF16)
    selb = jnp.concatenate([selb, jnp.zeros((LANES - ns, QT), BF16)], axis=0)

    sel_rows = jnp.concatenate([selb] * GQA, axis=1)
    qa_sel = jnp.concatenate([aug_query(self_score(ks_ref)), sel_rows], axis=0)

    def sel_scores(kt):
        lhs = jnp.concatenate([tile(ks_ref, kt), tile(oh_ref, kt)], axis=1)
        return jnp.dot(lhs, qa_sel, preferred_element_type=F32)

    used = jnp.max(jnp.where(sel, 1.0, 0.0), axis=1, keepdims=True)
    row = lax.broadcasted_iota(jnp.int32, (ns, 1), 0)
    bits = jnp.where(used > 0.0, jnp.left_shift(1, row & 31), 0)
    words = [jnp.sum(bits[32 * w:32 * (w + 1), :]) for w in range(ns // 32)]
    blocks_per_tile = TQ // SLC_BLOCK
    n_act = jnp.int32(0)
    for kt in range(ns // blocks_per_tile - 1):
        field = (words[(kt * blocks_per_tile) // 32] >> ((kt * blocks_per_tile) % 32))
        active = ((field & (2 ** blocks_per_tile - 1)) != 0) & (kt < kt_last)
        tiles_smem[n_act] = kt
        n_act = n_act + active.astype(jnp.int32)
    tiles_smem[n_act] = kt_last

    n_loop = jnp.maximum(n_act // UNROLL - 1, 0)
    base = n_loop * UNROLL
    for v in range(2 * UNROLL):
        @pl.when(n_act - base == v)
        def _(v=v):
            kts = [tiles_smem[base + u] for u in range(v + 1)]
            plain_tiles(sel_scores, vst_ref, acc_s, kts, [None] * v + [causal(kts[v])], True)

    def sel_body(grp, carry):
        kts = [tiles_smem[grp * UNROLL + u] for u in range(UNROLL)]
        plain_tiles(sel_scores, vst_ref, acc_s, kts, [None] * UNROLL, False)
        return carry

    lax.fori_loop(0, n_loop, sel_body, 0)
    o_slc = finish(acc_s)

    def gate(branch):
        return jnp.concatenate(
            [gnt_ref[branch * GQA + r:branch * GQA + r + 1, :] for r in range(GQA)], axis=1)

    def emit(o_slc, o_win):
        o = gate(0) * o_cmp + gate(1) * o_slc + gate(2) * o_win
        for h in range(GQA // 2):
            pair = jnp.concatenate([o[:, (2 * h) * QT:(2 * h + 1) * QT],
                                    o[:, (2 * h + 1) * QT:(2 * h + 2) * QT]], axis=0)
            for c in range(TILES_Q):
                o_ref[c * TQ:(c + 1) * TQ, h * LANES:(h + 1) * LANES] = (
                    pair[:, c * TQ:(c + 1) * TQ].T.astype(BF16))
        return o

    bad = jnp.max(jnp.where(jnp.isfinite(emit(o_slc, o_win)), 0.0, 1.0))

    @pl.when(bad > 0.0)
    def _():
        m_run = online_tile(sel_scores, vst_ref, acc_s, kt_last, causal(kt_last), None)
        lax.fori_loop(0, kt_last, lambda kt, m_in: online_tile(sel_scores, vst_ref, acc_s, kt,
                                                               causal(kt), m_in), m_run)
        m_run = None
        for d in list(range(TILES_Q - 1, n_wt)) + list(range(TILES_Q - 1)):
            pos = (kt_last - d) * TQ + pos_col
            inside = (tq >= pos) & (tq - pos < WINDOW) & (pos >= 0)
            m_run = online_tile(win_scores, vwt_ref, acc_w, jnp.maximum(kt_last - d, 0), inside,
                                m_run)
        emit(finish(acc_s), finish(acc_w))


def _attn_call(qt, kc, vct, ks, vst, kw, vwt, gnt, ovl, onehot, cband, eye, B, S):
    T = B * S
    nq = S // QT
    ncp = S // CMP_STRIDE
    ns = S // SLC_BLOCK
    bg4 = lambda b, g, i: (b, g, 0, 0)
    bg5 = lambda b, g, i: (b, g, 0, 0, 0)
    k_spec = pl.BlockSpec((None, None, S, LANES), bg4)
    vt_spec = pl.BlockSpec((None, None, S // TQ, HEAD_DIM, TQ), bg5)
    return pl.pallas_call(
        _attn_kernel,
        grid=(B, N_KV, nq),
        in_specs=[
            pl.BlockSpec((None, GQA, HEAD_DIM, QT), lambda b, g, i: (b, g, 0, i)),
            pl.BlockSpec((None, None, ncp, LANES), bg4),
            pl.BlockSpec((None, None, HEAD_DIM, ncp), bg4),
            k_spec, vt_spec, k_spec, vt_spec,
            pl.BlockSpec((None, None, GROWS, QT), lambda b, g, i: (b, g, 0, i)),
            _const_spec((ns, ncp)),
            _const_spec((S, LANES)),
            _const_spec((2 * ncp, QT)),
            _const_spec((QT, NL)),
        ],
        out_specs=pl.BlockSpec((QT, GQA * HEAD_DIM), lambda b, g, i: (b * nq + i, g)),
        out_shape=jax.ShapeDtypeStruct((T, QW), BF16),
        scratch_shapes=[pltpu.VMEM((ns, QT), F32), pltpu.VMEM((ns, QT), jnp.int32),
                        pltpu.VMEM((HEAD_DIM, NL), F32), pltpu.VMEM((8, NL), F32),
                        pltpu.VMEM((HEAD_DIM, NL), F32), pltpu.VMEM((8, NL), F32),
                        pltpu.SMEM((S // TQ + 8,), jnp.int32)],
        compiler_params=pltpu.CompilerParams(
            dimension_semantics=("arbitrary", "arbitrary", "arbitrary"),
            vmem_limit_bytes=VMEM_LIMIT),
        name="attn",
    )(qt, kc, vct, ks, vst, kw, vwt, gnt, ovl, onehot, cband, eye)


FF_CHUNK = 512


def _mlp_kernel(x_ref, o_ref, sga_ref, ob_ref, wa_ref, wo_ref, g2_ref, wu_ref, wd_ref,
                out_ref):
    a = jnp.dot(o_ref[...], wa_ref[...], preferred_element_type=F32)
    mixed = sga_ref[...] * a + ob_ref[...]
    x1 = x_ref[...] + jnp.dot(mixed.astype(BF16), wo_ref[...], preferred_element_type=F32)
    h = _rms(x1, g2_ref[...]).astype(BF16)
    acts = []
    for c in range(D_FF // FF_CHUNK):
        lo, hi = c * FF_CHUNK, (c + 1) * FF_CHUNK
        up = jnp.dot(h, wu_ref[:, lo:hi], preferred_element_type=F32)
        acts.append(jnp.square(jnp.maximum(up, 0.0)).astype(BF16))
    out_ref[...] = x1 + jnp.dot(jnp.concatenate(acts, axis=1), wd_ref[...],
                                preferred_element_type=F32)


def _mlp_call(x2, o_nsa, sga, ob, w_a, w_o, g2, w_up, w_down, tm):
    T = x2.shape[0]
    row = lambda i: (i, 0)
    return pl.pallas_call(
        _mlp_kernel,
        grid=(T // tm,),
        in_specs=[
            pl.BlockSpec((tm, D_MODEL), row),
            pl.BlockSpec((tm, QW), row),
            pl.BlockSpec((tm, D_MODEL), row),
            pl.BlockSpec((tm, D_MODEL), row),
            _const_spec((QW, D_MODEL)),
            _const_spec((D_MODEL, D_MODEL)),
            _const_spec((1, D_MODEL)),
            _const_spec((D_MODEL, D_FF)),
            _const_spec((D_FF, D_MODEL)),
        ],
        out_specs=pl.BlockSpec((tm, D_MODEL), row),
        out_shape=jax.ShapeDtypeStruct((T, D_MODEL), F32),
        compiler_params=pltpu.CompilerParams(
            dimension_semantics=("arbitrary",), vmem_limit_bytes=VMEM_LIMIT),
        name="mlp",
    )(x2, o_nsa, sga, ob, w_a, w_o, g2, w_up, w_down)


def _pack_w_in(w_in):
    o = OFF_GN
    gates = w_in[:, o:o + 3 * N_HEADS]
    gates = gates.reshape(D_MODEL, 3, N_KV, GQA).transpose(0, 2, 1, 3)
    gates = gates.reshape(D_MODEL, N_KV, 3 * GQA)
    gates = jnp.pad(gates, ((0, 0), (0, 0), (0, HEAD_DIM - 3 * GQA))).reshape(D_MODEL, LANES)
    head = jnp.concatenate([w_in[:, :o], gates], axis=1).astype(BF16)
    return head, w_in[:, o + 3 * N_HEADS:].astype(BF16)


def _pack_cmp(w_cmp_k, w_cmp_v, cmp_pos_k, cmp_pos_v):
    wk = w_cmp_k.reshape(CMP_BLOCK, HEAD_DIM, HEAD_DIM)
    wv = w_cmp_v.reshape(CMP_BLOCK, HEAD_DIM, HEAD_DIM)
    slabs = [jnp.pad(blk.astype(BF16), ((0, 0), (0, 0), (col, CMP_OUT - col - HEAD_DIM)))
             for blk, col in ((wk, 0), (wk, LANES), (wv, 2 * LANES), (wv, 2 * LANES + HEAD_DIM))]
    pos4 = jnp.concatenate([cmp_pos_k, cmp_pos_k, cmp_pos_v, cmp_pos_v], axis=1)
    return jnp.concatenate(slabs, axis=1), pos4


def _pad_gain(g):
    return jnp.pad(g, ((0, 0), (0, LANES - HEAD_DIM)))


def _overlap(S):
    ncp = S // CMP_STRIDE
    ns = S // SLC_BLOCK
    c_start = np.arange(ncp) * CMP_STRIDE
    s_start = np.arange(ns) * SLC_BLOCK
    ov = np.clip(np.minimum(c_start[None, :] + CMP_BLOCK, s_start[:, None] + SLC_BLOCK)
                 - np.maximum(c_start[None, :], s_start[:, None]), 0, None)
    return jnp.asarray(ov.astype(np.float32) / CMP_BLOCK, dtype=BF16)


def _block_onehot(S):
    pos = np.arange(S)
    oh = np.zeros((S, LANES), np.float32)
    oh[pos, pos // SLC_BLOCK] = 1.0
    return jnp.asarray(oh, dtype=BF16)


def _cmp_band(S):
    ncp = S // CMP_STRIDE
    c_rel = np.arange(2 * ncp)[:, None] - ncp
    visible = c_rel * CMP_STRIDE + (CMP_BLOCK - 1) <= np.arange(QT)[None, :]
    return jnp.asarray(np.where(visible, 0.0, NEG_INF), dtype=F32)


def _tiled_eye():
    return jnp.asarray(np.tile(np.eye(QT, dtype=np.float32), (1, GQA)), dtype=BF16)


def _layer(x, norm1_g, w_in, q_norm_g, k_norm_g, cmp_pos_k, cmp_pos_v, w_cmp_k, w_cmp_v,
           conv_w, w_branch_a, w_branch_b, w_out, norm2_g, w_up, w_down):
    B, S, _ = x.shape
    assert S % (CMP_STRIDE * LANES) == 0 and S // SLC_BLOCK <= LANES
    assert (S // TQ) % UNROLL == 0 and S >= WINDOW
    assert TILES_Q == 1 and (S // SLC_BLOCK) % 32 == 0
    x2 = x.reshape(B * S, D_MODEL)
    w_bd, pos4 = _pack_cmp(w_cmp_k, w_cmp_v, cmp_pos_k, cmp_pos_v)
    kg = _pad_gain(k_norm_g)
    qt, kci, vci, ks, vst, kw, vwt, gnt, sga, ob = _proj_call(
        x2, norm1_g[None, :], *_pack_w_in(w_in), w_branch_b.astype(BF16),
        q_norm_g[:, None], kg, conv_w, B, S, tm=1024)
    kc, vct = _cmp_call(kci, vci, pos4, w_bd, kg, B, S)
    o_nsa = _attn_call(qt, kc, vct, ks, vst, kw, vwt, gnt, _overlap(S), _block_onehot(S),
                       _cmp_band(S), _tiled_eye(), B, S)
    out = _mlp_call(x2, o_nsa, sga, ob, w_branch_a.astype(BF16), w_out.astype(BF16),
                    norm2_g[None, :], w_up.astype(BF16), w_down.astype(BF16), tm=512)
    return out.reshape(B, S, D_MODEL)


@jax.jit
def kernel(x, norm1_g, w_in, q_norm_g, k_norm_g, cmp_pos_k, cmp_pos_v, w_cmp_k, w_cmp_v,
           conv_w, w_branch_a, w_branch_b, w_out, norm2_g, w_up, w_down):
    for l in range(norm1_g.shape[0]):
        x = _layer(x, norm1_g[l], w_in[l], q_norm_g[l], k_norm_g[l], cmp_pos_k[l],
                   cmp_pos_v[l], w_cmp_k[l], w_cmp_v[l], conv_w[l], w_branch_a[l],
                   w_branch_b[l], w_out[l], norm2_g[l], w_up[l], w_down[l])
    return x
```
